```python
import jax
import jax.numpy as jnp
from jax import lax
import numpy as np

D_MODEL = 1024
BATCH = 2
SEQ = 8192
DEPTH = 2

EPS = 1e-6
ROPE_THETA = 10000.0
NEG_INF = -1e30
D_FF = 2816
N_BRANCH = 3
MIX_WIDTH = 512

MLA_HEADS = 8
MLA_Q_RANK = 256
MLA_KV_RANK = 128
MLA_NOPE = 64
MLA_ROPE = 32
MLA_V = 64
MLA_QK = MLA_NOPE + MLA_ROPE
ATTN_BLOCK_Q = 128

GDN_HEADS = 4
GDN_DK = 128
GDN_DV = 128
GDN_CONV = 4
GDN_CHUNK = 64

MOBA_HEADS = 8
MOBA_DH = 64
MOBA_BLOCK = 256
MOBA_TOPK = 3
MOBA_QCHUNK = 64

IN_SPLITS = (
    MLA_Q_RANK,
    MLA_KV_RANK,
    MLA_ROPE,
    GDN_HEADS * GDN_DK,
    GDN_HEADS * GDN_DK,
    GDN_HEADS * GDN_DV,
    GDN_HEADS,
    GDN_HEADS,
    GDN_HEADS * GDN_DV,
    3 * MOBA_HEADS * MOBA_DH,
    N_BRANCH * D_MODEL,
)
D_IN = sum(IN_SPLITS)

kernel_name = 'hybrid_mla_gdn_moba_macaron'


def split_cols(t, sizes):
    offs = np.cumsum(sizes)[:-1].tolist()
    return jnp.split(t, offs, axis=-1)


def rms_norm(x, g):
    xf = x.astype(jnp.float32)
    y = xf * lax.rsqrt(jnp.mean(xf * xf, axis=-1, keepdims=True) + EPS)
    return (y * g.astype(jnp.float32)).astype(x.dtype)


def l2norm(x):
    return x * lax.rsqrt(jnp.sum(x * x, axis=-1, keepdims=True) + EPS)


def rope(x, pos):
    d = x.shape[-1]
    half = d // 2
    inv_freq = ROPE_THETA ** (-jnp.arange(half, dtype=jnp.float32) * 2.0 / d)
    ang = pos.astype(jnp.float32)[:, None] * inv_freq[None, :]
    cos = jnp.cos(ang)[None, :, None, :]
    sin = jnp.sin(ang)[None, :, None, :]
    xf = x.astype(jnp.float32)
    x1, x2 = xf[..., :half], xf[..., half:]
    return jnp.concatenate([x1 * cos - x2 * sin, x2 * cos + x1 * sin], axis=-1).astype(x.dtype)


def swiglu(h, w_in, w_out):
    gate, up = jnp.split(h @ w_in, 2, axis=-1)
    return (jax.nn.silu(gate) * up) @ w_out


def causal_dwconv(x, w):
    c = x.shape[-1]
    k = w.shape[0]
    return lax.conv_general_dilated(
        x, w[:, None, :].astype(x.dtype), window_strides=(1,), padding=[(k - 1, 0)],
        dimension_numbers=('NWC', 'WIO', 'NWC'), feature_group_count=c)


def causal_attention_blocked(q, k, v, scale):
    b, h, s, dk = q.shape
    nb = s // ATTN_BLOCK_Q
    qb = jnp.moveaxis(q.reshape(b, h, nb, ATTN_BLOCK_Q, dk), 2, 0)
    k_pos = jnp.arange(s)

    def one_block(args):
        i, q_i = args
        logits = jnp.einsum('bhqd,bhkd->bhqk', q_i, k, preferred_element_type=jnp.float32) * scale
        q_pos = i * ATTN_BLOCK_Q + jnp.arange(ATTN_BLOCK_Q)
        logits = jnp.where(k_pos[None, :] <= q_pos[:, None], logits, NEG_INF)
        p = jax.nn.softmax(logits, axis=-1).astype(v.dtype)
        return jnp.einsum('bhqk,bhkd->bhqd', p, v)

    o = lax.map(one_block, (jnp.arange(nb), qb))
    return jnp.moveaxis(o, 0, 2).reshape(b, h, s, v.shape[-1])


def mla_branch(c_q, c_kv, k_rope, pos, cq_norm, ckv_norm, w_uq, w_ukv, q_norm, k_norm):
    b, s, _ = c_q.shape
    h = MLA_HEADS
    q = (rms_norm(c_q, cq_norm) @ w_uq).reshape(b, s, h, MLA_QK)
    kv = (rms_norm(c_kv, ckv_norm) @ w_ukv).reshape(b, s, h, MLA_NOPE + MLA_V)
    k_nope, v = kv[..., :MLA_NOPE], kv[..., MLA_NOPE:]
    k = jnp.concatenate([k_nope, jnp.broadcast_to(k_rope[:, :, None, :], (b, s, h, MLA_ROPE))], axis=-1)
    q = rms_norm(q, q_norm)
    k = rms_norm(k, k_norm)
    q = jnp.concatenate([q[..., :MLA_NOPE], rope(q[..., MLA_NOPE:], pos)], axis=-1)
    k = jnp.concatenate([k[..., :MLA_NOPE], rope(k[..., MLA_NOPE:], pos)], axis=-1)
    to_bhsd = lambda t: jnp.transpose(t, (0, 2, 1, 3))
    o = causal_attention_blocked(to_bhsd(q), to_bhsd(k), to_bhsd(v), MLA_QK ** -0.5)
    return jnp.transpose(o, (0, 2, 1, 3)).reshape(b, s, h * MLA_V)


def gated_delta_rule_chunked(q, k, v, beta, g):
    b, s, h, dk = q.shape
    dv = v.shape[-1]
    c = GDN_CHUNK
    n = s // c

    def chunks(t):
        return jnp.moveaxis(t.reshape((b, n, c, h) + t.shape[3:]), 3, 1)

    q, k, v, beta, g = (chunks(t) for t in (q, k, v, beta, g))
    gcum = jnp.cumsum(g, axis=-1)
    tril = jnp.tril(jnp.ones((c, c), dtype=bool))
    strict = jnp.tril(jnp.ones((c, c), dtype=bool), -1)
    decay = jnp.exp(jnp.where(tril, gcum[..., :, None] - gcum[..., None, :], NEG_INF))
    k_beta = k * beta[..., None]
    a = jnp.where(strict, jnp.einsum('bhnid,bhnjd->bhnij', k_beta, k) * decay, 0.0)
    eye = jnp.eye(c, dtype=q.dtype)
    rhs = jnp.concatenate([v * beta[..., None], k_beta * jnp.exp(gcum)[..., None]], axis=-1)
    sol = lax.linalg.triangular_solve(eye + a, rhs, left_side=True, lower=True, unit_diagonal=True)
    u, w = sol[..., :dv], sol[..., dv:]
    qk = jnp.einsum('bhnid,bhnjd->bhnij', q, k) * decay
    q_dec = q * jnp.exp(gcum)[..., None]
    g_last = gcum[..., -1]
    k_dec = k * jnp.exp(g_last[..., None] - gcum)[..., None]

    def step(state, xs):
        q_i, w_i, u_i, qk_i, k_i, gl_i = xs
        v_new = u_i - jnp.einsum('bhcd,bhde->bhce', w_i, state)
        o_i = jnp.einsum('bhcd,bhde->bhce', q_i, state) + jnp.einsum('bhcj,bhje->bhce', qk_i, v_new)
        state = state * jnp.exp(gl_i)[..., None, None] + jnp.einsum('bhcd,bhce->bhde', k_i, v_new)
        return state, o_i

    xs = tuple(jnp.moveaxis(t, 2, 0) for t in (q_dec, w, u, qk, k_dec, g_last))
    state0 = jnp.zeros((b, h, dk, dv), q.dtype)
    _, o = lax.scan(step, state0, xs)
    return jnp.transpose(o, (1, 0, 3, 2, 4)).reshape(b, s, h, dv)


def gdn_branch(q, k, v, b_logit, a_logit, z, conv_w, a_log, dt_bias, out_norm):
    b, s, _ = q.shape
    f32 = jnp.float32
    qkv = jax.nn.silu(causal_dwconv(jnp.concatenate([q, k, v], axis=-1), conv_w)).astype(f32)
    q, k, v = jnp.split(qkv, [GDN_HEADS * GDN_DK, 2 * GDN_HEADS * GDN_DK], axis=-1)
    q = l2norm(q.reshape(b, s, GDN_HEADS, GDN_DK)) * (GDN_DK ** -0.5)
    k = l2norm(k.reshape(b, s, GDN_HEADS, GDN_DK))
    v = v.reshape(b, s, GDN_HEADS, GDN_DV)
    beta = jax.nn.sigmoid(b_logit.astype(f32))
    g = -jnp.exp(a_log.astype(f32)) * jax.nn.softplus(a_logit.astype(f32) + dt_bias.astype(f32))
    o = gated_delta_rule_chunked(q, k, v, beta, g)
    o = rms_norm(o, out_norm).astype(z.dtype) * jax.nn.silu(z.reshape(b, s, GDN_HEADS, GDN_DV))
    return o.reshape(b, s, GDN_HEADS * GDN_DV)


def moba_branch(qkv, pos, q_norm, k_norm):
    b, s, _ = qkv.shape
    h, dh, bs = MOBA_HEADS, MOBA_DH, MOBA_BLOCK
    q, k, v = (t.reshape(b, s, h, dh) for t in jnp.split(qkv, 3, axis=-1))
    q = rope(rms_norm(q, q_norm), pos)
    k = rope(rms_norm(k, k_norm), pos)
    nb = -(-s // bs)
    s_pad = nb * bs

    def prep(t):
        return jnp.pad(jnp.transpose(t, (0, 2, 1, 3)), ((0, 0), (0, 0), (0, s_pad - s), (0, 0)))

    q, k, v = prep(q), prep(k), prep(v)
    kb = k.reshape(b, h, nb, bs, dh)
    vb = v.reshape(b, h, nb, bs, dh)
    k_mean = jnp.mean(kb.astype(jnp.float32), axis=3)
    gate = jnp.einsum('bhsd,bhnd->bhsn', q.astype(jnp.float32), k_mean)
    q_blk = jnp.arange(s_pad) // bs
    gate = jnp.where(jnp.arange(nb)[None, :] < q_blk[:, None], gate, NEG_INF)
    n_sel = min(MOBA_TOPK, nb)
    _, sel = lax.top_k(gate, n_sel)
    sel_valid = sel < q_blk[:, None]
    scale = dh ** -0.5
    b_idx = jnp.arange(b)[:, None, None, None]
    h_idx = jnp.arange(h)[None, :, None, None]
    key_off = jnp.arange(bs)

    def one_chunk(ci):
        start = ci * MOBA_QCHUNK
        q_c = lax.dynamic_slice_in_dim(q, start, MOBA_QCHUNK, axis=2)
        sel_c = lax.dynamic_slice_in_dim(sel, start, MOBA_QCHUNK, axis=2)
        valid_c = lax.dynamic_slice_in_dim(sel_valid, start, MOBA_QCHUNK, axis=2)
        k_sel = kb[b_idx, h_idx, sel_c]
        v_sel = vb[b_idx, h_idx, sel_c]
        s_sel = jnp.einsum('bhqd,bhqnkd->bhqnk', q_c, k_sel, preferred_element_type=jnp.float32) * scale
        s_sel = jnp.where(valid_c[..., None], s_sel, NEG_INF)
        j = start // bs
        k_own = lax.dynamic_index_in_dim(kb, j, axis=2, keepdims=False)
        v_own = lax.dynamic_index_in_dim(vb, j, axis=2, keepdims=False)
        s_own = jnp.einsum('bhqd,bhkd->bhqk', q_c, k_own, preferred_element_type=jnp.float32) * scale
        q_pos = start + jnp.arange(MOBA_QCHUNK)
        k_pos = j * bs + key_off
        s_own = jnp.where(k_pos[None, :] <= q_pos[:, None], s_own, NEG_INF)
        logits = jnp.concatenate([s_sel.reshape(b, h, MOBA_QCHUNK, n_sel * bs), s_own], axis=-1)
        p = jax.nn.softmax(logits, axis=-1).astype(v.dtype)
        p_sel = p[..., :n_sel * bs].reshape(b, h, MOBA_QCHUNK, n_sel, bs)
        p_own = p[..., n_sel * bs:]
        return (jnp.einsum('bhqnk,bhqnkd->bhqd', p_sel, v_sel)
                + jnp.einsum('bhqk,bhkd->bhqd', p_own, v_own))

    o = lax.map(one_chunk, jnp.arange(s_pad // MOBA_QCHUNK))
    o = jnp.transpose(o, (1, 0, 3, 2, 4)).reshape(b, s_pad, h * dh)
    return o[:, :s]


def setup_inputs(seed: int = 0) -> dict:
    key = jax.random.key(seed)
    ks = jax.random.split(key, 24)
    L = DEPTH
    f32 = jnp.float32

    def nrm(k, shape, fan_in):
        return jax.random.normal(k, shape, f32) * (fan_in ** -0.5)

    def gain(k, shape):
        return 1.0 + 0.02 * jax.random.normal(k, shape, f32)

    dt = jnp.exp(jax.random.uniform(ks[14], (L, GDN_HEADS), f32, np.log(1e-3), np.log(1e-1)))
    return {
        'x': jax.random.normal(ks[0], (BATCH, SEQ, D_MODEL), f32),
        'ffa_norm': gain(ks[1], (L, D_MODEL)),
        'ffa_w_in': nrm(ks[2], (L, D_MODEL, 2 * D_FF), D_MODEL),
        'ffa_w_out': nrm(ks[3], (L, D_FF, D_MODEL), D_FF),
        'mix_norm': gain(ks[4], (L, D_MODEL)),
        'w_in': nrm(ks[5], (L, D_MODEL, D_IN), D_MODEL),
        'mla_cq_norm': gain(ks[6], (L, MLA_Q_RANK)),
        'mla_ckv_norm': gain(ks[7], (L, MLA_KV_RANK)),
        'mla_w_uq': nrm(ks[8], (L, MLA_Q_RANK, MLA_HEADS * MLA_QK), MLA_Q_RANK),
        'mla_w_ukv': nrm(ks[9], (L, MLA_KV_RANK, MLA_HEADS * (MLA_NOPE + MLA_V)), MLA_KV_RANK),
        'mla_q_norm': gain(ks[10], (L, MLA_QK)),
        'mla_k_norm': gain(ks[11], (L, MLA_QK)),
        'gdn_conv': nrm(ks[12], (L, GDN_CONV, GDN_HEADS * (2 * GDN_DK + GDN_DV)), GDN_CONV),
        'gdn_a_log': jnp.log(jax.random.uniform(ks[13], (L, GDN_HEADS), f32, 1.0, 16.0)),
        'gdn_dt_bias': dt + jnp.log(-jnp.expm1(-dt)),
        'gdn_out_norm': gain(ks[15], (L, GDN_DV)),
        'moba_q_norm': gain(ks[16], (L, MOBA_DH)),
        'moba_k_norm': gain(ks[17], (L, MOBA_DH)),
        'w_branch': nrm(ks[18], (L, N_BRANCH, MIX_WIDTH, D_MODEL), MIX_WIDTH),
        'w_out': nrm(ks[19], (L, D_MODEL, D_MODEL), D_MODEL),
        'ffb_norm': gain(ks[20], (L, D_MODEL)),
        'ffb_w_in': nrm(ks[21], (L, D_MODEL, 2 * D_FF), D_MODEL),
        'ffb_w_out': nrm(ks[22], (L, D_FF, D_MODEL), D_FF),
    }


def reference(x, ffa_norm, ffa_w_in, ffa_w_out, mix_norm, w_in, mla_cq_norm, mla_ckv_norm,
              mla_w_uq, mla_w_ukv, mla_q_norm, mla_k_norm, gdn_conv, gdn_a_log, gdn_dt_bias,
              gdn_out_norm, moba_q_norm, moba_k_norm, w_branch, w_out, ffb_norm, ffb_w_in,
              ffb_w_out):
    b, s, _ = x.shape
    pos = jnp.arange(s, dtype=jnp.int32)
    for l in range(DEPTH):
        x = x + 0.5 * swiglu(rms_norm(x, ffa_norm[l]), ffa_w_in[l], ffa_w_out[l])
        h = rms_norm(x, mix_norm[l])
        (c_q, c_kv, k_rope, g_q, g_k, g_v, g_b, g_a, g_z, m_qkv, gate_logits) = split_cols(h @ w_in[l], IN_SPLITS)
        o_mla = mla_branch(c_q, c_kv, k_rope, pos, mla_cq_norm[l], mla_ckv_norm[l], mla_w_uq[l],
                           mla_w_ukv[l], mla_q_norm[l], mla_k_norm[l])
        o_gdn = gdn_branch(g_q, g_k, g_v, g_b, g_a, g_z, gdn_conv[l], gdn_a_log[l], gdn_dt_bias[l],
                           gdn_out_norm[l])
        o_moba = moba_branch(m_qkv, pos, moba_q_norm[l], moba_k_norm[l])
        branches = jnp.stack([o_mla, o_gdn, o_moba], axis=2)
        up = jnp.einsum('bsnw,nwd->bsnd', branches, w_branch[l])
        gates = jax.nn.sigmoid(gate_logits.reshape(b, s, N_BRANCH, D_MODEL))
        x = x + jnp.sum(gates * up, axis=2) @ w_out[l]
        x = x + 0.5 * swiglu(rms_norm(x, ffb_norm[l]), ffb_w_in[l], ffb_w_out[l])
    return x
```

```python
import functools

import numpy as np
import jax
import jax.numpy as jnp
from jax import lax
from jax.experimental import pallas as pl
from jax.experimental.pallas import tpu as pltpu

F32 = jnp.float32
BF16 = jnp.bfloat16
MXU_DTYPE = BF16
HIGHEST = lax.Precision.HIGHEST

EPS = 1e-6
ROPE_THETA = 10000.0
NEG_INF = -1e30
SENTINEL = -3e38
LOG2E = 1.4426950408889634

D_MODEL = 1024
D_FF = 2816
MIX_WIDTH = 512
LANES = 128

MLA_HEADS = 8
MLA_Q_RANK = 256
MLA_KV_RANK = 128
MLA_NOPE = 64
MLA_ROPE = 32
MLA_V = 64
MLA_QK = MLA_NOPE + MLA_ROPE

GDN_HEADS = 4
GDN_DK = 128
GDN_DV = 128
GDN_CONV = 4
GDN_CHUNK = 64
GDN_HALO = 8

MOBA_HEADS = 8
MOBA_DH = 64
MOBA_BLOCK = 256
MOBA_TOPK = 3
MOBA_MAX_BLOCKS = 32

COL_GATE = 0
COL_GDN = 3072
COL_MOBA = 4608
COL_Z = 6144
COL_MLA = 6656
COL_BA = COL_MLA + 384
N_PROJ = 7168

VMEM_LIMIT = 48 * 1024 * 1024


def _dot(a, b, precision=None):
    return jnp.dot(a, b, preferred_element_type=F32, precision=precision)


def _dot_nt(a, b, precision=None):
    return lax.dot_general(a, b, (((1,), (1,)), ((), ())), preferred_element_type=F32,
                           precision=precision)


def _dot_tn(a, b, precision=None):
    return lax.dot_general(a, b, (((0,), (0,)), ((), ())), preferred_element_type=F32,
                           precision=precision)


def _rms(x, gain, n):
    ms = jnp.sum(x * x, axis=-1, keepdims=True) * (1.0 / n)
    return x * lax.rsqrt(ms + EPS) * gain


def _silu(x):
    return x * jax.nn.sigmoid(x)


def _softplus(x):
    return jnp.maximum(x, 0.0) + jnp.log1p(jnp.exp(-jnp.abs(x)))


def _params(sem):
    return pltpu.CompilerParams(dimension_semantics=sem, vmem_limit_bytes=VMEM_LIMIT)


def _ffn_body(x_ref, g_ref, wg_ref, wu_ref, wo_ref, o_ref, h_scr, acc_scr, *, nk):
    k = pl.program_id(1)

    @pl.when(k == 0)
    def _():
        h_scr[...] = _rms(x_ref[...], g_ref[...], D_MODEL).astype(MXU_DTYPE)
        acc_scr[...] = jnp.zeros_like(acc_scr)

    h = h_scr[...]
    gate = _dot(h, wg_ref[...])
    up = _dot(h, wu_ref[...])
    act = (_silu(gate) * up).astype(MXU_DTYPE)
    acc_scr[...] += _dot(act, wo_ref[...])

    @pl.when(k == nk - 1)
    def _():
        o_ref[...] = x_ref[...] + 0.5 * acc_scr[...]


def _ffn(x, gain, w_in, w_out, *, tm, tf):
    t, d = x.shape
    nk = D_FF // tf
    return pl.pallas_call(
        functools.partial(_ffn_body, nk=nk),
        grid=(t // tm, nk),
        in_specs=[
            pl.BlockSpec((tm, d), lambda i, k: (i, 0)),
            pl.BlockSpec((1, d), lambda i, k: (0, 0)),
            pl.BlockSpec((d, tf), lambda i, k: (0, k)),
            pl.BlockSpec((d, tf), lambda i, k: (0, k + nk)),
            pl.BlockSpec((tf, d), lambda i, k: (k, 0)),
        ],
        out_specs=pl.BlockSpec((tm, d), lambda i, k: (i, 0)),
        out_shape=jax.ShapeDtypeStruct((t, d), F32),
        scratch_shapes=[pltpu.VMEM((tm, d), MXU_DTYPE), pltpu.VMEM((tm, d), F32)],
        compiler_params=_params(("parallel", "arbitrary")),
        name="ffn",
    )(x, gain, w_in, w_in, w_out)


def _inproj_body(x_ref, g_ref, w_ref, wbat_ref, o_ref, bat_ref, h_scr):
    j = pl.program_id(1)

    @pl.when(j == 0)
    def _():
        h = _rms(x_ref[...], g_ref[...], D_MODEL).astype(MXU_DTYPE)
        h_scr[...] = h
        bat_ref[...] = _dot_nt(wbat_ref[...], h)

    o_ref[...] = _dot(h_scr[...], w_ref[...])


def _inproj(x, gain, w, wbat, *, tm, tn):
    t, d = x.shape
    return pl.pallas_call(
        _inproj_body,
        grid=(t // tm, N_PROJ // tn),
        in_specs=[
            pl.BlockSpec((tm, d), lambda i, j: (i, 0)),
            pl.BlockSpec((1, d), lambda i, j: (0, 0)),
            pl.BlockSpec((d, tn), lambda i, j: (0, j)),
            pl.BlockSpec((8, d), lambda i, j: (0, 0)),
        ],
        out_specs=[
            pl.BlockSpec((tm, tn), lambda i, j: (i, j)),
            pl.BlockSpec((8, tm), lambda i, j: (0, i)),
        ],
        out_shape=[jax.ShapeDtypeStruct((t, N_PROJ), F32), jax.ShapeDtypeStruct((8, t), F32)],
        scratch_shapes=[pltpu.VMEM((tm, d), MXU_DTYPE)],
        compiler_params=_params(("parallel", "arbitrary")),
        name="inproj",
    )(x, gain, w, wbat)


def _mla_rot(x, lane):
    return jnp.where(lane < MLA_NOPE + MLA_ROPE // 2, -pltpu.roll(x, LANES - MLA_ROPE // 2, 1),
                     pltpu.roll(x, MLA_ROPE // 2, 1))


def _mla_prep_body(lat_ref, cqg_ref, ckvg_ref, wq_ref, wk_ref, wv_ref, qg_ref, kg_ref, cos_ref,
                   sin_ref, q_ref, k_ref, v_ref, *, scale):
    lat = lat_ref[...]
    tm = lat.shape[0]
    cqn = _rms(lat[:, :MLA_Q_RANK], cqg_ref[...], MLA_Q_RANK).astype(MXU_DTYPE)
    ckvn = _rms(lat[:, MLA_Q_RANK:MLA_Q_RANK + MLA_KV_RANK], ckvg_ref[...], MLA_KV_RANK).astype(MXU_DTYPE)
    lane = lax.broadcasted_iota(jnp.int32, (tm, LANES), 1)
    kr = jnp.where((lane >= MLA_NOPE) & (lane < MLA_QK), lat[:, MLA_Q_RANK + MLA_KV_RANK:], 0.0)
    q_all = _dot(cqn, wq_ref[...])
    k_all = _dot(ckvn, wk_ref[...])
    v_ref[0] = _dot(ckvn, wv_ref[...]).astype(v_ref.dtype)
    cos = cos_ref[...]
    sin = sin_ref[...]
    qg = qg_ref[...]
    kg = kg_ref[...]
    for h in range(MLA_HEADS):
        sl = slice(h * LANES, (h + 1) * LANES)
        qn = _rms(q_all[:, sl], qg, MLA_QK)
        q_ref[0, h] = ((qn * cos + _mla_rot(qn, lane) * sin) * scale).astype(q_ref.dtype)
        kn = _rms(k_all[:, sl] + kr, kg, MLA_QK)
        k_ref[0, h] = (kn * cos + _mla_rot(kn, lane) * sin).astype(k_ref.dtype)


def _mla_prep(proj, cqg, ckvg, wq, wk, wv, qg, kg, cos, sin, *, b, s, tm):
    nt = s // tm
    scale = (MLA_QK ** -0.5) * LOG2E
    cblk = COL_MLA // 512
    full = lambda shape: pl.BlockSpec(shape, lambda bi, i: (0,) * len(shape))
    return pl.pallas_call(
        functools.partial(_mla_prep_body, scale=scale),
        grid=(b, nt),
        in_specs=[
            pl.BlockSpec((tm, 512), lambda bi, i: (bi * nt + i, cblk)),
            full((1, MLA_Q_RANK)), full((1, MLA_KV_RANK)),
            full((MLA_Q_RANK, MLA_HEADS * LANES)), full((MLA_KV_RANK, MLA_HEADS * LANES)),
            full((MLA_KV_RANK, MIX_WIDTH)), full((1, LANES)), full((1, LANES)),
            pl.BlockSpec((tm, LANES), lambda bi, i: (i, 0)),
            pl.BlockSpec((tm, LANES), lambda bi, i: (i, 0)),
        ],
        out_specs=[
            pl.BlockSpec((1, MLA_HEADS, tm, LANES), lambda bi, i: (bi, 0, i, 0)),
            pl.BlockSpec((1, MLA_HEADS, tm, LANES), lambda bi, i: (bi, 0, i, 0)),
            pl.BlockSpec((1, tm, MIX_WIDTH), lambda bi, i: (bi, i, 0)),
        ],
        out_shape=[
            jax.ShapeDtypeStruct((b, MLA_HEADS, s, LANES), MXU_DTYPE),
            jax.ShapeDtypeStruct((b, MLA_HEADS, s, LANES), MXU_DTYPE),
            jax.ShapeDtypeStruct((b, s, MIX_WIDTH), MXU_DTYPE),
        ],
        compiler_params=_params(("parallel", "parallel")),
        name="mla_prep",
    )(proj, cqg, ckvg, wq, wk, wv, qg, kg, cos, sin)


def _moba_rot(x, lane):
    half = MOBA_DH // 2
    return jnp.where(lane % MOBA_DH < half, -pltpu.roll(x, LANES - half, 1), pltpu.roll(x, half, 1))


def _pair_rms(x, gain, lo_mask):
    sq = x * x
    ss_lo = jnp.sum(jnp.where(lo_mask, sq, 0.0), axis=-1, keepdims=True)
    ss_hi = jnp.sum(jnp.where(lo_mask, 0.0, sq), axis=-1, keepdims=True)
    r = jnp.where(lo_mask, lax.rsqrt(ss_lo * (1.0 / MOBA_DH) + EPS), lax.rsqrt(ss_hi * (1.0 / MOBA_DH) + EPS))
    return x * r * gain


def _select_bias(gate, lane, lane_f, lo, j):
    g = jnp.where((lane >= lo) & (lane < lo + j), gate, SENTINEL)
    allowed = lane == lo + j
    for _ in range(MOBA_TOPK):
        m = jnp.max(g, axis=1, keepdims=True)
        first = jnp.min(jnp.where(g == m, lane_f, 1e9), axis=1, keepdims=True)
        pick = (lane_f == first) & (m > SENTINEL)
        allowed = allowed | pick
        g = jnp.where(pick, SENTINEL, g)
    return jnp.where(allowed, 0.0, NEG_INF)


def _moba_prep_body(qkv_ref, qg_ref, kg_ref, cos_ref, sin_ref, q_ref, k_ref, v_ref, kmt_scr, *, scale):
    j = pl.program_id(1)

    @pl.when(j == 0)
    def _():
        kmt_scr[...] = jnp.zeros_like(kmt_scr)

    tq = qkv_ref.shape[0]
    hw = MOBA_HEADS * MOBA_DH
    lane = lax.broadcasted_iota(jnp.int32, (tq, LANES), 1)
    lane_f = lane.astype(F32)
    lo_mask = lane < MOBA_DH
    cos = cos_ref[...]
    sin = sin_ref[...]
    v_ref[0] = qkv_ref[:, 2 * hw:].astype(v_ref.dtype)
    row_lane = lax.broadcasted_iota(jnp.int32, (1, LANES), 1)
    slot = lax.broadcasted_iota(jnp.int32, (LANES, LANES), 0)
    for p in range(MOBA_HEADS // 2):
        sl = slice(p * LANES, (p + 1) * LANES)
        qn = _pair_rms(qkv_ref[:, sl], qg_ref[...], lo_mask)
        q = qn * cos + _moba_rot(qn, lane) * sin
        kn = _pair_rms(qkv_ref[:, hw + p * LANES:hw + (p + 1) * LANES], kg_ref[...], lo_mask)
        k = kn * cos + _moba_rot(kn, lane) * sin
        gate = _dot_nt(q, kmt_scr[p], precision=HIGHEST)
        bias_e = _select_bias(gate, lane, lane_f, MOBA_DH, j)
        bias_o = _select_bias(gate, lane, lane_f, 0, j)
        qs = q * scale
        q_ref[0, 2 * p] = jnp.where(lo_mask, qs, jnp.where(lane < MOBA_DH + MOBA_MAX_BLOCKS, bias_e, 0.0)).astype(q_ref.dtype)
        q_ref[0, 2 * p + 1] = jnp.where(lo_mask, jnp.where(lane < MOBA_MAX_BLOCKS, bias_o, 0.0), qs).astype(q_ref.dtype)
        k_ref[0, 2 * p] = jnp.where(lo_mask, k, jnp.where(lane == MOBA_DH + j, 1.0, 0.0)).astype(k_ref.dtype)
        k_ref[0, 2 * p + 1] = jnp.where(lo_mask, jnp.where(lane == j, 1.0, 0.0), k).astype(k_ref.dtype)
        km = jnp.sum(k, axis=0, keepdims=True) * (1.0 / MOBA_BLOCK)
        new_rows = jnp.where(slot == MOBA_DH + j, jnp.where(row_lane < MOBA_DH, km, 0.0),
                             jnp.where(row_lane < MOBA_DH, 0.0, km))
        kmt_scr[p] = jnp.where((slot == MOBA_DH + j) | (slot == j), new_rows, kmt_scr[p])


def _moba_prep(proj, qg, kg, cos, sin, *, b, s):
    tq = MOBA_BLOCK
    nt = s // tq
    scale = (MOBA_DH ** -0.5) * LOG2E
    cblk = COL_MOBA // (3 * MOBA_HEADS * MOBA_DH)
    return pl.pallas_call(
        functools.partial(_moba_prep_body, scale=scale),
        grid=(b, nt),
        in_specs=[
            pl.BlockSpec((tq, 3 * MOBA_HEADS * MOBA_DH), lambda bi, j: (bi * nt + j, cblk)),
            pl.BlockSpec((1, LANES), lambda bi, j: (0, 0)),
            pl.BlockSpec((1, LANES), lambda bi, j: (0, 0)),
            pl.BlockSpec((tq, LANES), lambda bi, j: (j, 0)),
            pl.BlockSpec((tq, LANES), lambda bi, j: (j, 0)),
        ],
        out_specs=[
            pl.BlockSpec((1, MOBA_HEADS, tq, LANES), lambda bi, j: (bi, 0, j, 0)),
            pl.BlockSpec((1, MOBA_HEADS, tq, LANES), lambda bi, j: (bi, 0, j, 0)),
            pl.BlockSpec((1, tq, MIX_WIDTH), lambda bi, j: (bi, j, 0)),
        ],
        out_shape=[
            jax.ShapeDtypeStruct((b, MOBA_HEADS, s, LANES), MXU_DTYPE),
            jax.ShapeDtypeStruct((b, MOBA_HEADS, s, LANES), MXU_DTYPE),
            jax.ShapeDtypeStruct((b, s, MIX_WIDTH), MXU_DTYPE),
        ],
        scratch_shapes=[pltpu.VMEM((MOBA_HEADS // 2, LANES, LANES), F32)],
        compiler_params=_params(("parallel", "arbitrary")),
        name="moba_prep",
    )(proj, qg, kg, cos, sin)


def _flash_body(q_ref, k_ref, v_ref, o_ref, *, tq):
    i = pl.program_id(2)
    qs = (q_ref[0, 0], q_ref[0, 1])

    def step(t, carry, diagonal):
        ks = pl.ds(pl.multiple_of(t * tq, tq), tq)
        vt = v_ref[0, ks, :]
        out = []
        for hh in range(2):
            m, l, acc = carry[3 * hh:3 * hh + 3]
            s = _dot_nt(qs[hh], k_ref[0, hh, ks, :])
            if diagonal:
                row = lax.broadcasted_iota(jnp.int32, s.shape, 0)
                col = lax.broadcasted_iota(jnp.int32, s.shape, 1)
                s = jnp.where(col <= row, s, NEG_INF)
            m_new = jnp.maximum(m, jnp.max(s, axis=1, keepdims=True))
            alpha = jnp.exp2(m - m_new)
            p = jnp.exp2(s - m_new)
            l = alpha * l + jnp.sum(p, axis=1, keepdims=True)
            acc = alpha * acc + _dot(p.astype(MXU_DTYPE), vt)
            out += [m_new, l, acc]
        return tuple(out)

    init = (jnp.full((tq, 1), NEG_INF, F32), jnp.zeros((tq, 1), F32), jnp.zeros((tq, LANES), F32)) * 2
    carry = lax.fori_loop(0, i, lambda t, c: step(t, c, False), init)
    _, l0, a0, _, l1, a1 = step(i, carry, True)
    lane = lax.broadcasted_iota(jnp.int32, (tq, LANES), 1)
    o_ref[0] = jnp.where(lane < LANES // 2, a0 / l0, a1 / l1).astype(o_ref.dtype)


def _flash(q, k, v, *, tq):
    b, h, s, _ = q.shape
    return pl.pallas_call(
        functools.partial(_flash_body, tq=tq),
        grid=(b, h // 2, s // tq),
        in_specs=[
            pl.BlockSpec((1, 2, tq, LANES), lambda bi, p, i: (bi, p, i, 0)),
            pl.BlockSpec((1, 2, s, LANES), lambda bi, p, i: (bi, p, 0, 0)),
            pl.BlockSpec((1, s, LANES), lambda bi, p, i: (bi, 0, p)),
        ],
        out_specs=pl.BlockSpec((1, tq, LANES), lambda bi, p, i: (bi, i, p)),
        out_shape=jax.ShapeDtypeStruct((b, s, MIX_WIDTH), MXU_DTYPE),
        compiler_params=_params(("parallel", "parallel", "arbitrary")),
        name="flash",
    )(q, k, v)


def _gdn_prep_body(qkv_ref, halo_ref, ba_ref, bat_ref, conv_ref, alog_ref, dtb_ref, alogt_ref, dtbt_ref,
                   u_ref, w_ref, qd_ref, kd_ref, qk_ref, egl_ref, xe_scr, y_scr, bg_scr,
                   *, tiles_per_seq, nc):
    i = pl.program_id(0)
    tm = qkv_ref.shape[0]
    c = GDN_CHUNK
    hk = GDN_HEADS * GDN_DK

    xe_scr[pl.ds(0, GDN_HALO), :] = jnp.where(i % tiles_per_seq == 0, 0.0, halo_ref[...])
    xe_scr[pl.ds(GDN_HALO, tm), :] = qkv_ref[...]
    cw = conv_ref[...]
    y = xe_scr[pl.ds(GDN_HALO, tm), :] * cw[GDN_CONV - 1:GDN_CONV]
    for back in range(1, GDN_CONV):
        y = y + xe_scr[pl.ds(GDN_HALO - back, tm), :] * cw[GDN_CONV - 1 - back:GDN_CONV - back]
    y_scr[...] = _silu(y)

    ba = ba_ref[...]
    lane = lax.broadcasted_iota(jnp.int32, ba.shape, 1)
    g_col = -jnp.exp(alog_ref[...]) * _softplus(ba + dtb_ref[...])
    bg_scr[...] = jnp.where(lane < GDN_HEADS, jax.nn.sigmoid(ba), g_col)

    ri = lax.broadcasted_iota(jnp.int32, (c, c), 0)
    ci = lax.broadcasted_iota(jnp.int32, (c, c), 1)
    tril = ci <= ri
    strict = ci < ri
    lower_ones = jnp.where(tril, 1.0, 0.0)
    upper_ones = jnp.where(ci >= ri, 1.0, 0.0)

    def chunk(ch, _):
        rows = pl.ds(pl.multiple_of(ch * c, c), c)
        bg = bg_scr[rows, :]
        gcum_col = _dot(lower_ones, bg, precision=HIGHEST)
        bat = bat_ref[ch]
        g_row = -jnp.exp(alogt_ref[...]) * _softplus(bat + dtbt_ref[...])
        gcum_row = _dot(g_row, upper_ones, precision=HIGHEST)
        for h in range(GDN_HEADS):
            sl = slice(h * GDN_DK, (h + 1) * GDN_DK)
            qh = y_scr[rows, sl]
            kh = y_scr[rows, pl.ds(hk + h * GDN_DK, GDN_DK)]
            vh = y_scr[rows, pl.ds(2 * hk + h * GDN_DV, GDN_DV)]
            qn = qh * lax.rsqrt(jnp.sum(qh * qh, axis=-1, keepdims=True) + EPS) * (GDN_DK ** -0.5)
            kn = kh * lax.rsqrt(jnp.sum(kh * kh, axis=-1, keepdims=True) + EPS)
            beta = bg[:, h:h + 1]
            gc = gcum_col[:, GDN_HEADS + h:GDN_HEADS + h + 1]
            gr = gcum_row[GDN_HEADS + h:GDN_HEADS + h + 1, :]
            kb = kn * beta
            knm = kn.astype(MXU_DTYPE)
            dec = jnp.exp(jnp.where(tril, gc - gr, NEG_INF))
            a = jnp.where(strict, _dot_nt(kb.astype(MXU_DTYPE), knm) * dec, 0.0)
            eg = jnp.exp(gc)
            sol = jnp.concatenate([vh * beta, kb * eg], axis=1)
            pw = -a
            n_fac = int(np.log2(c))
            for it in range(n_fac):
                sol = sol + _dot(pw, sol, precision=HIGHEST)
                if it < n_fac - 1:
                    pw = _dot(pw, pw, precision=HIGHEST)
            gl = gc[c - 1:c, :]
            u_ref[rows, sl] = sol[:, :GDN_DV]
            w_ref[rows, sl] = sol[:, GDN_DV:].astype(w_ref.dtype)
            qd_ref[rows, sl] = (qn * eg).astype(qd_ref.dtype)
            kd_ref[rows, sl] = (kn * jnp.exp(gl - gc)).astype(kd_ref.dtype)
            qk = _dot_nt(qn.astype(MXU_DTYPE), knm) * dec
            qk_ref[rows, pl.ds(h * LANES, c)] = qk.astype(qk_ref.dtype)
            qk_ref[rows, pl.ds(h * LANES + c, LANES - c)] = jnp.zeros((c, LANES - c), qk_ref.dtype)
            egl_ref[ch, :, sl] = jnp.broadcast_to(jnp.exp(gl), (1, GDN_DV))
        return 0

    lax.fori_loop(0, nc, chunk, 0)


def _gdn_prep(proj, bat, conv, alog, dtb, alogt, dtbt, *, s, tm):
    t = proj.shape[0]
    nc = tm // GDN_CHUNK
    hk = GDN_HEADS * GDN_DK
    width = 3 * hk
    cblk = COL_GDN // width
    full = lambda shape: pl.BlockSpec(shape, lambda i: (0,) * len(shape))
    row = lambda w: pl.BlockSpec((tm, w), lambda i: (i, 0))
    return pl.pallas_call(
        functools.partial(_gdn_prep_body, tiles_per_seq=s // tm, nc=nc),
        grid=(t // tm,),
        in_specs=[
            pl.BlockSpec((tm, width), lambda i: (i, cblk)),
            pl.BlockSpec((GDN_HALO, width), lambda i: (jnp.maximum(i * (tm // GDN_HALO) - 1, 0), cblk)),
            pl.BlockSpec((tm, LANES), lambda i: (i, COL_BA // LANES)),
            pl.BlockSpec((nc, 8, GDN_CHUNK), lambda i: (i, 0, 0)),
            full((GDN_CONV, width)), full((1, LANES)), full((1, LANES)), full((8, 1)), full((8, 1)),
        ],
        out_specs=[row(hk), row(hk), row(hk), row(hk), row(GDN_HEADS * LANES),
                   pl.BlockSpec((nc, 1, hk), lambda i: (i, 0, 0))],
        out_shape=[
            jax.ShapeDtypeStruct((t, hk), F32),
            jax.ShapeDtypeStruct((t, hk), MXU_DTYPE),
            jax.ShapeDtypeStruct((t, hk), MXU_DTYPE),
            jax.ShapeDtypeStruct((t, hk), MXU_DTYPE),
            jax.ShapeDtypeStruct((t, GDN_HEADS * LANES), MXU_DTYPE),
            jax.ShapeDtypeStruct((t // GDN_CHUNK, 1, hk), F32),
        ],
        scratch_shapes=[
            pltpu.VMEM((tm + GDN_HALO, width), F32),
            pltpu.VMEM((tm, width), F32),
            pltpu.VMEM((tm, LANES), F32),
        ],
        compiler_params=_params(("parallel",)),
        name="gdn_prep",
    )(proj, proj, proj, bat, conv, alog, dtb, alogt, dtbt)


def _gdn_scan_body(u_ref, w_ref, qd_ref, kd_ref, qk_ref, egl_ref, z_ref, og_ref, o_ref, s_scr, *, nb, nc):
    @pl.when(pl.program_id(0) == 0)
    def _():
        s_scr[...] = jnp.zeros_like(s_scr)

    c = GDN_CHUNK
    og = og_ref[...]

    def chunk(ch, _):
        rows = pl.ds(pl.multiple_of(ch * c, c), c)
        for bi in range(nb):
            for h in range(GDN_HEADS):
                sl = slice(h * GDN_DV, (h + 1) * GDN_DV)
                st = s_scr[bi * GDN_HEADS + h]
                stm = st.astype(MXU_DTYPE)
                v_new = u_ref[bi, rows, sl] - _dot(w_ref[bi, rows, sl], stm)
                vm = v_new.astype(MXU_DTYPE)
                o = _dot(qd_ref[bi, rows, sl], stm) + _dot(qk_ref[bi, rows, pl.ds(h * LANES, c)], vm)
                s_scr[bi * GDN_HEADS + h] = st * egl_ref[bi, ch, :, sl] + _dot_tn(kd_ref[bi, rows, sl], vm)
                o_ref[bi, rows, sl] = (_rms(o, og, GDN_DV) * _silu(z_ref[bi, rows, sl])).astype(o_ref.dtype)
        return 0

    lax.fori_loop(0, nc, chunk, 0)


def _gdn_scan(u, w, qd, kd, qk, egl, proj3, og, *, tt):
    b, s, hk = u.shape
    nc = tt // GDN_CHUNK
    blk = lambda width: pl.BlockSpec((b, tt, width), lambda i: (0, i, 0))
    return pl.pallas_call(
        functools.partial(_gdn_scan_body, nb=b, nc=nc),
        grid=(s // tt,),
        in_specs=[
            blk(hk), blk(hk), blk(hk), blk(hk), blk(GDN_HEADS * LANES),
            pl.BlockSpec((b, nc, 1, hk), lambda i: (0, i, 0, 0)),
            pl.BlockSpec((b, tt, hk), lambda i: (0, i, COL_Z // hk)),
            pl.BlockSpec((1, GDN_DV), lambda i: (0, 0)),
        ],
        out_specs=blk(hk),
        out_shape=jax.ShapeDtypeStruct((b, s, hk), MXU_DTYPE),
        scratch_shapes=[pltpu.VMEM((b * GDN_HEADS, GDN_DK, GDN_DV), F32)],
        compiler_params=_params(("arbitrary",)),
        name="gdn_scan",
    )(u, w, qd, kd, qk, egl, proj3, og)


def _merge_body(x_ref, om_ref, og_ref, ob_ref, gl_ref, wb_ref, wo_ref, o_ref):
    mixed = None
    for n, br in enumerate((om_ref, og_ref, ob_ref)):
        up = _dot(br[...], wb_ref[n])
        term = jax.nn.sigmoid(gl_ref[:, n * D_MODEL:(n + 1) * D_MODEL]) * up
        mixed = term if mixed is None else mixed + term
    o_ref[...] = x_ref[...] + _dot(mixed.astype(MXU_DTYPE), wo_ref[...])


def _merge(x, o_mla, o_gdn, o_moba, proj, wb, wo, *, tm):
    t, d = x.shape
    row = lambda w: pl.BlockSpec((tm, w), lambda i: (i, 0))
    return pl.pallas_call(
        _merge_body,
        grid=(t // tm,),
        in_specs=[
            row(d), row(MIX_WIDTH), row(MIX_WIDTH), row(MIX_WIDTH), row(3 * d),
            pl.BlockSpec((3, MIX_WIDTH, d), lambda i: (0, 0, 0)),
            pl.BlockSpec((d, d), lambda i: (0, 0)),
        ],
        out_specs=row(d),
        out_shape=jax.ShapeDtypeStruct((t, d), F32),
        compiler_params=_params(("parallel",)),
        name="merge",
    )(x, o_mla, o_gdn, o_moba, proj, wb, wo)


def _rope_tables(s):
    pos = jnp.arange(s, dtype=jnp.int32).astype(F32)[:, None]

    def cs(d):
        half = d // 2
        inv_freq = ROPE_THETA ** (-jnp.arange(half, dtype=F32) * 2.0 / d)
        ang = pos * inv_freq[None, :]
        return jnp.cos(ang), jnp.sin(ang)

    c, sn = cs(MLA_ROPE)
    pad = LANES - MLA_QK
    mla_cos = jnp.concatenate([jnp.ones((s, MLA_NOPE), F32), c, c, jnp.zeros((s, pad), F32)], axis=1)
    mla_sin = jnp.concatenate([jnp.zeros((s, MLA_NOPE), F32), sn, sn, jnp.zeros((s, pad), F32)], axis=1)
    c, sn = cs(MOBA_DH)
    return mla_cos, mla_sin, jnp.tile(c, (1, 4)), jnp.tile(sn, (1, 4))


def _proj_weights(w_in):
    d = w_in.shape[0]
    o = np.cumsum((0, MLA_Q_RANK, MLA_KV_RANK, MLA_ROPE, 512, 512, 512, GDN_HEADS, GDN_HEADS, 512, 1536, 3 * D_MODEL))
    c_q, c_kv, k_rope = w_in[:, o[0]:o[1]], w_in[:, o[1]:o[2]], w_in[:, o[2]:o[3]]
    gdn_qkv = w_in[:, o[3]:o[6]]
    ba = w_in[:, o[6]:o[8]]
    z = w_in[:, o[8]:o[9]]
    moba = w_in[:, o[9]:o[10]]
    gate = w_in[:, o[10]:o[11]]
    zeros = lambda n: jnp.zeros((d, n), w_in.dtype)
    w = jnp.concatenate([gate, gdn_qkv, moba, z, c_q, c_kv, ba, zeros(MLA_NOPE - 2 * GDN_HEADS), k_rope,
                         zeros(LANES - MLA_QK)], axis=1)
    assert w.shape[1] == N_PROJ
    return w.astype(MXU_DTYPE), ba.T.astype(MXU_DTYPE)


def _lane_pad(v, n=LANES):
    return jnp.pad(v, (0, n - v.shape[0]))[None, :]


def kernel(x, ffa_norm, ffa_w_in, ffa_w_out, mix_norm, w_in, mla_cq_norm, mla_ckv_norm, mla_w_uq, mla_w_ukv, mla_q_norm, mla_k_norm, gdn_conv, gdn_a_log, gdn_dt_bias, gdn_out_norm, moba_q_norm, moba_k_norm, w_branch, w_out, ffb_norm, ffb_w_in, ffb_w_out):
    b, s, d = x.shape
    t = b * s
    assert d == D_MODEL and s % 512 == 0 and s // MOBA_BLOCK <= MOBA_MAX_BLOCKS
    depth = ffa_norm.shape[0]
    tm = 512
    mla_cos, mla_sin, moba_cos, moba_sin = _rope_tables(s)
    x = x.reshape(t, d)
    for l in range(depth):
        x = _ffn(x, ffa_norm[l][None, :], ffa_w_in[l].astype(MXU_DTYPE), ffa_w_out[l].astype(MXU_DTYPE),
                 tm=tm, tf=256)

        w_proj, w_bat = _proj_weights(w_in[l])
        proj, bat = _inproj(x, mix_norm[l][None, :], w_proj, w_bat, tm=tm, tn=1024)

        wq = jnp.pad(mla_w_uq[l].reshape(MLA_Q_RANK, MLA_HEADS, MLA_QK),
                     ((0, 0), (0, 0), (0, LANES - MLA_QK))).reshape(MLA_Q_RANK, MLA_HEADS * LANES)
        wkv = mla_w_ukv[l].reshape(MLA_KV_RANK, MLA_HEADS, MLA_NOPE + MLA_V)
        wk = jnp.pad(wkv[:, :, :MLA_NOPE], ((0, 0), (0, 0), (0, LANES - MLA_NOPE))).reshape(MLA_KV_RANK, MLA_HEADS * LANES)
        wv = wkv[:, :, MLA_NOPE:].reshape(MLA_KV_RANK, MLA_HEADS * MLA_V)
        q, k, v = _mla_prep(proj, mla_cq_norm[l][None, :], mla_ckv_norm[l][None, :], wq.astype(MXU_DTYPE),
                            wk.astype(MXU_DTYPE), wv.astype(MXU_DTYPE), _lane_pad(mla_q_norm[l]),
                            _lane_pad(mla_k_norm[l]), mla_cos, mla_sin, b=b, s=s, tm=tm)
        o_mla = _flash(q, k, v, tq=512)

        bat_chunks = bat.reshape(8, t // GDN_CHUNK, GDN_CHUNK).transpose(1, 0, 2)
        head_pad = lambda v: jnp.pad(v, (GDN_HEADS, LANES - 2 * GDN_HEADS))[None, :]
        head_col = lambda v: jnp.pad(v, (GDN_HEADS, 0))[:, None]
        u, w, qd, kd, qk, egl = _gdn_prep(proj, bat_chunks, gdn_conv[l], head_pad(gdn_a_log[l]),
                                          head_pad(gdn_dt_bias[l]), head_col(gdn_a_log[l]),
                                          head_col(gdn_dt_bias[l]), s=s, tm=tm)
        r3 = lambda a: a.reshape(b, s, a.shape[-1])
        o_gdn = _gdn_scan(r3(u), r3(w), r3(qd), r3(kd), r3(qk), egl.reshape(b, s // GDN_CHUNK, 1, -1),
                          r3(proj), gdn_out_norm[l][None, :], tt=tm)

        q, k, v = _moba_prep(proj, jnp.tile(moba_q_norm[l], 2)[None, :], jnp.tile(moba_k_norm[l], 2)[None, :],
                             moba_cos, moba_sin, b=b, s=s)
        o_moba = _flash(q, k, v, tq=512)

        x = _merge(x, o_mla.reshape(t, -1), o_gdn.reshape(t, -1), o_moba.reshape(t, -1), proj,
                   w_branch[l].astype(MXU_DTYPE), w_out[l].astype(MXU_DTYPE), tm=tm)

        x = _ffn(x, ffb_norm[l][None, :], ffb_w_in[l].astype(MXU_DTYPE), ffb_w_out[l].astype(MXU_DTYPE),
                 tm=tm, tf=256)
    return x.reshape(b, s, d)
```

```python
import functools

import numpy as np
import jax
import jax.numpy as jnp
from jax import lax
from jax.experimental import pallas as pl
from jax.experimental.pallas import tpu as pltpu

F32 = jnp.float32
BF16 = jnp.bfloat16
MXU_DTYPE = BF16
HIGHEST = lax.Precision.HIGHEST

EPS = 1e-6
ROPE_THETA = 10000.0
NEG_INF = -1e30
SENTINEL = -3e38
LOG2E = 1.4426950408889634

D_MODEL = 1024
D_FF = 2816
MIX_WIDTH = 512
LANES = 128

MLA_HEADS = 8
MLA_Q_RANK = 256
MLA_KV_RANK = 128
MLA_NOPE = 64
MLA_ROPE = 32
MLA_V = 64
MLA_QK = MLA_NOPE + MLA_ROPE

GDN_HEADS = 4
GDN_DK = 128
GDN_DV = 128
GDN_CONV = 4
GDN_CHUNK = 64
GDN_HALO = 8
GDN_GROUP = 2

MOBA_HEADS = 8
MOBA_DH = 64
MOBA_BLOCK = 256
MOBA_TOPK = 3
MOBA_MAX_BLOCKS = 32

COL_GATE = 0
COL_GDN = 3072
COL_MOBA = 4608
COL_Z = 6144
COL_MLA = 6656
COL_BA = COL_MLA + 384
N_PROJ = 7168

VMEM_LIMIT = 48 * 1024 * 1024


def _dot(a, b, precision=None):
    return jnp.dot(a, b, preferred_element_type=F32, precision=precision)


def _dot_nt(a, b, precision=None):
    return lax.dot_general(a, b, (((1,), (1,)), ((), ())), preferred_element_type=F32,
                           precision=precision)


def _dot_tn(a, b, precision=None):
    return lax.dot_general(a, b, (((0,), (0,)), ((), ())), preferred_element_type=F32,
                           precision=precision)


def _rms(x, gain, n):
    ms = jnp.sum(x * x, axis=-1, keepdims=True) * (1.0 / n)
    return x * lax.rsqrt(ms + EPS) * gain


def _silu(x):
    return x * jax.nn.sigmoid(x)


def _softplus(x):
    return jnp.maximum(x, 0.0) + jnp.log1p(jnp.exp(-jnp.abs(x)))


def _params(sem):
    return pltpu.CompilerParams(dimension_semantics=sem, vmem_limit_bytes=VMEM_LIMIT)


def _ffn_body(x_ref, g_ref, wg_ref, wu_ref, wo_ref, o_ref, h_scr, acc_scr, *, nk):
    k = pl.program_id(1)

    @pl.when(k == 0)
    def _():
        h_scr[...] = _rms(x_ref[...], g_ref[...], D_MODEL).astype(MXU_DTYPE)
        acc_scr[...] = jnp.zeros_like(acc_scr)

    h = h_scr[...]
    gate = _dot(h, wg_ref[...])
    up = _dot(h, wu_ref[...])
    act = (_silu(gate) * up).astype(MXU_DTYPE)
    acc_scr[...] += _dot(act, wo_ref[...])

    @pl.when(k == nk - 1)
    def _():
        o_ref[...] = x_ref[...] + 0.5 * acc_scr[...]


def _ffn(x, gain, w_in, w_out, *, tm, tf):
    t, d = x.shape
    nk = D_FF // tf
    return pl.pallas_call(
        functools.partial(_ffn_body, nk=nk),
        grid=(t // tm, nk),
        in_specs=[
            pl.BlockSpec((tm, d), lambda i, k: (i, 0)),
            pl.BlockSpec((1, d), lambda i, k: (0, 0)),
            pl.BlockSpec((d, tf), lambda i, k: (0, k)),
            pl.BlockSpec((d, tf), lambda i, k: (0, k + nk)),
            pl.BlockSpec((tf, d), lambda i, k: (k, 0)),
        ],
        out_specs=pl.BlockSpec((tm, d), lambda i, k: (i, 0)),
        out_shape=jax.ShapeDtypeStruct((t, d), F32),
        scratch_shapes=[pltpu.VMEM((tm, d), MXU_DTYPE), pltpu.VMEM((tm, d), F32)],
        compiler_params=_params(("parallel", "arbitrary")),
        name="ffn",
    )(x, gain, w_in, w_in, w_out)


def _inproj_body(x_ref, g_ref, w_ref, wbat_ref, o_ref, bat_ref, h_scr):
    j = pl.program_id(1)

    @pl.when(j == 0)
    def _():
        h = _rms(x_ref[...], g_ref[...], D_MODEL).astype(MXU_DTYPE)
        h_scr[...] = h
        bat_ref[...] = _dot_nt(wbat_ref[...], h)

    o_ref[...] = _dot(h_scr[...], w_ref[...])


def _inproj(x, gain, w, wbat, *, tm, tn):
    t, d = x.shape
    return pl.pallas_call(
        _inproj_body,
        grid=(t // tm, N_PROJ // tn),
        in_specs=[
            pl.BlockSpec((tm, d), lambda i, j: (i, 0)),
            pl.BlockSpec((1, d), lambda i, j: (0, 0)),
            pl.BlockSpec((d, tn), lambda i, j: (0, j)),
            pl.BlockSpec((8, d), lambda i, j: (0, 0)),
        ],
        out_specs=[
            pl.BlockSpec((tm, tn), lambda i, j: (i, j)),
            pl.BlockSpec((8, tm), lambda i, j: (0, i)),
        ],
        out_shape=[jax.ShapeDtypeStruct((t, N_PROJ), F32), jax.ShapeDtypeStruct((8, t), F32)],
        scratch_shapes=[pltpu.VMEM((tm, d), MXU_DTYPE)],
        compiler_params=_params(("parallel", "arbitrary")),
        name="inproj",
    )(x, gain, w, wbat)


def _mla_rot(x, lane):
    return jnp.where(lane < MLA_NOPE + MLA_ROPE // 2, -pltpu.roll(x, LANES - MLA_ROPE // 2, 1),
                     pltpu.roll(x, MLA_ROPE // 2, 1))


def _mla_prep_body(lat_ref, cqg_ref, ckvg_ref, wq_ref, wk_ref, wv_ref, qg_ref, kg_ref, cos_ref,
                   sin_ref, q_ref, k_ref, v_ref, *, scale):
    lat = lat_ref[...]
    tm = lat.shape[0]
    cqn = _rms(lat[:, :MLA_Q_RANK], cqg_ref[...], MLA_Q_RANK).astype(MXU_DTYPE)
    ckvn = _rms(lat[:, MLA_Q_RANK:MLA_Q_RANK + MLA_KV_RANK], ckvg_ref[...], MLA_KV_RANK).astype(MXU_DTYPE)
    lane = lax.broadcasted_iota(jnp.int32, (tm, LANES), 1)
    kr = jnp.where((lane >= MLA_NOPE) & (lane < MLA_QK), lat[:, MLA_Q_RANK + MLA_KV_RANK:], 0.0)
    q_all = _dot(cqn, wq_ref[...])
    k_all = _dot(ckvn, wk_ref[...])
    v_ref[0] = _dot_nt(wv_ref[...], ckvn).astype(v_ref.dtype)
    cos = cos_ref[...]
    sin = sin_ref[...]
    qg = qg_ref[...]
    kg = kg_ref[...]
    for h in range(MLA_HEADS):
        sl = slice(h * LANES, (h + 1) * LANES)
        qn = _rms(q_all[:, sl], qg, MLA_QK)
        q_ref[0, h] = ((qn * cos + _mla_rot(qn, lane) * sin) * scale).astype(q_ref.dtype)
        kn = _rms(k_all[:, sl] + kr, kg, MLA_QK)
        k_ref[0, h] = (kn * cos + _mla_rot(kn, lane) * sin).astype(k_ref.dtype)


def _mla_prep(proj, cqg, ckvg, wq, wk, wv, qg, kg, cos, sin, *, b, s, tm):
    nt = s // tm
    scale = (MLA_QK ** -0.5) * LOG2E
    cblk = COL_MLA // 512
    full = lambda shape: pl.BlockSpec(shape, lambda bi, i: (0,) * len(shape))
    return pl.pallas_call(
        functools.partial(_mla_prep_body, scale=scale),
        grid=(b, nt),
        in_specs=[
            pl.BlockSpec((tm, 512), lambda bi, i: (bi * nt + i, cblk)),
            full((1, MLA_Q_RANK)), full((1, MLA_KV_RANK)),
            full((MLA_Q_RANK, MLA_HEADS * LANES)), full((MLA_KV_RANK, MLA_HEADS * LANES)),
            full((MIX_WIDTH, MLA_KV_RANK)), full((1, LANES)), full((1, LANES)),
            pl.BlockSpec((tm, LANES), lambda bi, i: (i, 0)),
            pl.BlockSpec((tm, LANES), lambda bi, i: (i, 0)),
        ],
        out_specs=[
            pl.BlockSpec((1, MLA_HEADS, tm, LANES), lambda bi, i: (bi, 0, i, 0)),
            pl.BlockSpec((1, MLA_HEADS, tm, LANES), lambda bi, i: (bi, 0, i, 0)),
            pl.BlockSpec((1, MIX_WIDTH, tm), lambda bi, i: (bi, 0, i)),
        ],
        out_shape=[
            jax.ShapeDtypeStruct((b, MLA_HEADS, s, LANES), MXU_DTYPE),
            jax.ShapeDtypeStruct((b, MLA_HEADS, s, LANES), MXU_DTYPE),
            jax.ShapeDtypeStruct((b, MIX_WIDTH, s), MXU_DTYPE),
        ],
        compiler_params=_params(("parallel", "parallel")),
        name="mla_prep",
    )(proj, cqg, ckvg, wq, wk, wv, qg, kg, cos, sin)


def _moba_rot(x, lane):
    half = MOBA_DH // 2
    return jnp.where(lane % MOBA_DH < half, -pltpu.roll(x, LANES - half, 1), pltpu.roll(x, half, 1))


def _pair_rms(x, gain, lo_mask):
    sq = x * x
    ss_lo = jnp.sum(jnp.where(lo_mask, sq, 0.0), axis=-1, keepdims=True)
    ss_hi = jnp.sum(jnp.where(lo_mask, 0.0, sq), axis=-1, keepdims=True)
    r = jnp.where(lo_mask, lax.rsqrt(ss_lo * (1.0 / MOBA_DH) + EPS), lax.rsqrt(ss_hi * (1.0 / MOBA_DH) + EPS))
    return x * r * gain


def _select_bias(gate, lane, lane_f, lo, j):
    g = jnp.where((lane >= lo) & (lane < lo + j), gate, SENTINEL)
    allowed = lane == lo + j
    for _ in range(MOBA_TOPK):
        m = jnp.max(g, axis=1, keepdims=True)
        first = jnp.min(jnp.where(g == m, lane_f, 1e9), axis=1, keepdims=True)
        pick = (lane_f == first) & (m > SENTINEL)
        allowed = allowed | pick
        g = jnp.where(pick, SENTINEL, g)
    return jnp.where(allowed, 0.0, NEG_INF)


def _moba_prep_body(qkv_ref, qg_ref, kg_ref, cos_ref, sin_ref, q_ref, k_ref, v_ref, kmt_scr, *, scale):
    j = pl.program_id(1)

    @pl.when(j == 0)
    def _():
        kmt_scr[...] = jnp.zeros_like(kmt_scr)

    tq = qkv_ref.shape[0]
    hw = MOBA_HEADS * MOBA_DH
    lane = lax.broadcasted_iota(jnp.int32, (tq, LANES), 1)
    lane_f = lane.astype(F32)
    lo_mask = lane < MOBA_DH
    cos = cos_ref[...]
    sin = sin_ref[...]
    v_ref[0] = qkv_ref[:, 2 * hw:].T.astype(v_ref.dtype)
    row_lane = lax.broadcasted_iota(jnp.int32, (1, LANES), 1)
    slot = lax.broadcasted_iota(jnp.int32, (LANES, LANES), 0)
    for p in range(MOBA_HEADS // 2):
        sl = slice(p * LANES, (p + 1) * LANES)
        qn = _pair_rms(qkv_ref[:, sl], qg_ref[...], lo_mask)
        q = qn * cos + _moba_rot(qn, lane) * sin
        kn = _pair_rms(qkv_ref[:, hw + p * LANES:hw + (p + 1) * LANES], kg_ref[...], lo_mask)
        k = kn * cos + _moba_rot(kn, lane) * sin
        gate = _dot_nt(q, kmt_scr[p], precision=HIGHEST)
        bias_e = _select_bias(gate, lane, lane_f, MOBA_DH, j)
        bias_o = _select_bias(gate, lane, lane_f, 0, j)
        qs = q * scale
        q_ref[0, 2 * p] = jnp.where(lo_mask, qs, jnp.where(lane < MOBA_DH + MOBA_MAX_BLOCKS, bias_e, 0.0)).astype(q_ref.dtype)
        q_ref[0, 2 * p + 1] = jnp.where(lo_mask, jnp.where(lane < MOBA_MAX_BLOCKS, bias_o, 0.0), qs).astype(q_ref.dtype)
        k_ref[0, 2 * p] = jnp.where(lo_mask, k, jnp.where(lane == MOBA_DH + j, 1.0, 0.0)).astype(k_ref.dtype)
        k_ref[0, 2 * p + 1] = jnp.where(lo_mask, jnp.where(lane == j, 1.0, 0.0), k).astype(k_ref.dtype)
        km = jnp.sum(k, axis=0, keepdims=True) * (1.0 / MOBA_BLOCK)
        new_rows = jnp.where(slot == MOBA_DH + j, jnp.where(row_lane < MOBA_DH, km, 0.0),
                             jnp.where(row_lane < MOBA_DH, 0.0, km))
        kmt_scr[p] = jnp.where((slot == MOBA_DH + j) | (slot == j), new_rows, kmt_scr[p])


def _moba_prep(proj, qg, kg, cos, sin, *, b, s):
    tq = MOBA_BLOCK
    nt = s // tq
    scale = (MOBA_DH ** -0.5) * LOG2E
    cblk = COL_MOBA // (3 * MOBA_HEADS * MOBA_DH)
    return pl.pallas_call(
        functools.partial(_moba_prep_body, scale=scale),
        grid=(b, nt),
        in_specs=[
            pl.BlockSpec((tq, 3 * MOBA_HEADS * MOBA_DH), lambda bi, j: (bi * nt + j, cblk)),
            pl.BlockSpec((1, LANES), lambda bi, j: (0, 0)),
            pl.BlockSpec((1, LANES), lambda bi, j: (0, 0)),
            pl.BlockSpec((tq, LANES), lambda bi, j: (j, 0)),
            pl.BlockSpec((tq, LANES), lambda bi, j: (j, 0)),
        ],
        out_specs=[
            pl.BlockSpec((1, MOBA_HEADS, tq, LANES), lambda bi, j: (bi, 0, j, 0)),
            pl.BlockSpec((1, MOBA_HEADS, tq, LANES), lambda bi, j: (bi, 0, j, 0)),
            pl.BlockSpec((1, MIX_WIDTH, tq), lambda bi, j: (bi, 0, j)),
        ],
        out_shape=[
            jax.ShapeDtypeStruct((b, MOBA_HEADS, s, LANES), MXU_DTYPE),
            jax.ShapeDtypeStruct((b, MOBA_HEADS, s, LANES), MXU_DTYPE),
            jax.ShapeDtypeStruct((b, MIX_WIDTH, s), MXU_DTYPE),
        ],
        scratch_shapes=[pltpu.VMEM((MOBA_HEADS // 2, LANES, LANES), F32)],
        compiler_params=_params(("parallel", "arbitrary")),
        name="moba_prep",
    )(proj, qg, kg, cos, sin)


def _flash_body(q_ref, k_ref, vt_ref, o_ref, buf0, buf1, *, tq):
    i = pl.program_id(2)
    qs = (q_ref[0, 0], q_ref[0, 1])
    dv = LANES // 2

    def scores(t, buf, diagonal):
        ks = pl.ds(pl.multiple_of(t * tq, tq), tq)
        out = []
        for hh in range(2):
            s = _dot_nt(k_ref[0, hh, ks, :], qs[hh])
            if diagonal:
                key = lax.broadcasted_iota(jnp.int32, s.shape, 0)
                qry = lax.broadcasted_iota(jnp.int32, s.shape, 1)
                s = jnp.where(key <= qry, s, NEG_INF)
            buf[hh] = s
            out.append(jnp.max(s, axis=0, keepdims=True))
        return tuple(out)

    def consume(t, buf, smax, stats):
        ks = pl.ds(pl.multiple_of(t * tq, tq), tq)
        out = []
        for hh in range(2):
            m, l, acc = stats[3 * hh:3 * hh + 3]
            m_new = jnp.maximum(m, smax[hh])
            alpha = jnp.exp2(m - m_new)
            p = jnp.exp2(buf[hh] - m_new)
            l = alpha * l + jnp.sum(p, axis=0, keepdims=True)
            acc = alpha * acc + _dot(vt_ref[0, hh * dv:(hh + 1) * dv, ks], p.astype(MXU_DTYPE))
            out += [m_new, l, acc]
        return tuple(out)

    def finish(stats):
        o_t = jnp.concatenate([stats[2] / stats[1], stats[5] / stats[4]], axis=0)
        o_ref[0] = o_t.T.astype(o_ref.dtype)

    init = (jnp.full((1, tq), NEG_INF, F32), jnp.zeros((1, tq), F32), jnp.zeros((dv, tq), F32)) * 2

    @pl.when(i % 2 == 0)
    def _():
        def pair(u, carry):
            smax, stats = carry
            t0 = 2 * u
            smax1 = scores(t0, buf1, False)
            stats = consume(jnp.where(u == 0, i, t0 - 1), buf0, smax, stats)
            smax0 = scores(t0 + 1, buf0, False)
            return smax0, consume(t0, buf1, smax1, stats)

        smax, stats = lax.fori_loop(0, i // 2, pair, (scores(i, buf0, True), init))
        finish(consume(jnp.where(i == 0, 0, i - 1), buf0, smax, stats))

    @pl.when(i % 2 == 1)
    def _():
        smax0 = scores(i, buf0, True)
        smax1 = scores(0, buf1, False)
        stats = consume(i, buf0, smax0, init)

        def pair(u, carry):
            smax, stats = carry
            t0 = 2 * u + 1
            smax0 = scores(t0, buf0, False)
            stats = consume(t0 - 1, buf1, smax, stats)
            smax1 = scores(t0 + 1, buf1, False)
            return smax1, consume(t0, buf0, smax0, stats)

        smax, stats = lax.fori_loop(0, i // 2, pair, (smax1, stats))
        finish(consume(i - 1, buf1, smax, stats))


def _flash(q, k, vt, *, tq):
    b, h, s, _ = q.shape
    return pl.pallas_call(
        functools.partial(_flash_body, tq=tq),
        grid=(b, h // 2, s // tq),
        in_specs=[
            pl.BlockSpec((1, 2, tq, LANES), lambda bi, p, i: (bi, p, i, 0)),
            pl.BlockSpec((1, 2, s, LANES), lambda bi, p, i: (bi, p, 0, 0)),
            pl.BlockSpec((1, LANES, s), lambda bi, p, i: (bi, p, 0)),
        ],
        out_specs=pl.BlockSpec((1, tq, LANES), lambda bi, p, i: (bi, i, p)),
        out_shape=jax.ShapeDtypeStruct((b, s, MIX_WIDTH), MXU_DTYPE),
        scratch_shapes=[pltpu.VMEM((2, tq, tq), F32), pltpu.VMEM((2, tq, tq), F32)],
        compiler_params=_params(("parallel", "parallel", "arbitrary")),
        name="flash",
    )(q, k, vt)


def _hi_lo(x):
    hi = x.astype(MXU_DTYPE)
    return hi, (x - hi.astype(F32)).astype(MXU_DTYPE)


def _unit_lower_solve(a_list, rhs_list, eye):
    c = eye.shape[0]
    n = range(len(a_list))
    t = [eye - a for a in a_list]
    pk = []
    for a in a_list:
        pwm = (-a).astype(MXU_DTYPE)
        pk.append(_dot(pwm, pwm))
    n_sq = int(np.log2(c)) - 1
    for it in range(n_sq):
        if it < n_sq - 1:
            both = [_dot(jnp.concatenate([t[j], pk[j]], axis=0).astype(MXU_DTYPE), pk[j].astype(MXU_DTYPE))
                    for j in n]
            t = [t[j] + both[j][:c] for j in n]
            pk = [both[j][c:] for j in n]
        else:
            t = [t[j] + _dot(t[j].astype(MXU_DTYPE), pk[j].astype(MXU_DTYPE)) for j in n]
    tm = [x.astype(MXU_DTYPE) for x in t]
    x1 = [_dot(tm[j], rhs_list[j].astype(MXU_DTYPE)) for j in n]
    a_split = [_hi_lo(a) for a in a_list]
    x_split = [_hi_lo(x) for x in x1]
    ax = [_dot(jnp.concatenate(a_split[j], axis=0), x_split[j][0]) for j in n]
    ax_lo = [_dot(a_split[j][0], x_split[j][1]) for j in n]
    res = [rhs_list[j] - x1[j] - (ax[j][:c] + ax[j][c:] + ax_lo[j]) for j in n]
    return [x1[j] + _dot(tm[j], res[j].astype(MXU_DTYPE)) for j in n]


def _gdn_prep_body(qkv_ref, halo_ref, ba_ref, bat_ref, conv_ref, alog_ref, dtb_ref, alogt_ref, dtbt_ref,
                   u_ref, w_ref, qd_ref, kd_ref, qk_ref, egl_ref, xe_scr, y_scr, bg_scr,
                   *, tiles_per_seq, nc):
    i = pl.program_id(0)
    tm = qkv_ref.shape[0]
    c = GDN_CHUNK
    hk = GDN_HEADS * GDN_DK

    xe_scr[pl.ds(0, GDN_HALO), :] = jnp.where(i % tiles_per_seq == 0, 0.0, halo_ref[...])
    xe_scr[pl.ds(GDN_HALO, tm), :] = qkv_ref[...]
    cw = conv_ref[...]
    y = xe_scr[pl.ds(GDN_HALO, tm), :] * cw[GDN_CONV - 1:GDN_CONV]
    for back in range(1, GDN_CONV):
        y = y + xe_scr[pl.ds(GDN_HALO - back, tm), :] * cw[GDN_CONV - 1 - back:GDN_CONV - back]
    y_scr[...] = _silu(y)

    ba = ba_ref[...]
    lane = lax.broadcasted_iota(jnp.int32, ba.shape, 1)
    g_col = -jnp.exp(alog_ref[...]) * _softplus(ba + dtb_ref[...])
    bg_scr[...] = jnp.where(lane < GDN_HEADS, jax.nn.sigmoid(ba), g_col)

    ri = lax.broadcasted_iota(jnp.int32, (c, c), 0)
    ci = lax.broadcasted_iota(jnp.int32, (c, c), 1)
    tril = ci <= ri
    strict = ci < ri
    eye = jnp.where(ci == ri, 1.0, 0.0)
    lower_ones = jnp.where(tril, 1.0, 0.0)
    upper_ones = jnp.where(ci >= ri, 1.0, 0.0)

    def chunk_group(grp, _):
        chs = [grp * GDN_GROUP + j for j in range(GDN_GROUP)]
        rows = [pl.ds(pl.multiple_of(ch * c, c), c) for ch in chs]
        bg = [bg_scr[r, :] for r in rows]
        gcum_col = [_dot(lower_ones, x, precision=HIGHEST) for x in bg]
        g_row = [-jnp.exp(alogt_ref[...]) * _softplus(bat_ref[ch] + dtbt_ref[...]) for ch in chs]
        gcum_row = [_dot(x, upper_ones, precision=HIGHEST) for x in g_row]
        probs = [(j, h) for j in range(GDN_GROUP) for h in range(GDN_HEADS)]
        n = range(len(probs))
        sls = [slice(h * GDN_DK, (h + 1) * GDN_DK) for _, h in probs]
        qh = [y_scr[rows[j], sls[i]] for i, (j, h) in enumerate(probs)]
        kh = [y_scr[rows[j], pl.ds(hk + h * GDN_DK, GDN_DK)] for j, h in probs]
        vh = [y_scr[rows[j], pl.ds(2 * hk + h * GDN_DV, GDN_DV)] for j, h in probs]
        qn = [x * lax.rsqrt(jnp.sum(x * x, axis=-1, keepdims=True) + EPS) * (GDN_DK ** -0.5) for x in qh]
        kn = [x * lax.rsqrt(jnp.sum(x * x, axis=-1, keepdims=True) + EPS) for x in kh]
        beta = [bg[j][:, h:h + 1] for j, h in probs]
        gc = [gcum_col[j][:, GDN_HEADS + h:GDN_HEADS + h + 1] for j, h in probs]
        gr = [gcum_row[j][GDN_HEADS + h:GDN_HEADS + h + 1, :] for j, h in probs]
        kb = [kn[i] * beta[i] for i in n]
        knm = [x.astype(MXU_DTYPE) for x in kn]
        dec = [jnp.exp(jnp.where(tril, gc[i] - gr[i], NEG_INF)) for i in n]
        skk = [_dot_nt(kb[i].astype(MXU_DTYPE), knm[i]) for i in n]
        sqk = [_dot_nt(qn[i].astype(MXU_DTYPE), knm[i]) for i in n]
        a = [jnp.where(strict, skk[i] * dec[i], 0.0) for i in n]
        eg = [jnp.exp(x) for x in gc]
        rhs = [jnp.concatenate([vh[i] * beta[i], kb[i] * eg[i]], axis=1) for i in n]
        sol = _unit_lower_solve(a, rhs, eye)
        for i, (j, h) in enumerate(probs):
            sl, r = sls[i], rows[j]
            gl = gc[i][c - 1:c, :]
            u_ref[r, sl] = sol[i][:, :GDN_DV]
            w_ref[r, sl] = sol[i][:, GDN_DV:].astype(w_ref.dtype)
            qd_ref[r, sl] = (qn[i] * eg[i]).astype(qd_ref.dtype)
            kd_ref[r, sl] = (kn[i] * jnp.exp(gl - gc[i])).astype(kd_ref.dtype)
            qk_ref[r, pl.ds(h * LANES, c)] = (sqk[i] * dec[i]).astype(qk_ref.dtype)
            qk_ref[r, pl.ds(h * LANES + c, LANES - c)] = jnp.zeros((c, LANES - c), qk_ref.dtype)
            egl_ref[chs[j], :, sl] = jnp.broadcast_to(jnp.exp(gl), (1, GDN_DV))
        return 0

    lax.fori_loop(0, nc // GDN_GROUP, chunk_group, 0)


def _gdn_prep(proj, bat, conv, alog, dtb, alogt, dtbt, *, s, tm):
    t = proj.shape[0]
    nc = tm // GDN_CHUNK
    hk = GDN_HEADS * GDN_DK
    width = 3 * hk
    cblk = COL_GDN // width
    full = lambda shape: pl.BlockSpec(shape, lambda i: (0,) * len(shape))
    row = lambda w: pl.BlockSpec((tm, w), lambda i: (i, 0))
    return pl.pallas_call(
        functools.partial(_gdn_prep_body, tiles_per_seq=s // tm, nc=nc),
        grid=(t // tm,),
        in_specs=[
            pl.BlockSpec((tm, width), lambda i: (i, cblk)),
            pl.BlockSpec((GDN_HALO, width), lambda i: (jnp.maximum(i * (tm // GDN_HALO) - 1, 0), cblk)),
            pl.BlockSpec((tm, LANES), lambda i: (i, COL_BA // LANES)),
            pl.BlockSpec((nc, 8, GDN_CHUNK), lambda i: (i, 0, 0)),
            full((GDN_CONV, width)), full((1, LANES)), full((1, LANES)), full((8, 1)), full((8, 1)),
        ],
        out_specs=[row(hk), row(hk), row(hk), row(hk), row(GDN_HEADS * LANES),
                   pl.BlockSpec((nc, 1, hk), lambda i: (i, 0, 0))],
        out_shape=[
            jax.ShapeDtypeStruct((t, hk), F32),
            jax.ShapeDtypeStruct((t, hk), MXU_DTYPE),
            jax.ShapeDtypeStruct((t, hk), MXU_DTYPE),
            jax.ShapeDtypeStruct((t, hk), MXU_DTYPE),
            jax.ShapeDtypeStruct((t, GDN_HEADS * LANES), MXU_DTYPE),
            jax.ShapeDtypeStruct((t // GDN_CHUNK, 1, hk), F32),
        ],
        scratch_shapes=[
            pltpu.VMEM((tm + GDN_HALO, width), F32),
            pltpu.VMEM((tm, width), F32),
            pltpu.VMEM((tm, LANES), F32),
        ],
        compiler_params=_params(("parallel",)),
        name="gdn_prep",
    )(proj, proj, proj, bat, conv, alog, dtb, alogt, dtbt)


def _gdn_scan_body(u_ref, w_ref, qd_ref, kd_ref, qk_ref, egl_ref, z_ref, og_ref, o_ref, s_scr, *, nb, nc):
    @pl.when(pl.program_id(0) == 0)
    def _():
        s_scr[...] = jnp.zeros_like(s_scr)

    c = GDN_CHUNK
    og = og_ref[...]

    def chunk(ch, _):
        rows = pl.ds(pl.multiple_of(ch * c, c), c)
        for bi in range(nb):
            for h in range(GDN_HEADS):
                sl = slice(h * GDN_DV, (h + 1) * GDN_DV)
                st = s_scr[bi * GDN_HEADS + h]
                stm = st.astype(MXU_DTYPE)
                v_new = u_ref[bi, rows, sl] - _dot(w_ref[bi, rows, sl], stm)
                vm = v_new.astype(MXU_DTYPE)
                o = _dot(qd_ref[bi, rows, sl], stm) + _dot(qk_ref[bi, rows, pl.ds(h * LANES, c)], vm)
                s_scr[bi * GDN_HEADS + h] = st * egl_ref[bi, ch, :, sl] + _dot_tn(kd_ref[bi, rows, sl], vm)
                o_ref[bi, rows, sl] = (_rms(o, og, GDN_DV) * _silu(z_ref[bi, rows, sl])).astype(o_ref.dtype)
        return 0

    lax.fori_loop(0, nc, chunk, 0)


def _gdn_scan(u, w, qd, kd, qk, egl, proj3, og, *, tt):
    b, s, hk = u.shape
    nc = tt // GDN_CHUNK
    blk = lambda width: pl.BlockSpec((b, tt, width), lambda i: (0, i, 0))
    return pl.pallas_call(
        functools.partial(_gdn_scan_body, nb=b, nc=nc),
        grid=(s // tt,),
        in_specs=[
            blk(hk), blk(hk), blk(hk), blk(hk), blk(GDN_HEADS * LANES),
            pl.BlockSpec((b, nc, 1, hk), lambda i: (0, i, 0, 0)),
            pl.BlockSpec((b, tt, hk), lambda i: (0, i, COL_Z // hk)),
            pl.BlockSpec((1, GDN_DV), lambda i: (0, 0)),
        ],
        out_specs=blk(hk),
        out_shape=jax.ShapeDtypeStruct((b, s, hk), MXU_DTYPE),
        scratch_shapes=[pltpu.VMEM((b * GDN_HEADS, GDN_DK, GDN_DV), F32)],
        compiler_params=_params(("arbitrary",)),
        name="gdn_scan",
    )(u, w, qd, kd, qk, egl, proj3, og)


def _merge_body(x_ref, om_ref, og_ref, ob_ref, gl_ref, wb_ref, wo_ref, o_ref):
    mixed = None
    for n, br in enumerate((om_ref, og_ref, ob_ref)):
        up = _dot(br[...], wb_ref[n])
        term = jax.nn.sigmoid(gl_ref[:, n * D_MODEL:(n + 1) * D_MODEL]) * up
        mixed = term if mixed is None else mixed + term
    o_ref[...] = x_ref[...] + _dot(mixed.astype(MXU_DTYPE), wo_ref[...])


def _merge(x, o_mla, o_gdn, o_moba, proj, wb, wo, *, tm):
    t, d = x.shape
    row = lambda w: pl.BlockSpec((tm, w), lambda i: (i, 0))
    return pl.pallas_call(
        _merge_body,
        grid=(t // tm,),
        in_specs=[
            row(d), row(MIX_WIDTH), row(MIX_WIDTH), row(MIX_WIDTH), row(3 * d),
            pl.BlockSpec((3, MIX_WIDTH, d), lambda i: (0, 0, 0)),
            pl.BlockSpec((d, d), lambda i: (0, 0)),
        ],
        out_specs=row(d),
        out_shape=jax.ShapeDtypeStruct((t, d), F32),
        compiler_params=_params(("parallel",)),
        name="merge",
    )(x, o_mla, o_gdn, o_moba, proj, wb, wo)


def _rope_tables(s):
    pos = jnp.arange(s, dtype=jnp.int32).astype(F32)[:, None]

    def cs(d):
        half = d // 2
        inv_freq = ROPE_THETA ** (-jnp.arange(half, dtype=F32) * 2.0 / d)
        ang = pos * inv_freq[None, :]
        return jnp.cos(ang), jnp.sin(ang)

    c, sn = cs(MLA_ROPE)
    pad = LANES - MLA_QK
    mla_cos = jnp.concatenate([jnp.ones((s, MLA_NOPE), F32), c, c, jnp.zeros((s, pad), F32)], axis=1)
    mla_sin = jnp.concatenate([jnp.zeros((s, MLA_NOPE), F32), sn, sn, jnp.zeros((s, pad), F32)], axis=1)
    c, sn = cs(MOBA_DH)
    return mla_cos, mla_sin, jnp.tile(c, (1, 4)), jnp.tile(sn, (1, 4))


def _proj_weights(w_in):
    d = w_in.shape[0]
    o = np.cumsum((0, MLA_Q_RANK, MLA_KV_RANK, MLA_ROPE, 512, 512, 512, GDN_HEADS, GDN_HEADS, 512, 1536, 3 * D_MODEL))
    c_q, c_kv, k_rope = w_in[:, o[0]:o[1]], w_in[:, o[1]:o[2]], w_in[:, o[2]:o[3]]
    gdn_qkv = w_in[:, o[3]:o[6]]
    ba = w_in[:, o[6]:o[8]]
    z = w_in[:, o[8]:o[9]]
    moba = w_in[:, o[9]:o[10]]
    gate = w_in[:, o[10]:o[11]]
    zeros = lambda n: jnp.zeros((d, n), w_in.dtype)
    w = jnp.concatenate([gate, gdn_qkv, moba, z, c_q, c_kv, ba, zeros(MLA_NOPE - 2 * GDN_HEADS), k_rope,
                         zeros(LANES - MLA_QK)], axis=1)
    assert w.shape[1] == N_PROJ
    return w.astype(MXU_DTYPE), ba.T.astype(MXU_DTYPE)


def _lane_pad(v, n=LANES):
    return jnp.pad(v, (0, n - v.shape[0]))[None, :]


def kernel(x, ffa_norm, ffa_w_in, ffa_w_out, mix_norm, w_in, mla_cq_norm, mla_ckv_norm, mla_w_uq, mla_w_ukv, mla_q_norm, mla_k_norm, gdn_conv, gdn_a_log, gdn_dt_bias, gdn_out_norm, moba_q_norm, moba_k_norm, w_branch, w_out, ffb_norm, ffb_w_in, ffb_w_out):
    b, s, d = x.shape
    t = b * s
    assert d == D_MODEL and s % 512 == 0 and s // MOBA_BLOCK <= MOBA_MAX_BLOCKS
    depth = ffa_norm.shape[0]
    tm = 512
    mla_cos, mla_sin, moba_cos, moba_sin = _rope_tables(s)
    x = x.reshape(t, d)
    for l in range(depth):
        x = _ffn(x, ffa_norm[l][None, :], ffa_w_in[l].astype(MXU_DTYPE), ffa_w_out[l].astype(MXU_DTYPE),
                 tm=tm, tf=256)

        w_proj, w_bat = _proj_weights(w_in[l])
        proj, bat = _inproj(x, mix_norm[l][None, :], w_proj, w_bat, tm=tm, tn=1024)

        wq = jnp.pad(mla_w_uq[l].reshape(MLA_Q_RANK, MLA_HEADS, MLA_QK),
                     ((0, 0), (0, 0), (0, LANES - MLA_QK))).reshape(MLA_Q_RANK, MLA_HEADS * LANES)
        wkv = mla_w_ukv[l].reshape(MLA_KV_RANK, MLA_HEADS, MLA_NOPE + MLA_V)
        wk = jnp.pad(wkv[:, :, :MLA_NOPE], ((0, 0), (0, 0), (0, LANES - MLA_NOPE))).reshape(MLA_KV_RANK, MLA_HEADS * LANES)
        wv = wkv[:, :, MLA_NOPE:].reshape(MLA_KV_RANK, MLA_HEADS * MLA_V).T
        q, k, v = _mla_prep(proj, mla_cq_norm[l][None, :], mla_ckv_norm[l][None, :], wq.astype(MXU_DTYPE),
                            wk.astype(MXU_DTYPE), wv.astype(MXU_DTYPE), _lane_pad(mla_q_norm[l]),
                            _lane_pad(mla_k_norm[l]), mla_cos, mla_sin, b=b, s=s, tm=tm)
        o_mla = _flash(q, k, v, tq=512)

        bat_chunks = bat.reshape(8, t // GDN_CHUNK, GDN_CHUNK).transpose(1, 0, 2)
        head_pad = lambda v: jnp.pad(v, (GDN_HEADS, LANES - 2 * GDN_HEADS))[None, :]
        head_col = lambda v: jnp.pad(v, (GDN_HEADS, 0))[:, None]
        u, w, qd, kd, qk, egl = _gdn_prep(proj, bat_chunks, gdn_conv[l], head_pad(gdn_a_log[l]),
                                          head_pad(gdn_dt_bias[l]), head_col(gdn_a_log[l]),
                                          head_col(gdn_dt_bias[l]), s=s, tm=tm)
        r3 = lambda a: a.reshape(b, s, a.shape[-1])
        o_gdn = _gdn_scan(r3(u), r3(w), r3(qd), r3(kd), r3(qk), egl.reshape(b, s // GDN_CHUNK, 1, -1),
                          r3(proj), gdn_out_norm[l][None, :], tt=tm)

        q, k, v = _moba_prep(proj, jnp.tile(moba_q_norm[l], 2)[None, :], jnp.tile(moba_k_norm[l], 2)[None, :],
                             moba_cos, moba_sin, b=b, s=s)
        o_moba = _flash(q, k, v, tq=512)

        x = _merge(x, o_mla.reshape(t, -1), o_gdn.reshape(t, -1), o_moba.reshape(t, -1), proj,
                   w_branch[l].astype(MXU_DTYPE), w_out[l].astype(MXU_DTYPE), tm=tm)

        x = _ffn(x, ffb_norm[l][None, :], ffb_w_in[l].astype(MXU_DTYPE), ffb_w_out[l].astype(MXU_DTYPE),
                 tm=tm, tf=256)
    return x.reshape(b, s, d)
```

```python
import functools

import numpy as np
import jax
import jax.numpy as jnp
from jax import lax
from jax.experimental import pallas as pl
from jax.experimental.pallas import tpu as pltpu

F32 = jnp.float32
BF16 = jnp.bfloat16
MXU_DTYPE = BF16
HIGHEST = lax.Precision.HIGHEST

EPS = 1e-6
ROPE_THETA = 10000.0
NEG_INF = -1e30
SENTINEL = -3e38
LOG2E = 1.4426950408889634

D_MODEL = 1024
D_FF = 2816
MIX_WIDTH = 512
LANES = 128

MLA_HEADS = 8
MLA_Q_RANK = 256
MLA_KV_RANK = 128
MLA_NOPE = 64
MLA_ROPE = 32
MLA_V = 64
MLA_QK = MLA_NOPE + MLA_ROPE

GDN_HEADS = 4
GDN_DK = 128
GDN_DV = 128
GDN_CONV = 4
GDN_CHUNK = 64
GDN_HALO = 16
GDN_GROUP = 2

MOBA_HEADS = 8
MOBA_DH = 64
MOBA_BLOCK = 256
MOBA_TOPK = 3
MOBA_MAX_BLOCKS = 32

COL_GATE = 0
COL_GDN = 3072
COL_MOBA = 4608
COL_Z = 6144
COL_MLA = 6656
COL_BA = COL_MLA + 384
N_PROJ = 7168

VMEM_LIMIT = 48 * 1024 * 1024


def _dot(a, b, precision=None):
    return jnp.dot(a, b, preferred_element_type=F32, precision=precision)


def _dot_nt(a, b, precision=None):
    return lax.dot_general(a, b, (((1,), (1,)), ((), ())), preferred_element_type=F32,
                           precision=precision)


def _dot_tn(a, b, precision=None):
    return lax.dot_general(a, b, (((0,), (0,)), ((), ())), preferred_element_type=F32,
                           precision=precision)


def _rms(x, gain, n):
    ms = jnp.sum(x * x, axis=-1, keepdims=True) * (1.0 / n)
    return x * lax.rsqrt(ms + EPS) * gain


def _silu(x):
    return x * jax.nn.sigmoid(x)


def _softplus(x):
    return jnp.maximum(x, 0.0) + jnp.log1p(jnp.exp(-jnp.abs(x)))


def _params(sem):
    return pltpu.CompilerParams(dimension_semantics=sem, vmem_limit_bytes=VMEM_LIMIT)


def _ffn_body(x_ref, g_ref, wi_ref, wo_ref, o_ref, act_scr, *, tf):
    x = x_ref[...]
    h = _rms(x, g_ref[...], D_MODEL).astype(MXU_DTYPE)
    for k in range(D_FF // tf):
        gate = _dot(h, wi_ref[:, k * tf:(k + 1) * tf])
        up = _dot(h, wi_ref[:, D_FF + k * tf:D_FF + (k + 1) * tf])
        act_scr[:, k * tf:(k + 1) * tf] = (_silu(gate) * up).astype(MXU_DTYPE)
    o_ref[...] = x + 0.5 * _dot(act_scr[...], wo_ref[...])


def _resident(shape):
    return pl.BlockSpec(shape, lambda *_: (0,) * len(shape), pipeline_mode=pl.Buffered(1))


def _ffn(x, gain, w_in, w_out, *, tm, tf):
    t, d = x.shape
    return pl.pallas_call(
        functools.partial(_ffn_body, tf=tf),
        grid=(t // tm,),
        in_specs=[
            pl.BlockSpec((tm, d), lambda i: (i, 0)),
            _resident((1, d)),
            _resident((d, 2 * D_FF)),
            _resident((D_FF, d)),
        ],
        out_specs=pl.BlockSpec((tm, d), lambda i: (i, 0)),
        out_shape=jax.ShapeDtypeStruct((t, d), F32),
        scratch_shapes=[pltpu.VMEM((tm, D_FF), MXU_DTYPE)],
        compiler_params=_params(("parallel",)),
        name="ffn",
    )(x, gain, w_in, w_out)


def _inproj_body(x_ref, g_ref, w_ref, wbat_ref, o_ref, tail_ref, bat_ref, *, tn):
    h = _rms(x_ref[...], g_ref[...], D_MODEL).astype(MXU_DTYPE)
    bat_ref[...] = _dot_nt(wbat_ref[...], h)
    n_col = N_PROJ // tn
    for j in range(n_col):
        r = _dot(h, w_ref[:, j * tn:(j + 1) * tn])
        o_ref[:, j * tn:(j + 1) * tn] = r.astype(o_ref.dtype)
        if j == n_col - 1:
            tail_ref[...] = r[:, tn - LANES:]


def _inproj(x, gain, w, wbat, *, tm, tn):
    t, d = x.shape
    return pl.pallas_call(
        functools.partial(_inproj_body, tn=tn),
        grid=(t // tm,),
        in_specs=[
            pl.BlockSpec((tm, d), lambda i: (i, 0)),
            _resident((1, d)),
            _resident((d, N_PROJ)),
            _resident((8, d)),
        ],
        out_specs=[
            pl.BlockSpec((tm, N_PROJ), lambda i: (i, 0)),
            pl.BlockSpec((tm, LANES), lambda i: (i, 0)),
            pl.BlockSpec((8, tm), lambda i: (0, i)),
        ],
        out_shape=[jax.ShapeDtypeStruct((t, N_PROJ), MXU_DTYPE), jax.ShapeDtypeStruct((t, LANES), F32),
                   jax.ShapeDtypeStruct((8, t), F32)],
        compiler_params=_params(("parallel",)),
        name="inproj",
    )(x, gain, w, wbat)


def _mla_rot(x, lane):
    return jnp.where(lane < MLA_NOPE + MLA_ROPE // 2, -pltpu.roll(x, LANES - MLA_ROPE // 2, 1),
                     pltpu.roll(x, MLA_ROPE // 2, 1))


def _mla_prep_body(lat_ref, tail_ref, cqg_ref, ckvg_ref, wq_ref, wk_ref, wv_ref, qg_ref, kg_ref, cos_ref,
                   sin_ref, q_ref, k_ref, v_ref, *, scale):
    lat = lat_ref[...].astype(F32)
    tm = lat.shape[0]
    cqn = _rms(lat[:, :MLA_Q_RANK], cqg_ref[...], MLA_Q_RANK).astype(MXU_DTYPE)
    ckvn = _rms(lat[:, MLA_Q_RANK:MLA_Q_RANK + MLA_KV_RANK], ckvg_ref[...], MLA_KV_RANK).astype(MXU_DTYPE)
    lane = lax.broadcasted_iota(jnp.int32, (tm, LANES), 1)
    kr = jnp.where((lane >= MLA_NOPE) & (lane < MLA_QK), tail_ref[...], 0.0)
    q_all = _dot(cqn, wq_ref[...])
    k_all = _dot(ckvn, wk_ref[...])
    v_ref[0] = _dot_nt(wv_ref[...], ckvn).astype(v_ref.dtype)
    cos = cos_ref[...]
    sin = sin_ref[...]
    qg = qg_ref[...]
    kg = kg_ref[...]
    for h in range(MLA_HEADS):
        sl = slice(h * LANES, (h + 1) * LANES)
        qn = _rms(q_all[:, sl], qg, MLA_QK)
        q_ref[0, h] = ((qn * cos + _mla_rot(qn, lane) * sin) * scale).astype(q_ref.dtype)
        kn = _rms(k_all[:, sl] + kr, kg, MLA_QK)
        k_ref[0, h] = (kn * cos + _mla_rot(kn, lane) * sin).astype(k_ref.dtype)


def _mla_prep(proj, tail, cqg, ckvg, wq, wk, wv, qg, kg, cos, sin, *, b, s, tm):
    nt = s // tm
    scale = (MLA_QK ** -0.5) * LOG2E
    cblk = COL_MLA // 512
    full = lambda shape: pl.BlockSpec(shape, lambda bi, i: (0,) * len(shape))
    return pl.pallas_call(
        functools.partial(_mla_prep_body, scale=scale),
        grid=(b, nt),
        in_specs=[
            pl.BlockSpec((tm, 512), lambda bi, i: (bi * nt + i, cblk)),
            pl.BlockSpec((tm, LANES), lambda bi, i: (bi * nt + i, 0)),
            full((1, MLA_Q_RANK)), full((1, MLA_KV_RANK)),
            full((MLA_Q_RANK, MLA_HEADS * LANES)), full((MLA_KV_RANK, MLA_HEADS * LANES)),
            full((MIX_WIDTH, MLA_KV_RANK)), full((1, LANES)), full((1, LANES)),
            pl.BlockSpec((tm, LANES), lambda bi, i: (i, 0)),
            pl.BlockSpec((tm, LANES), lambda bi, i: (i, 0)),
        ],
        out_specs=[
            pl.BlockSpec((1, MLA_HEADS, tm, LANES), lambda bi, i: (bi, 0, i, 0)),
            pl.BlockSpec((1, MLA_HEADS, tm, LANES), lambda bi, i: (bi, 0, i, 0)),
            pl.BlockSpec((1, MIX_WIDTH, tm), lambda bi, i: (bi, 0, i)),
        ],
        out_shape=[
            jax.ShapeDtypeStruct((b, MLA_HEADS, s, LANES), MXU_DTYPE),
            jax.ShapeDtypeStruct((b, MLA_HEADS, s, LANES), MXU_DTYPE),
            jax.ShapeDtypeStruct((b, MIX_WIDTH, s), MXU_DTYPE),
        ],
        compiler_params=_params(("parallel", "parallel")),
        name="mla_prep",
    )(proj, tail, cqg, ckvg, wq, wk, wv, qg, kg, cos, sin)


def _moba_rot(x, lane):
    half = MOBA_DH // 2
    return jnp.where(lane % MOBA_DH < half, -pltpu.roll(x, LANES - half, 1), pltpu.roll(x, half, 1))


def _pair_rms(x, gain, lo_mask):
    sq = x * x
    ss_lo = jnp.sum(jnp.where(lo_mask, sq, 0.0), axis=-1, keepdims=True)
    ss_hi = jnp.sum(jnp.where(lo_mask, 0.0, sq), axis=-1, keepdims=True)
    r = jnp.where(lo_mask, lax.rsqrt(ss_lo * (1.0 / MOBA_DH) + EPS), lax.rsqrt(ss_hi * (1.0 / MOBA_DH) + EPS))
    return x * r * gain


def _select_bias(gate, lane, lane_f, lo, j):
    g = jnp.where((lane >= lo) & (lane < lo + j), gate, SENTINEL)
    allowed = lane == lo + j
    for _ in range(MOBA_TOPK):
        m = jnp.max(g, axis=1, keepdims=True)
        first = jnp.min(jnp.where(g == m, lane_f, 1e9), axis=1, keepdims=True)
        pick = (lane_f == first) & (m > SENTINEL)
        allowed = allowed | pick
        g = jnp.where(pick, SENTINEL, g)
    return jnp.where(allowed, 0.0, NEG_INF)


def _moba_prep_body(qkv_ref, qg_ref, kg_ref, cos_ref, sin_ref, q_ref, k_ref, v_ref, kmt_scr, *, scale):
    j = pl.program_id(1)

    @pl.when(j == 0)
    def _():
        kmt_scr[...] = jnp.zeros_like(kmt_scr)

    tq = qkv_ref.shape[0]
    hw = MOBA_HEADS * MOBA_DH
    lane = lax.broadcasted_iota(jnp.int32, (tq, LANES), 1)
    lane_f = lane.astype(F32)
    lo_mask = lane < MOBA_DH
    cos = cos_ref[...]
    sin = sin_ref[...]
    v_ref[0] = qkv_ref[:, 2 * hw:].astype(F32).T.astype(v_ref.dtype)
    row_lane = lax.broadcasted_iota(jnp.int32, (1, LANES), 1)
    slot = lax.broadcasted_iota(jnp.int32, (LANES, LANES), 0)
    for p in range(MOBA_HEADS // 2):
        sl = slice(p * LANES, (p + 1) * LANES)
        qn = _pair_rms(qkv_ref[:, sl].astype(F32), qg_ref[...], lo_mask)
        q = qn * cos + _moba_rot(qn, lane) * sin
        kn = _pair_rms(qkv_ref[:, hw + p * LANES:hw + (p + 1) * LANES].astype(F32), kg_ref[...], lo_mask)
        k = kn * cos + _moba_rot(kn, lane) * sin
        gate = _dot_nt(q, kmt_scr[p], precision=HIGHEST)
        bias_e = _select_bias(gate, lane, lane_f, MOBA_DH, j)
        bias_o = _select_bias(gate, lane, lane_f, 0, j)
        qs = q * scale
        q_ref[0, 2 * p] = jnp.where(lo_mask, qs, jnp.where(lane < MOBA_DH + MOBA_MAX_BLOCKS, bias_e, 0.0)).astype(q_ref.dtype)
        q_ref[0, 2 * p + 1] = jnp.where(lo_mask, jnp.where(lane < MOBA_MAX_BLOCKS, bias_o, 0.0), qs).astype(q_ref.dtype)
        k_ref[0, 2 * p] = jnp.where(lo_mask, k, jnp.where(lane == MOBA_DH + j, 1.0, 0.0)).astype(k_ref.dtype)
        k_ref[0, 2 * p + 1] = jnp.where(lo_mask, jnp.where(lane == j, 1.0, 0.0), k).astype(k_ref.dtype)
        km = jnp.sum(k, axis=0, keepdims=True) * (1.0 / MOBA_BLOCK)
        new_rows = jnp.where(slot == MOBA_DH + j, jnp.where(row_lane < MOBA_DH, km, 0.0),
                             jnp.where(row_lane < MOBA_DH, 0.0, km))
        kmt_scr[p] = jnp.where((slot == MOBA_DH + j) | (slot == j), new_rows, kmt_scr[p])


def _moba_prep(proj, qg, kg, cos, sin, *, b, s):
    tq = MOBA_BLOCK
    nt = s // tq
    scale = (MOBA_DH ** -0.5) * LOG2E
    cblk = COL_MOBA // (3 * MOBA_HEADS * MOBA_DH)
    return pl.pallas_call(
        functools.partial(_moba_prep_body, scale=scale),
        grid=(b, nt),
        in_specs=[
            pl.BlockSpec((tq, 3 * MOBA_HEADS * MOBA_DH), lambda bi, j: (bi * nt + j, cblk)),
            pl.BlockSpec((1, LANES), lambda bi, j: (0, 0)),
            pl.BlockSpec((1, LANES), lambda bi, j: (0, 0)),
            pl.BlockSpec((tq, LANES), lambda bi, j: (j, 0)),
            pl.BlockSpec((tq, LANES), lambda bi, j: (j, 0)),
        ],
        out_specs=[
            pl.BlockSpec((1, MOBA_HEADS, tq, LANES), lambda bi, j: (bi, 0, j, 0)),
            pl.BlockSpec((1, MOBA_HEADS, tq, LANES), lambda bi, j: (bi, 0, j, 0)),
            pl.BlockSpec((1, MIX_WIDTH, tq), lambda bi, j: (bi, 0, j)),
        ],
        out_shape=[
            jax.ShapeDtypeStruct((b, MOBA_HEADS, s, LANES), MXU_DTYPE),
            jax.ShapeDtypeStruct((b, MOBA_HEADS, s, LANES), MXU_DTYPE),
            jax.ShapeDtypeStruct((b, MIX_WIDTH, s), MXU_DTYPE),
        ],
        scratch_shapes=[pltpu.VMEM((MOBA_HEADS // 2, LANES, LANES), F32)],
        compiler_params=_params(("parallel", "arbitrary")),
        name="moba_prep",
    )(proj, qg, kg, cos, sin)


def _flash_body(q_ref, k_ref, vt_ref, o_ref, buf0, buf1, *, tq):
    i = pl.program_id(2)
    qs = (q_ref[0, 0], q_ref[0, 1])
    dv = LANES // 2

    def scores(t, buf, diagonal):
        ks = pl.ds(pl.multiple_of(t * tq, tq), tq)
        out = []
        for hh in range(2):
            s = _dot_nt(k_ref[0, hh, ks, :], qs[hh])
            if diagonal:
                key = lax.broadcasted_iota(jnp.int32, s.shape, 0)
                qry = lax.broadcasted_iota(jnp.int32, s.shape, 1)
                s = jnp.where(key <= qry, s, NEG_INF)
            buf[hh] = s
            out.append(jnp.max(s, axis=0, keepdims=True))
        return tuple(out)

    def consume(t, buf, smax, stats):
        ks = pl.ds(pl.multiple_of(t * tq, tq), tq)
        out = []
        for hh in range(2):
            m, l, acc = stats[3 * hh:3 * hh + 3]
            m_new = jnp.maximum(m, smax[hh])
            alpha = jnp.exp2(m - m_new)
            p = jnp.exp2(buf[hh] - m_new)
            l = alpha * l + jnp.sum(p, axis=0, keepdims=True)
            acc = alpha * acc + _dot(vt_ref[0, hh * dv:(hh + 1) * dv, ks], p.astype(MXU_DTYPE))
            out += [m_new, l, acc]
        return tuple(out)

    def finish(stats):
        o_t = jnp.concatenate([stats[2] / stats[1], stats[5] / stats[4]], axis=0)
        o_ref[0] = o_t.T.astype(o_ref.dtype)

    init = (jnp.full((1, tq), NEG_INF, F32), jnp.zeros((1, tq), F32), jnp.zeros((dv, tq), F32)) * 2

    @pl.when(i % 2 == 0)
    def _():
        def pair(u, carry):
            smax, stats = carry
            t0 = 2 * u
            smax1 = scores(t0, buf1, False)
            stats = consume(jnp.where(u == 0, i, t0 - 1), buf0, smax, stats)
            smax0 = scores(t0 + 1, buf0, False)
            return smax0, consume(t0, buf1, smax1, stats)

        smax, stats = lax.fori_loop(0, i // 2, pair, (scores(i, buf0, True), init))
        finish(consume(jnp.where(i == 0, 0, i - 1), buf0, smax, stats))

    @pl.when(i % 2 == 1)
    def _():
        smax0 = scores(i, buf0, True)
        smax1 = scores(0, buf1, False)
        stats = consume(i, buf0, smax0, init)

        def pair(u, carry):
            smax, stats = carry
            t0 = 2 * u + 1
            smax0 = scores(t0, buf0, False)
            stats = consume(t0 - 1, buf1, smax, stats)
            smax1 = scores(t0 + 1, buf1, False)
            return smax1, consume(t0, buf0, smax0, stats)

        smax, stats = lax.fori_loop(0, i // 2, pair, (smax1, stats))
        finish(consume(i - 1, buf1, smax, stats))


def _flash(q, k, vt, *, tq):
    b, h, s, _ = q.shape
    return pl.pallas_call(
        functools.partial(_flash_body, tq=tq),
        grid=(b, h // 2, s // tq),
        in_specs=[
            pl.BlockSpec((1, 2, tq, LANES), lambda bi, p, i: (bi, p, i, 0)),
            pl.BlockSpec((1, 2, s, LANES), lambda bi, p, i: (bi, p, 0, 0)),
            pl.BlockSpec((1, LANES, s), lambda bi, p, i: (bi, p, 0)),
        ],
        out_specs=pl.BlockSpec((1, tq, LANES), lambda bi, p, i: (bi, i, p)),
        out_shape=jax.ShapeDtypeStruct((b, s, MIX_WIDTH), MXU_DTYPE),
        scratch_shapes=[pltpu.VMEM((2, tq, tq), F32), pltpu.VMEM((2, tq, tq), F32)],
        compiler_params=_params(("parallel", "parallel", "arbitrary")),
        name="flash",
    )(q, k, vt)


def _hi_lo(x):
    hi = x.astype(MXU_DTYPE)
    return hi, (x - hi.astype(F32)).astype(MXU_DTYPE)


def _unit_lower_solve(a_list, rhs_list, eye):
    c = eye.shape[0]
    n = range(len(a_list))
    t = [eye - a for a in a_list]
    pk = []
    for a in a_list:
        pwm = (-a).astype(MXU_DTYPE)
        pk.append(_dot(pwm, pwm))
    n_sq = int(np.log2(c)) - 1
    for it in range(n_sq):
        if it < n_sq - 1:
            both = [_dot(jnp.concatenate([t[j], pk[j]], axis=0).astype(MXU_DTYPE), pk[j].astype(MXU_DTYPE))
                    for j in n]
            t = [t[j] + both[j][:c] for j in n]
            pk = [both[j][c:] for j in n]
        else:
            t = [t[j] + _dot(t[j].astype(MXU_DTYPE), pk[j].astype(MXU_DTYPE)) for j in n]
    tm = [x.astype(MXU_DTYPE) for x in t]
    x1 = [_dot(tm[j], rhs_list[j].astype(MXU_DTYPE)) for j in n]
    a_split = [_hi_lo(a) for a in a_list]
    x_split = [_hi_lo(x) for x in x1]
    ax = [_dot(jnp.concatenate(a_split[j], axis=0), x_split[j][0]) for j in n]
    ax_lo = [_dot(a_split[j][0], x_split[j][1]) for j in n]
    res = [rhs_list[j] - x1[j] - (ax[j][:c] + ax[j][c:] + ax_lo[j]) for j in n]
    return [x1[j] + _dot(tm[j], res[j].astype(MXU_DTYPE)) for j in n]


def _gdn_prep_body(qkv_ref, halo_ref, ba_ref, bat_ref, conv_ref, alog_ref, dtb_ref, alogt_ref, dtbt_ref,
                   u_ref, w_ref, qd_ref, kd_ref, qk_ref, egl_ref, xe_scr, y_scr, bg_scr,
                   *, tiles_per_seq, nc):
    i = pl.program_id(0)
    tm = qkv_ref.shape[0]
    c = GDN_CHUNK
    hk = GDN_HEADS * GDN_DK

    xe_scr[pl.ds(0, GDN_HALO), :] = jnp.where(i % tiles_per_seq == 0, 0.0, halo_ref[...].astype(F32))
    xe_scr[pl.ds(GDN_HALO, tm), :] = qkv_ref[...].astype(F32)
    cw = conv_ref[...]
    y = xe_scr[pl.ds(GDN_HALO, tm), :] * cw[GDN_CONV - 1:GDN_CONV]
    for back in range(1, GDN_CONV):
        y = y + xe_scr[pl.ds(GDN_HALO - back, tm), :] * cw[GDN_CONV - 1 - back:GDN_CONV - back]
    y_scr[...] = _silu(y)

    ba = ba_ref[...]
    lane = lax.broadcasted_iota(jnp.int32, ba.shape, 1)
    g_col = -jnp.exp(alog_ref[...]) * _softplus(ba + dtb_ref[...])
    bg_scr[...] = jnp.where(lane < GDN_HEADS, jax.nn.sigmoid(ba), g_col)

    ri = lax.broadcasted_iota(jnp.int32, (c, c), 0)
    ci = lax.broadcasted_iota(jnp.int32, (c, c), 1)
    tril = ci <= ri
    strict = ci < ri
    eye = jnp.where(ci == ri, 1.0, 0.0)
    lower_ones = jnp.where(tril, 1.0, 0.0)
    upper_ones = jnp.where(ci >= ri, 1.0, 0.0)

    def chunk_group(grp, _):
        chs = [grp * GDN_GROUP + j for j in range(GDN_GROUP)]
        rows = [pl.ds(pl.multiple_of(ch * c, c), c) for ch in chs]
        bg = [bg_scr[r, :] for r in rows]
        gcum_col = [_dot(lower_ones, x, precision=HIGHEST) for x in bg]
        g_row = [-jnp.exp(alogt_ref[...]) * _softplus(bat_ref[ch] + dtbt_ref[...]) for ch in chs]
        gcum_row = [_dot(x, upper_ones, precision=HIGHEST) for x in g_row]
        probs = [(j, h) for j in range(GDN_GROUP) for h in range(GDN_HEADS)]
        n = range(len(probs))
        sls = [slice(h * GDN_DK, (h + 1) * GDN_DK) for _, h in probs]
        qh = [y_scr[rows[j], sls[i]] for i, (j, h) in enumerate(probs)]
        kh = [y_scr[rows[j], pl.ds(hk + h * GDN_DK, GDN_DK)] for j, h in probs]
        vh = [y_scr[rows[j], pl.ds(2 * hk + h * GDN_DV, GDN_DV)] for j, h in probs]
        qn = [x * lax.rsqrt(jnp.sum(x * x, axis=-1, keepdims=True) + EPS) * (GDN_DK ** -0.5) for x in qh]
        kn = [x * lax.rsqrt(jnp.sum(x * x, axis=-1, keepdims=True) + EPS) for x in kh]
        beta = [bg[j][:, h:h + 1] for j, h in probs]
        gc = [gcum_col[j][:, GDN_HEADS + h:GDN_HEADS + h + 1] for j, h in probs]
        gr = [gcum_row[j][GDN_HEADS + h:GDN_HEADS + h + 1, :] for j, h in probs]
        kb = [kn[i] * beta[i] for i in n]
        knm = [x.astype(MXU_DTYPE) for x in kn]
        dec = [jnp.exp(jnp.where(tril, gc[i] - gr[i], NEG_INF)) for i in n]
        skk = [_dot_nt(kb[i].astype(MXU_DTYPE), knm[i]) for i in n]
        sqk = [_dot_nt(qn[i].astype(MXU_DTYPE), knm[i]) for i in n]
        a = [jnp.where(strict, skk[i] * dec[i], 0.0) for i in n]
        eg = [jnp.exp(x) for x in gc]
        rhs = [jnp.concatenate([vh[i] * beta[i], kb[i] * eg[i]], axis=1) for i in n]
        sol = _unit_lower_solve(a, rhs, eye)
        for i, (j, h) in enumerate(probs):
            sl, r = sls[i], rows[j]
            gl = gc[i][c - 1:c, :]
            u_ref[r, sl] = sol[i][:, :GDN_DV]
            w_ref[r, sl] = sol[i][:, GDN_DV:].astype(w_ref.dtype)
            qd_ref[r, sl] = (qn[i] * eg[i]).astype(qd_ref.dtype)
            kd_ref[r, sl] = (kn[i] * jnp.exp(gl - gc[i])).astype(kd_ref.dtype)
            qk_ref[r, pl.ds(h * LANES, c)] = (sqk[i] * dec[i]).astype(qk_ref.dtype)
            qk_ref[r, pl.ds(h * LANES + c, LANES - c)] = jnp.zeros((c, LANES - c), qk_ref.dtype)
            egl_ref[chs[j], :, sl] = jnp.broadcast_to(jnp.exp(gl), (1, GDN_DV))
        return 0

    lax.fori_loop(0, nc // GDN_GROUP, chunk_group, 0)


def _gdn_prep(proj, tail, bat, conv, alog, dtb, alogt, dtbt, *, s, tm):
    t = proj.shape[0]
    nc = tm // GDN_CHUNK
    hk = GDN_HEADS * GDN_DK
    width = 3 * hk
    cblk = COL_GDN // width
    full = lambda shape: pl.BlockSpec(shape, lambda i: (0,) * len(shape))
    row = lambda w: pl.BlockSpec((tm, w), lambda i: (i, 0))
    return pl.pallas_call(
        functools.partial(_gdn_prep_body, tiles_per_seq=s // tm, nc=nc),
        grid=(t // tm,),
        in_specs=[
            pl.BlockSpec((tm, width), lambda i: (i, cblk)),
            pl.BlockSpec((GDN_HALO, width), lambda i: (jnp.maximum(i * (tm // GDN_HALO) - 1, 0), cblk)),
            pl.BlockSpec((tm, LANES), lambda i: (i, 0)),
            pl.BlockSpec((nc, 8, GDN_CHUNK), lambda i: (i, 0, 0)),
            full((GDN_CONV, width)), full((1, LANES)), full((1, LANES)), full((8, 1)), full((8, 1)),
        ],
        out_specs=[row(hk), row(hk), row(hk), row(hk), row(GDN_HEADS * LANES),
                   pl.BlockSpec((nc, 1, hk), lambda i: (i, 0, 0))],
        out_shape=[
            jax.ShapeDtypeStruct((t, hk), F32),
            jax.ShapeDtypeStruct((t, hk), MXU_DTYPE),
            jax.ShapeDtypeStruct((t, hk), MXU_DTYPE),
            jax.ShapeDtypeStruct((t, hk), MXU_DTYPE),
            jax.ShapeDtypeStruct((t, GDN_HEADS * LANES), MXU_DTYPE),
            jax.ShapeDtypeStruct((t // GDN_CHUNK, 1, hk), F32),
        ],
        scratch_shapes=[
            pltpu.VMEM((tm + GDN_HALO, width), F32),
            pltpu.VMEM((tm, width), F32),
            pltpu.VMEM((tm, LANES), F32),
        ],
        compiler_params=_params(("parallel",)),
        name="gdn_prep",
    )(proj, proj, tail, bat, conv, alog, dtb, alogt, dtbt)


def _gdn_scan_body(u_ref, w_ref, qd_ref, kd_ref, qk_ref, egl_ref, z_ref, og_ref, o_ref, s_scr, *, nb, nc):
    @pl.when(pl.program_id(0) == 0)
    def _():
        s_scr[...] = jnp.zeros_like(s_scr)

    c = GDN_CHUNK
    og = og_ref[...]

    def chunk(ch, _):
        rows = pl.ds(pl.multiple_of(ch * c, c), c)
        for bi in range(nb):
            for h in range(GDN_HEADS):
                sl = slice(h * GDN_DV, (h + 1) * GDN_DV)
                st = s_scr[bi * GDN_HEADS + h]
                stm = st.astype(MXU_DTYPE)
                v_new = u_ref[bi, rows, sl] - _dot(w_ref[bi, rows, sl], stm)
                vm = v_new.astype(MXU_DTYPE)
                o = _dot(qd_ref[bi, rows, sl], stm) + _dot(qk_ref[bi, rows, pl.ds(h * LANES, c)], vm)
                s_scr[bi * GDN_HEADS + h] = st * egl_ref[bi, ch, :, sl] + _dot_tn(kd_ref[bi, rows, sl], vm)
                o_ref[bi, rows, sl] = (_rms(o, og, GDN_DV) * _silu(z_ref[bi, rows, sl].astype(F32))).astype(o_ref.dtype)
        return 0

    lax.fori_loop(0, nc, chunk, 0)


def _gdn_scan(u, w, qd, kd, qk, egl, proj3, og, *, tt):
    b, s, hk = u.shape
    nc = tt // GDN_CHUNK
    blk = lambda width: pl.BlockSpec((b, tt, width), lambda i: (0, i, 0))
    return pl.pallas_call(
        functools.partial(_gdn_scan_body, nb=b, nc=nc),
        grid=(s // tt,),
        in_specs=[
            blk(hk), blk(hk), blk(hk), blk(hk), blk(GDN_HEADS * LANES),
            pl.BlockSpec((b, nc, 1, hk), lambda i: (0, i, 0, 0)),
            pl.BlockSpec((b, tt, hk), lambda i: (0, i, COL_Z // hk)),
            pl.BlockSpec((1, GDN_DV), lambda i: (0, 0)),
        ],
        out_specs=blk(hk),
        out_shape=jax.ShapeDtypeStruct((b, s, hk), MXU_DTYPE),
        scratch_shapes=[pltpu.VMEM((b * GDN_HEADS, GDN_DK, GDN_DV), F32)],
        compiler_params=_params(("arbitrary",)),
        name="gdn_scan",
    )(u, w, qd, kd, qk, egl, proj3, og)


def _merge_body(x_ref, om_ref, og_ref, ob_ref, gl_ref, wb_ref, wo_ref, o_ref):
    mixed = None
    for n, br in enumerate((om_ref, og_ref, ob_ref)):
        up = _dot(br[...], wb_ref[n])
        term = jax.nn.sigmoid(gl_ref[:, n * D_MODEL:(n + 1) * D_MODEL].astype(F32)) * up
        mixed = term if mixed is None else mixed + term
    o_ref[...] = x_ref[...] + _dot(mixed.astype(MXU_DTYPE), wo_ref[...])


def _merge(x, o_mla, o_gdn, o_moba, proj, wb, wo, *, tm):
    t, d = x.shape
    row = lambda w: pl.BlockSpec((tm, w), lambda i: (i, 0))
    return pl.pallas_call(
        _merge_body,
        grid=(t // tm,),
        in_specs=[
            row(d), row(MIX_WIDTH), row(MIX_WIDTH), row(MIX_WIDTH), row(3 * d),
            pl.BlockSpec((3, MIX_WIDTH, d), lambda i: (0, 0, 0)),
            pl.BlockSpec((d, d), lambda i: (0, 0)),
        ],
        out_specs=row(d),
        out_shape=jax.ShapeDtypeStruct((t, d), F32),
        compiler_params=_params(("parallel",)),
        name="merge",
    )(x, o_mla, o_gdn, o_moba, proj, wb, wo)


def _rope_tables(s):
    pos = jnp.arange(s, dtype=jnp.int32).astype(F32)[:, None]

    def cs(d):
        half = d // 2
        inv_freq = ROPE_THETA ** (-jnp.arange(half, dtype=F32) * 2.0 / d)
        ang = pos * inv_freq[None, :]
        return jnp.cos(ang), jnp.sin(ang)

    c, sn = cs(MLA_ROPE)
    pad = LANES - MLA_QK
    mla_cos = jnp.concatenate([jnp.ones((s, MLA_NOPE), F32), c, c, jnp.zeros((s, pad), F32)], axis=1)
    mla_sin = jnp.concatenate([jnp.zeros((s, MLA_NOPE), F32), sn, sn, jnp.zeros((s, pad), F32)], axis=1)
    c, sn = cs(MOBA_DH)
    return mla_cos, mla_sin, jnp.tile(c, (1, 4)), jnp.tile(sn, (1, 4))


def _proj_weights(w_in):
    d = w_in.shape[0]
    o = np.cumsum((0, MLA_Q_RANK, MLA_KV_RANK, MLA_ROPE, 512, 512, 512, GDN_HEADS, GDN_HEADS, 512, 1536, 3 * D_MODEL))
    c_q, c_kv, k_rope = w_in[:, o[0]:o[1]], w_in[:, o[1]:o[2]], w_in[:, o[2]:o[3]]
    gdn_qkv = w_in[:, o[3]:o[6]]
    ba = w_in[:, o[6]:o[8]]
    z = w_in[:, o[8]:o[9]]
    moba = w_in[:, o[9]:o[10]]
    gate = w_in[:, o[10]:o[11]]
    zeros = lambda n: jnp.zeros((d, n), w_in.dtype)
    w = jnp.concatenate([gate, gdn_qkv, moba, z, c_q, c_kv, ba, zeros(MLA_NOPE - 2 * GDN_HEADS), k_rope,
                         zeros(LANES - MLA_QK)], axis=1)
    assert w.shape[1] == N_PROJ
    return w.astype(MXU_DTYPE), ba.T.astype(MXU_DTYPE)


def _lane_pad(v, n=LANES):
    return jnp.pad(v, (0, n - v.shape[0]))[None, :]


def kernel(x, ffa_norm, ffa_w_in, ffa_w_out, mix_norm, w_in, mla_cq_norm, mla_ckv_norm, mla_w_uq, mla_w_ukv, mla_q_norm, mla_k_norm, gdn_conv, gdn_a_log, gdn_dt_bias, gdn_out_norm, moba_q_norm, moba_k_norm, w_branch, w_out, ffb_norm, ffb_w_in, ffb_w_out):
    b, s, d = x.shape
    t = b * s
    assert d == D_MODEL and s % 512 == 0 and s // MOBA_BLOCK <= MOBA_MAX_BLOCKS
    depth = ffa_norm.shape[0]
    tm = 512
    mla_cos, mla_sin, moba_cos, moba_sin = _rope_tables(s)
    x = x.reshape(t, d)
    for l in range(depth):
        x = _ffn(x, ffa_norm[l][None, :], ffa_w_in[l].astype(MXU_DTYPE), ffa_w_out[l].astype(MXU_DTYPE),
                 tm=tm, tf=256)

        w_proj, w_bat = _proj_weights(w_in[l])
        proj, tail, bat = _inproj(x, mix_norm[l][None, :], w_proj, w_bat, tm=tm, tn=512)

        wq = jnp.pad(mla_w_uq[l].reshape(MLA_Q_RANK, MLA_HEADS, MLA_QK),
                     ((0, 0), (0, 0), (0, LANES - MLA_QK))).reshape(MLA_Q_RANK, MLA_HEADS * LANES)
        wkv = mla_w_ukv[l].reshape(MLA_KV_RANK, MLA_HEADS, MLA_NOPE + MLA_V)
        wk = jnp.pad(wkv[:, :, :MLA_NOPE], ((0, 0), (0, 0), (0, LANES - MLA_NOPE))).reshape(MLA_KV_RANK, MLA_HEADS * LANES)
        wv = wkv[:, :, MLA_NOPE:].reshape(MLA_KV_RANK, MLA_HEADS * MLA_V).T
        q, k, v = _mla_prep(proj, tail, mla_cq_norm[l][None, :], mla_ckv_norm[l][None, :], wq.astype(MXU_DTYPE),
                            wk.astype(MXU_DTYPE), wv.astype(MXU_DTYPE), _lane_pad(mla_q_norm[l]),
                            _lane_pad(mla_k_norm[l]), mla_cos, mla_sin, b=b, s=s, tm=tm)
        o_mla = _flash(q, k, v, tq=512)

        bat_chunks = bat.reshape(8, t // GDN_CHUNK, GDN_CHUNK).transpose(1, 0, 2)
        head_pad = lambda v: jnp.pad(v, (GDN_HEADS, LANES - 2 * GDN_HEADS))[None, :]
        head_col = lambda v: jnp.pad(v, (GDN_HEADS, 0))[:, None]
        u, w, qd, kd, qk, egl = _gdn_prep(proj, tail, bat_chunks, gdn_conv[l], head_pad(gdn_a_log[l]),
                                          head_pad(gdn_dt_bias[l]), head_col(gdn_a_log[l]),
                                          head_col(gdn_dt_bias[l]), s=s, tm=tm)
        r3 = lambda a: a.reshape(b, s, a.shape[-1])
        o_gdn = _gdn_scan(r3(u), r3(w), r3(qd), r3(kd), r3(qk), egl.reshape(b, s // GDN_CHUNK, 1, -1),
                          r3(proj), gdn_out_norm[l][None, :], tt=tm)

        q, k, v = _moba_prep(proj, jnp.tile(moba_q_norm[l], 2)[None, :], jnp.tile(moba_k_norm[l], 2)[None, :],
                             moba_cos, moba_sin, b=b, s=s)
        o_moba = _flash(q, k, v, tq=512)

        x = _merge(x, o_mla.reshape(t, -1), o_gdn.reshape(t, -1), o_moba.reshape(t, -1), proj,
                   w_branch[l].astype(MXU_DTYPE), w_out[l].astype(MXU_DTYPE), tm=tm)

        x = _ffn(x, ffb_norm[l][None, :], ffb_w_in[l].astype(MXU_DTYPE), ffb_w_out[l].astype(MXU_DTYPE),
                 tm=tm, tf=256)
    return x.reshape(b, s, d)
```

```python
import functools

import numpy as np
import jax
import jax.numpy as jnp
from jax import lax
from jax.experimental import pallas as pl
from jax.experimental.pallas import tpu as pltpu

F32 = jnp.float32
BF16 = jnp.bfloat16
MXU_DTYPE = BF16
HIGHEST = lax.Precision.HIGHEST

EPS = 1e-6
ROPE_THETA = 10000.0
NEG_INF = -1e30
SENTINEL = -3e38
LOG2E = 1.4426950408889634

D_MODEL = 1024
D_FF = 2816
MIX_WIDTH = 512
LANES = 128

MLA_HEADS = 8
MLA_Q_RANK = 256
MLA_KV_RANK = 128
MLA_NOPE = 64
MLA_ROPE = 32
MLA_V = 64
MLA_QK = MLA_NOPE + MLA_ROPE

GDN_HEADS = 4
GDN_DK = 128
GDN_DV = 128
GDN_CONV = 4
GDN_CHUNK = 64
GDN_HALO = 16
GDN_GROUP = 2

MOBA_HEADS = 8
MOBA_DH = 64
MOBA_BLOCK = 256
MOBA_TOPK = 3
MOBA_MAX_BLOCKS = 32
V_ROWS = 80

COL_GATE = 0
COL_GDN = 3072
COL_MOBA = 4608
COL_Z = 6144
COL_MLA = 6656
N_PROJ = 7168

VMEM_LIMIT = 48 * 1024 * 1024


def _dot(a, b, precision=None):
    return jnp.dot(a, b, preferred_element_type=F32, precision=precision)


def _dot_nt(a, b, precision=None):
    return lax.dot_general(a, b, (((1,), (1,)), ((), ())), preferred_element_type=F32,
                           precision=precision)


def _dot_tn(a, b, precision=None):
    return lax.dot_general(a, b, (((0,), (0,)), ((), ())), preferred_element_type=F32,
                           precision=precision)


def _rms(x, gain, n):
    ms = jnp.sum(x * x, axis=-1, keepdims=True) * (1.0 / n)
    return x * lax.rsqrt(ms + EPS) * gain


def _silu(x):
    return x * jax.nn.sigmoid(x)


def _softplus(x):
    return jnp.maximum(x, 0.0) + jnp.log1p(jnp.exp(-jnp.abs(x)))


def _params(sem):
    return pltpu.CompilerParams(dimension_semantics=sem, vmem_limit_bytes=VMEM_LIMIT)


def _ffn_body(x_ref, g_ref, wi_ref, wo_ref, o_ref, act_scr, *, tf):
    x = x_ref[...]
    h = _rms(x, g_ref[...], D_MODEL).astype(MXU_DTYPE)
    for k in range(D_FF // tf):
        gate = _dot(h, wi_ref[:, k * tf:(k + 1) * tf])
        up = _dot(h, wi_ref[:, D_FF + k * tf:D_FF + (k + 1) * tf])
        act_scr[:, k * tf:(k + 1) * tf] = (_silu(gate) * up).astype(MXU_DTYPE)
    o_ref[...] = x + 0.5 * _dot(act_scr[...], wo_ref[...])


def _resident(shape):
    return pl.BlockSpec(shape, lambda *_: (0,) * len(shape), pipeline_mode=pl.Buffered(1))


def _ffn(x, gain, w_in, w_out, *, tm, tf):
    t, d = x.shape
    return pl.pallas_call(
        functools.partial(_ffn_body, tf=tf),
        grid=(t // tm,),
        in_specs=[
            pl.BlockSpec((tm, d), lambda i: (i, 0)),
            _resident((1, d)),
            _resident((d, 2 * D_FF)),
            _resident((D_FF, d)),
        ],
        out_specs=pl.BlockSpec((tm, d), lambda i: (i, 0)),
        out_shape=jax.ShapeDtypeStruct((t, d), F32),
        scratch_shapes=[pltpu.VMEM((tm, D_FF), MXU_DTYPE)],
        compiler_params=_params(("parallel",)),
        name="ffn",
    )(x, gain, w_in, w_out)


def _inproj_body(x_ref, g_ref, w_ref, wbat_ref, o_ref, tail_ref, bat_ref, *, tn):
    h = _rms(x_ref[...], g_ref[...], D_MODEL).astype(MXU_DTYPE)
    bat_ref[...] = _dot_nt(wbat_ref[...], h)
    n_col = N_PROJ // tn
    for j in range(n_col):
        r = _dot(h, w_ref[:, j * tn:(j + 1) * tn])
        o_ref[:, j * tn:(j + 1) * tn] = r.astype(o_ref.dtype)
        if j == n_col - 1:
            tail_ref[...] = r[:, tn - LANES:]


def _inproj(x, gain, w, wbat, *, tm, tn):
    t, d = x.shape
    return pl.pallas_call(
        functools.partial(_inproj_body, tn=tn),
        grid=(t // tm,),
        in_specs=[
            pl.BlockSpec((tm, d), lambda i: (i, 0)),
            _resident((1, d)),
            _resident((d, N_PROJ)),
            _resident((8, d)),
        ],
        out_specs=[
            pl.BlockSpec((tm, N_PROJ), lambda i: (i, 0)),
            pl.BlockSpec((tm, LANES), lambda i: (i, 0)),
            pl.BlockSpec((8, tm), lambda i: (0, i)),
        ],
        out_shape=[jax.ShapeDtypeStruct((t, N_PROJ), MXU_DTYPE), jax.ShapeDtypeStruct((t, LANES), F32),
                   jax.ShapeDtypeStruct((8, t), F32)],
        compiler_params=_params(("parallel",)),
        name="inproj",
    )(x, gain, w, wbat)


def _mla_rot(x, lane):
    return jnp.where(lane < MLA_NOPE + MLA_ROPE // 2, -pltpu.roll(x, LANES - MLA_ROPE // 2, 1),
                     pltpu.roll(x, MLA_ROPE // 2, 1))


def _lane_tile(x, n):
    return x if n == 1 else jnp.concatenate([x] * n, axis=1)


def _value_rows(v_ref, h, vt_h):
    n = vt_h.shape[1]
    v_ref[0, h, 0:MLA_V, :] = vt_h.astype(v_ref.dtype)
    pad_rows = lax.broadcasted_iota(jnp.int32, (V_ROWS - MLA_V, n), 0)
    v_ref[0, h, MLA_V:V_ROWS, :] = jnp.where(pad_rows == 0, 1.0, 0.0).astype(v_ref.dtype)


def _mla_prep_body(lat_ref, tail_ref, cqg_ref, ckvg_ref, wqt_ref, wk_ref, wvt_ref, qgt_ref, kg_ref, cost_ref,
                   sint_ref, cos_ref, sin_ref, q_ref, k_ref, v_ref, *, scale):
    lat = lat_ref[...].astype(F32)
    tm = lat.shape[0]
    cqn = _rms(lat[:, :MLA_Q_RANK], cqg_ref[...], MLA_Q_RANK).astype(MXU_DTYPE)
    ckvn = _rms(lat[:, MLA_Q_RANK:MLA_Q_RANK + MLA_KV_RANK], ckvg_ref[...], MLA_KV_RANK).astype(MXU_DTYPE)

    qt_all = _dot_nt(wqt_ref[...], cqn)
    qgt = _lane_tile(qgt_ref[...], tm // LANES)
    cost = cost_ref[...]
    sint = sint_ref[...]
    half = MLA_ROPE // 2
    for h in range(MLA_HEADS):
        x = qt_all[h * LANES:(h + 1) * LANES, :]
        ms = jnp.sum(x * x, axis=0, keepdims=True) * (1.0 / MLA_QK)
        qn = x * lax.rsqrt(ms + EPS) * qgt
        rot = jnp.concatenate([jnp.zeros((MLA_NOPE, tm), F32), -qn[MLA_NOPE + half:MLA_QK],
                               qn[MLA_NOPE:MLA_NOPE + half], jnp.zeros((LANES - MLA_QK, tm), F32)], axis=0)
        q_ref[0, h] = ((qn * cost + rot * sint) * scale).astype(q_ref.dtype)

    lane = lax.broadcasted_iota(jnp.int32, (tm, LANES), 1)
    kr = jnp.where((lane >= MLA_NOPE) & (lane < MLA_QK), tail_ref[...], 0.0)
    kg = kg_ref[...]
    cos = cos_ref[...]
    rot_kr_sin = _mla_rot(kr * kg, lane) * sin_ref[...]
    k_all = _dot(ckvn, wk_ref[...])
    for h in range(MLA_HEADS):
        kh = k_all[:, h * LANES:(h + 1) * LANES] + kr
        r = lax.rsqrt(jnp.sum(kh * kh, axis=-1, keepdims=True) * (1.0 / MLA_QK) + EPS)
        k_ref[0, h] = (r * (kh * kg * cos + rot_kr_sin)).astype(k_ref.dtype)

    vt = _dot_nt(wvt_ref[...], ckvn)
    for h in range(MLA_HEADS):
        _value_rows(v_ref, h, vt[h * MLA_V:(h + 1) * MLA_V, :])


def _mla_prep(proj, tail, cqg, ckvg, wqt, wk, wvt, qgt, kg, cost, sint, cos, sin, *, b, s, tm):
    nt = s // tm
    scale = (MLA_QK ** -0.5) * LOG2E
    cblk = COL_MLA // 512
    full = lambda shape: pl.BlockSpec(shape, lambda bi, i: (0,) * len(shape))
    return pl.pallas_call(
        functools.partial(_mla_prep_body, scale=scale),
        grid=(b, nt),
        in_specs=[
            pl.BlockSpec((tm, 512), lambda bi, i: (bi * nt + i, cblk)),
            pl.BlockSpec((tm, LANES), lambda bi, i: (bi * nt + i, 0)),
            full((1, MLA_Q_RANK)), full((1, MLA_KV_RANK)),
            full((MLA_HEADS * LANES, MLA_Q_RANK)), full((MLA_KV_RANK, MLA_HEADS * LANES)),
            full((MIX_WIDTH, MLA_KV_RANK)), full((LANES, LANES)), full((1, LANES)),
            pl.BlockSpec((LANES, tm), lambda bi, i: (0, i)),
            pl.BlockSpec((LANES, tm), lambda bi, i: (0, i)),
            pl.BlockSpec((tm, LANES), lambda bi, i: (i, 0)),
            pl.BlockSpec((tm, LANES), lambda bi, i: (i, 0)),
        ],
        out_specs=[
            pl.BlockSpec((1, MLA_HEADS, LANES, tm), lambda bi, i: (bi, 0, 0, i)),
            pl.BlockSpec((1, MLA_HEADS, tm, LANES), lambda bi, i: (bi, 0, i, 0)),
            pl.BlockSpec((1, MLA_HEADS, V_ROWS, tm), lambda bi, i: (bi, 0, 0, i)),
        ],
        out_shape=[
            jax.ShapeDtypeStruct((b, MLA_HEADS, LANES, s), MXU_DTYPE),
            jax.ShapeDtypeStruct((b, MLA_HEADS, s, LANES), MXU_DTYPE),
            jax.ShapeDtypeStruct((b, MLA_HEADS, V_ROWS, s), MXU_DTYPE),
        ],
        compiler_params=_params(("parallel", "parallel")),
        name="mla_prep",
    )(proj, tail, cqg, ckvg, wqt, wk, wvt, qgt, kg, cost, sint, cos, sin)


def _moba_rot(x, lane):
    half = MOBA_DH // 2
    return jnp.where(lane % MOBA_DH < half, -pltpu.roll(x, LANES - half, 1), pltpu.roll(x, half, 1))


def _pair_rms(x, gain, lo_mask):
    sq = x * x
    ss_lo = jnp.sum(jnp.where(lo_mask, sq, 0.0), axis=-1, keepdims=True)
    ss_hi = jnp.sum(jnp.where(lo_mask, 0.0, sq), axis=-1, keepdims=True)
    r = jnp.where(lo_mask, lax.rsqrt(ss_lo * (1.0 / MOBA_DH) + EPS), lax.rsqrt(ss_hi * (1.0 / MOBA_DH) + EPS))
    return x * r * gain


def _select_bias(gate, blk, blk_f, j):
    g = jnp.where(blk < j, gate, SENTINEL)
    allowed = blk == j
    for _ in range(MOBA_TOPK):
        m = jnp.max(g, axis=0, keepdims=True)
        first = jnp.min(jnp.where(g == m, blk_f, 1e9), axis=0, keepdims=True)
        pick = (blk_f == first) & (m > SENTINEL)
        allowed = allowed | pick
        g = jnp.where(pick, SENTINEL, g)
    return jnp.where(allowed, 0.0, NEG_INF)


def _moba_prep_body(qkv_ref, qgt_ref, kg_ref, cost_ref, sint_ref, cos_ref, sin_ref, q_ref, k_ref, v_ref, kmt_scr,
                    *, scale):
    j = pl.program_id(1)

    @pl.when(j == 0)
    def _():
        kmt_scr[...] = jnp.zeros_like(kmt_scr)

    tq = qkv_ref.shape[0]
    hw = MOBA_HEADS * MOBA_DH
    dh, half, nblk = MOBA_DH, MOBA_DH // 2, MOBA_MAX_BLOCKS

    vt = qkv_ref[:, 2 * hw:].astype(F32).T
    for h in range(MOBA_HEADS):
        _value_rows(v_ref, h, vt[h * dh:(h + 1) * dh, :])

    qt_all = qkv_ref[:, :hw].astype(F32).T
    qgt = _lane_tile(qgt_ref[...], tq // LANES)
    cost = cost_ref[...]
    sint = sint_ref[...]
    blk = lax.broadcasted_iota(jnp.int32, (nblk, tq), 0)
    blk_f = blk.astype(F32)
    zeros = jnp.zeros((nblk, tq), F32)

    lane = lax.broadcasted_iota(jnp.int32, (tq, LANES), 1)
    lo_mask = lane < dh
    cos = cos_ref[...]
    sin = sin_ref[...]
    row_lane = lax.broadcasted_iota(jnp.int32, (1, LANES), 1)
    slot = lax.broadcasted_iota(jnp.int32, (LANES, LANES), 0)
    for p in range(MOBA_HEADS // 2):
        x = qt_all[p * LANES:(p + 1) * LANES, :]
        sq = x * x
        r_e = lax.rsqrt(jnp.sum(sq[:dh], axis=0, keepdims=True) * (1.0 / dh) + EPS)
        r_o = lax.rsqrt(jnp.sum(sq[dh:], axis=0, keepdims=True) * (1.0 / dh) + EPS)
        qn = jnp.concatenate([x[:dh] * r_e, x[dh:] * r_o], axis=0) * qgt
        rot = jnp.concatenate([-qn[half:dh], qn[:half], -qn[dh + half:], qn[dh:dh + half]], axis=0)
        q = qn * cost + rot * sint
        gate = _dot(kmt_scr[p], q, precision=HIGHEST)
        bias_e = _select_bias(gate[dh:dh + nblk], blk, blk_f, j)
        bias_o = _select_bias(gate[:nblk], blk, blk_f, j)
        qs = q * scale
        q_ref[0, 2 * p] = jnp.concatenate([qs[:dh], bias_e, zeros], axis=0).astype(q_ref.dtype)
        q_ref[0, 2 * p + 1] = jnp.concatenate([bias_o, zeros, qs[dh:]], axis=0).astype(q_ref.dtype)

        kn = _pair_rms(qkv_ref[:, hw + p * LANES:hw + (p + 1) * LANES].astype(F32), kg_ref[...], lo_mask)
        k = kn * cos + _moba_rot(kn, lane) * sin
        k_ref[0, 2 * p] = jnp.where(lo_mask, k, jnp.where(lane == dh + j, 1.0, 0.0)).astype(k_ref.dtype)
        k_ref[0, 2 * p + 1] = jnp.where(lo_mask, jnp.where(lane == j, 1.0, 0.0), k).astype(k_ref.dtype)
        km = jnp.sum(k, axis=0, keepdims=True) * (1.0 / MOBA_BLOCK)
        new_rows = jnp.where(slot == dh + j, jnp.where(row_lane < dh, km, 0.0),
                             jnp.where(row_lane < dh, 0.0, km))
        kmt_scr[p] = jnp.where((slot == dh + j) | (slot == j), new_rows, kmt_scr[p])


def _moba_prep(proj, qgt, kg, cost, sint, cos, sin, *, b, s):
    tq = MOBA_BLOCK
    nt = s // tq
    scale = (MOBA_DH ** -0.5) * LOG2E
    cblk = COL_MOBA // (3 * MOBA_HEADS * MOBA_DH)
    return pl.pallas_call(
        functools.partial(_moba_prep_body, scale=scale),
        grid=(b, nt),
        in_specs=[
            pl.BlockSpec((tq, 3 * MOBA_HEADS * MOBA_DH), lambda bi, j: (bi * nt + j, cblk)),
            pl.BlockSpec((LANES, LANES), lambda bi, j: (0, 0)),
            pl.BlockSpec((1, LANES), lambda bi, j: (0, 0)),
            pl.BlockSpec((LANES, tq), lambda bi, j: (0, j)),
            pl.BlockSpec((LANES, tq), lambda bi, j: (0, j)),
            pl.BlockSpec((tq, LANES), lambda bi, j: (j, 0)),
            pl.BlockSpec((tq, LANES), lambda bi, j: (j, 0)),
        ],
        out_specs=[
            pl.BlockSpec((1, MOBA_HEADS, LANES, tq), lambda bi, j: (bi, 0, 0, j)),
            pl.BlockSpec((1, MOBA_HEADS, tq, LANES), lambda bi, j: (bi, 0, j, 0)),
            pl.BlockSpec((1, MOBA_HEADS, V_ROWS, tq), lambda bi, j: (bi, 0, 0, j)),
        ],
        out_shape=[
            jax.ShapeDtypeStruct((b, MOBA_HEADS, LANES, s), MXU_DTYPE),
            jax.ShapeDtypeStruct((b, MOBA_HEADS, s, LANES), MXU_DTYPE),
            jax.ShapeDtypeStruct((b, MOBA_HEADS, V_ROWS, s), MXU_DTYPE),
        ],
        scratch_shapes=[pltpu.VMEM((MOBA_HEADS // 2, LANES, LANES), F32)],
        compiler_params=_params(("parallel", "arbitrary")),
        name="moba_prep",
    )(proj, qgt, kg, cost, sint, cos, sin)


def _flash_body(qt_ref, k_ref, vt_ref, o_ref, buf0, buf1, *, tq):
    i = pl.program_id(2)
    qts = (qt_ref[0, 0], qt_ref[0, 1])

    def scores(t, buf, diagonal):
        ks = pl.ds(pl.multiple_of(t * tq, tq), tq)
        out = []
        for hh in range(2):
            s = _dot(k_ref[0, hh, ks, :], qts[hh])
            if diagonal:
                key = lax.broadcasted_iota(jnp.int32, s.shape, 0)
                qry = lax.broadcasted_iota(jnp.int32, s.shape, 1)
                s = jnp.where(key <= qry, s, NEG_INF)
            buf[hh] = s
            out.append(jnp.max(s, axis=0, keepdims=True))
        return tuple(out)

    def consume(t, buf, smax, stats):
        ks = pl.ds(pl.multiple_of(t * tq, tq), tq)
        out = []
        for hh in range(2):
            m, acc = stats[2 * hh:2 * hh + 2]
            m_new = jnp.maximum(m, smax[hh])
            p = jnp.exp2(buf[hh] - m_new)
            acc = jnp.exp2(m - m_new) * acc + _dot(vt_ref[0, hh, :, ks], p.astype(MXU_DTYPE))
            out += [m_new, acc]
        return tuple(out)

    def finish(stats):
        o_t = jnp.concatenate([stats[1][:MLA_V] / stats[1][MLA_V:MLA_V + 1],
                               stats[3][:MLA_V] / stats[3][MLA_V:MLA_V + 1]], axis=0)
        o_ref[0] = o_t.T.astype(o_ref.dtype)

    init = (jnp.full((1, tq), NEG_INF, F32), jnp.zeros((V_ROWS, tq), F32)) * 2

    @pl.when(i % 2 == 0)
    def _():
        def pair(u, carry):
            smax, stats = carry
            t0 = 2 * u
            smax1 = scores(t0, buf1, False)
            stats = consume(jnp.where(u == 0, i, t0 - 1), buf0, smax, stats)
            smax0 = scores(t0 + 1, buf0, False)
            return smax0, consume(t0, buf1, smax1, stats)

        smax, stats = lax.fori_loop(0, i // 2, pair, (scores(i, buf0, True), init))
        finish(consume(jnp.where(i == 0, 0, i - 1), buf0, smax, stats))

    @pl.when(i % 2 == 1)
    def _():
        smax0 = scores(i, buf0, True)
        smax1 = scores(0, buf1, False)
        stats = consume(i, buf0, smax0, init)

        def pair(u, carry):
            smax, stats = carry
            t0 = 2 * u + 1
            smax0 = scores(t0, buf0, False)
            stats = consume(t0 - 1, buf1, smax, stats)
            smax1 = scores(t0 + 1, buf1, False)
            return smax1, consume(t0, buf0, smax0, stats)

        smax, stats = lax.fori_loop(0, i // 2, pair, (smax1, stats))
        finish(consume(i - 1, buf1, smax, stats))


def _flash(qt, k, vt, *, tq):
    b, h, s, _ = k.shape
    return pl.pallas_call(
        functools.partial(_flash_body, tq=tq),
        grid=(b, h // 2, s // tq),
        in_specs=[
            pl.BlockSpec((1, 2, LANES, tq), lambda bi, p, i: (bi, p, 0, i)),
            pl.BlockSpec((1, 2, s, LANES), lambda bi, p, i: (bi, p, 0, 0)),
            pl.BlockSpec((1, 2, V_ROWS, s), lambda bi, p, i: (bi, p, 0, 0)),
        ],
        out_specs=pl.BlockSpec((1, tq, LANES), lambda bi, p, i: (bi, i, p)),
        out_shape=jax.ShapeDtypeStruct((b, s, MIX_WIDTH), MXU_DTYPE),
        scratch_shapes=[pltpu.VMEM((2, tq, tq), F32), pltpu.VMEM((2, tq, tq), F32)],
        compiler_params=_params(("parallel", "parallel", "arbitrary")),
        name="flash",
    )(qt, k, vt)


def _hi_lo(x):
    hi = x.astype(MXU_DTYPE)
    return hi, (x - hi.astype(F32)).astype(MXU_DTYPE)


def _unit_lower_solve(a_list, rhs_list, eye):
    c = eye.shape[0]
    n = range(len(a_list))
    t = [eye - a for a in a_list]
    pk = []
    for a in a_list:
        pwm = (-a).astype(MXU_DTYPE)
        pk.append(_dot(pwm, pwm))
    n_sq = int(np.log2(c)) - 1
    for it in range(n_sq):
        if it < n_sq - 1:
            both = [_dot(jnp.concatenate([t[j], pk[j]], axis=0).astype(MXU_DTYPE), pk[j].astype(MXU_DTYPE))
                    for j in n]
            t = [t[j] + both[j][:c] for j in n]
            pk = [both[j][c:] for j in n]
        else:
            t = [t[j] + _dot(t[j].astype(MXU_DTYPE), pk[j].astype(MXU_DTYPE)) for j in n]
    tm = [x.astype(MXU_DTYPE) for x in t]
    x1 = [_dot(tm[j], rhs_list[j].astype(MXU_DTYPE)) for j in n]
    a_split = [_hi_lo(a) for a in a_list]
    x_split = [_hi_lo(x) for x in x1]
    ax = [_dot(jnp.concatenate(a_split[j], axis=0), x_split[j][0]) for j in n]
    ax_lo = [_dot(a_split[j][0], x_split[j][1]) for j in n]
    res = [rhs_list[j] - x1[j] - (ax[j][:c] + ax[j][c:] + ax_lo[j]) for j in n]
    return [x1[j] + _dot(tm[j], res[j].astype(MXU_DTYPE)) for j in n]


def _gdn_prep_body(qkv_ref, halo_ref, ba_ref, bat_ref, conv_ref, alog_ref, dtb_ref, alogt_ref, dtbt_ref,
                   u_ref, w_ref, qd_ref, kd_ref, qk_ref, egl_ref, xe_scr, y_scr, bg_scr,
                   *, tiles_per_seq, nc):
    i = pl.program_id(0)
    tm = qkv_ref.shape[0]
    c = GDN_CHUNK
    hk = GDN_HEADS * GDN_DK

    xe_scr[pl.ds(0, GDN_HALO), :] = jnp.where(i % tiles_per_seq == 0, 0.0, halo_ref[...].astype(F32))
    xe_scr[pl.ds(GDN_HALO, tm), :] = qkv_ref[...].astype(F32)
    cw = conv_ref[...]
    y = xe_scr[pl.ds(GDN_HALO, tm), :] * cw[GDN_CONV - 1:GDN_CONV]
    for back in range(1, GDN_CONV):
        y = y + xe_scr[pl.ds(GDN_HALO - back, tm), :] * cw[GDN_CONV - 1 - back:GDN_CONV - back]
    y_scr[...] = _silu(y)

    ba = ba_ref[...]
    lane = lax.broadcasted_iota(jnp.int32, ba.shape, 1)
    g_col = -jnp.exp(alog_ref[...]) * _softplus(ba + dtb_ref[...])
    bg_scr[...] = jnp.where(lane < GDN_HEADS, jax.nn.sigmoid(ba), g_col)

    ri = lax.broadcasted_iota(jnp.int32, (c, c), 0)
    ci = lax.broadcasted_iota(jnp.int32, (c, c), 1)
    tril = ci <= ri
    strict = ci < ri
    eye = jnp.where(ci == ri, 1.0, 0.0)
    lower_ones = jnp.where(tril, 1.0, 0.0)
    upper_ones = jnp.where(ci >= ri, 1.0, 0.0)

    def chunk_group(grp, _):
        chs = [grp * GDN_GROUP + j for j in range(GDN_GROUP)]
        rows = [pl.ds(pl.multiple_of(ch * c, c), c) for ch in chs]
        bg = [bg_scr[r, :] for r in rows]
        gcum_col = [_dot(lower_ones, x, precision=HIGHEST) for x in bg]
        g_row = [-jnp.exp(alogt_ref[...]) * _softplus(bat_ref[ch] + dtbt_ref[...]) for ch in chs]
        gcum_row = [_dot(x, upper_ones, precision=HIGHEST) for x in g_row]
        probs = [(j, h) for j in range(GDN_GROUP) for h in range(GDN_HEADS)]
        n = range(len(probs))
        sls = [slice(h * GDN_DK, (h + 1) * GDN_DK) for _, h in probs]
        qh = [y_scr[rows[j], sls[i]] for i, (j, h) in enumerate(probs)]
        kh = [y_scr[rows[j], pl.ds(hk + h * GDN_DK, GDN_DK)] for j, h in probs]
        vh = [y_scr[rows[j], pl.ds(2 * hk + h * GDN_DV, GDN_DV)] for j, h in probs]
        qn = [x * lax.rsqrt(jnp.sum(x * x, axis=-1, keepdims=True) + EPS) * (GDN_DK ** -0.5) for x in qh]
        kn = [x * lax.rsqrt(jnp.sum(x * x, axis=-1, keepdims=True) + EPS) for x in kh]
        beta = [bg[j][:, h:h + 1] for j, h in probs]
        gc = [gcum_col[j][:, GDN_HEADS + h:GDN_HEADS + h + 1] for j, h in probs]
        gr = [gcum_row[j][GDN_HEADS + h:GDN_HEADS + h + 1, :] for j, h in probs]
        kb = [kn[i] * beta[i] for i in n]
        knm = [x.astype(MXU_DTYPE) for x in kn]
        dec = [jnp.exp(jnp.where(tril, gc[i] - gr[i], NEG_INF)) for i in n]
        skk = [_dot_nt(kb[i].astype(MXU_DTYPE), knm[i]) for i in n]
        sqk = [_dot_nt(qn[i].astype(MXU_DTYPE), knm[i]) for i in n]
        a = [jnp.where(strict, skk[i] * dec[i], 0.0) for i in n]
        eg = [jnp.exp(x) for x in gc]
        rhs = [jnp.concatenate([vh[i] * beta[i], kb[i] * eg[i]], axis=1) for i in n]
        sol = _unit_lower_solve(a, rhs, eye)
        for i, (j, h) in enumerate(probs):
            sl, r = sls[i], rows[j]
            gl = gc[i][c - 1:c, :]
            u_ref[r, sl] = sol[i][:, :GDN_DV]
            w_ref[r, sl] = sol[i][:, GDN_DV:].astype(w_ref.dtype)
            qd_ref[r, sl] = (qn[i] * eg[i]).astype(qd_ref.dtype)
            kd_ref[r, sl] = (kn[i] * jnp.exp(gl - gc[i])).astype(kd_ref.dtype)
            qk_ref[r, pl.ds(h * LANES, c)] = (sqk[i] * dec[i]).astype(qk_ref.dtype)
            qk_ref[r, pl.ds(h * LANES + c, LANES - c)] = jnp.zeros((c, LANES - c), qk_ref.dtype)
            egl_ref[chs[j], :, sl] = jnp.broadcast_to(jnp.exp(gl), (1, GDN_DV))
        return 0

    lax.fori_loop(0, nc // GDN_GROUP, chunk_group, 0)


def _gdn_prep(proj, tail, bat, conv, alog, dtb, alogt, dtbt, *, s, tm):
    t = proj.shape[0]
    nc = tm // GDN_CHUNK
    hk = GDN_HEADS * GDN_DK
    width = 3 * hk
    cblk = COL_GDN // width
    full = lambda shape: pl.BlockSpec(shape, lambda i: (0,) * len(shape))
    row = lambda w: pl.BlockSpec((tm, w), lambda i: (i, 0))
    return pl.pallas_call(
        functools.partial(_gdn_prep_body, tiles_per_seq=s // tm, nc=nc),
        grid=(t // tm,),
        in_specs=[
            pl.BlockSpec((tm, width), lambda i: (i, cblk)),
            pl.BlockSpec((GDN_HALO, width), lambda i: (jnp.maximum(i * (tm // GDN_HALO) - 1, 0), cblk)),
            pl.BlockSpec((tm, LANES), lambda i: (i, 0)),
            pl.BlockSpec((nc, 8, GDN_CHUNK), lambda i: (i, 0, 0)),
            full((GDN_CONV, width)), full((1, LANES)), full((1, LANES)), full((8, 1)), full((8, 1)),
        ],
        out_specs=[row(hk), row(hk), row(hk), row(hk), row(GDN_HEADS * LANES),
                   pl.BlockSpec((nc, 1, hk), lambda i: (i, 0, 0))],
        out_shape=[
            jax.ShapeDtypeStruct((t, hk), F32),
            jax.ShapeDtypeStruct((t, hk), MXU_DTYPE),
            jax.ShapeDtypeStruct((t, hk), MXU_DTYPE),
            jax.ShapeDtypeStruct((t, hk), MXU_DTYPE),
            jax.ShapeDtypeStruct((t, GDN_HEADS * LANES), MXU_DTYPE),
            jax.ShapeDtypeStruct((t // GDN_CHUNK, 1, hk), F32),
        ],
        scratch_shapes=[
            pltpu.VMEM((tm + GDN_HALO, width), F32),
            pltpu.VMEM((tm, width), F32),
            pltpu.VMEM((tm, LANES), F32),
        ],
        compiler_params=_params(("parallel",)),
        name="gdn_prep",
    )(proj, proj, tail, bat, conv, alog, dtb, alogt, dtbt)


def _gdn_scan_body(u_ref, w_ref, qd_ref, kd_ref, qk_ref, egl_ref, z_ref, og_ref, o_ref, s_scr, *, nb, nc):
    @pl.when(pl.program_id(0) == 0)
    def _():
        s_scr[...] = jnp.zeros_like(s_scr)

    c = GDN_CHUNK
    og = og_ref[...]

    def chunk(ch, _):
        rows = pl.ds(pl.multiple_of(ch * c, c), c)
        probs = [(bi, h) for bi in range(nb) for h in range(GDN_HEADS)]
        n = range(len(probs))
        sls = [slice(h * GDN_DV, (h + 1) * GDN_DV) for _, h in probs]
        st = [s_scr[bi * GDN_HEADS + h] for bi, h in probs]
        stm = [x.astype(MXU_DTYPE) for x in st]
        wq = [jnp.concatenate([w_ref[bi, rows, sls[i]], qd_ref[bi, rows, sls[i]]], axis=0)
              for i, (bi, h) in enumerate(probs)]
        r1 = [_dot(wq[i], stm[i]) for i in n]
        vm = [(u_ref[bi, rows, sls[i]] - r1[i][:c]).astype(MXU_DTYPE) for i, (bi, h) in enumerate(probs)]
        o2 = [_dot(qk_ref[bi, rows, pl.ds(h * LANES, c)], vm[i]) for i, (bi, h) in enumerate(probs)]
        sd = [_dot_tn(kd_ref[bi, rows, sls[i]], vm[i]) for i, (bi, h) in enumerate(probs)]
        for i, (bi, h) in enumerate(probs):
            s_scr[bi * GDN_HEADS + h] = st[i] * egl_ref[bi, ch, :, sls[i]] + sd[i]
            o = r1[i][c:] + o2[i]
            o_ref[bi, rows, sls[i]] = (_rms(o, og, GDN_DV)
                                       * _silu(z_ref[bi, rows, sls[i]].astype(F32))).astype(o_ref.dtype)
        return 0

    lax.fori_loop(0, nc, chunk, 0)


def _gdn_scan(u, w, qd, kd, qk, egl, proj3, og, *, tt):
    b, s, hk = u.shape
    nc = tt // GDN_CHUNK
    blk = lambda width: pl.BlockSpec((b, tt, width), lambda i: (0, i, 0))
    return pl.pallas_call(
        functools.partial(_gdn_scan_body, nb=b, nc=nc),
        grid=(s // tt,),
        in_specs=[
            blk(hk), blk(hk), blk(hk), blk(hk), blk(GDN_HEADS * LANES),
            pl.BlockSpec((b, nc, 1, hk), lambda i: (0, i, 0, 0)),
            pl.BlockSpec((b, tt, hk), lambda i: (0, i, COL_Z // hk)),
            pl.BlockSpec((1, GDN_DV), lambda i: (0, 0)),
        ],
        out_specs=blk(hk),
        out_shape=jax.ShapeDtypeStruct((b, s, hk), MXU_DTYPE),
        scratch_shapes=[pltpu.VMEM((b * GDN_HEADS, GDN_DK, GDN_DV), F32)],
        compiler_params=_params(("arbitrary",)),
        name="gdn_scan",
    )(u, w, qd, kd, qk, egl, proj3, og)


def _merge_body(x_ref, om_ref, og_ref, ob_ref, gl_ref, wb_ref, wo_ref, o_ref):
    mixed = None
    for n, br in enumerate((om_ref, og_ref, ob_ref)):
        up = _dot(br[...], wb_ref[n])
        term = jax.nn.sigmoid(gl_ref[:, n * D_MODEL:(n + 1) * D_MODEL].astype(F32)) * up
        mixed = term if mixed is None else mixed + term
    o_ref[...] = x_ref[...] + _dot(mixed.astype(MXU_DTYPE), wo_ref[...])


def _merge(x, o_mla, o_gdn, o_moba, proj, wb, wo, *, tm):
    t, d = x.shape
    row = lambda w: pl.BlockSpec((tm, w), lambda i: (i, 0))
    return pl.pallas_call(
        _merge_body,
        grid=(t // tm,),
        in_specs=[
            row(d), row(MIX_WIDTH), row(MIX_WIDTH), row(MIX_WIDTH), row(3 * d),
            _resident((3, MIX_WIDTH, d)),
            _resident((d, d)),
        ],
        out_specs=row(d),
        out_shape=jax.ShapeDtypeStruct((t, d), F32),
        compiler_params=_params(("parallel",)),
        name="merge",
    )(x, o_mla, o_gdn, o_moba, proj, wb, wo)


def _rope_tables(s):
    pos = jnp.arange(s, dtype=jnp.int32).astype(F32)[:, None]

    def cs(d):
        half = d // 2
        inv_freq = ROPE_THETA ** (-jnp.arange(half, dtype=F32) * 2.0 / d)
        ang = pos * inv_freq[None, :]
        return jnp.cos(ang), jnp.sin(ang)

    c, sn = cs(MLA_ROPE)
    pad = LANES - MLA_QK
    mla_cos = jnp.concatenate([jnp.ones((s, MLA_NOPE), F32), c, c, jnp.zeros((s, pad), F32)], axis=1)
    mla_sin = jnp.concatenate([jnp.zeros((s, MLA_NOPE), F32), sn, sn, jnp.zeros((s, pad), F32)], axis=1)
    c, sn = cs(MOBA_DH)
    moba_cos, moba_sin = jnp.tile(c, (1, 4)), jnp.tile(sn, (1, 4))
    return mla_cos, mla_sin, mla_cos.T, mla_sin.T, moba_cos, moba_sin, moba_cos.T, moba_sin.T


def _proj_weights(w_in):
    d = w_in.shape[0]
    o = np.cumsum((0, MLA_Q_RANK, MLA_KV_RANK, MLA_ROPE, 512, 512, 512, GDN_HEADS, GDN_HEADS, 512, 1536, 3 * D_MODEL))
    c_q, c_kv, k_rope = w_in[:, o[0]:o[1]], w_in[:, o[1]:o[2]], w_in[:, o[2]:o[3]]
    gdn_qkv = w_in[:, o[3]:o[6]]
    ba = w_in[:, o[6]:o[8]]
    z = w_in[:, o[8]:o[9]]
    moba = w_in[:, o[9]:o[10]]
    gate = w_in[:, o[10]:o[11]]
    zeros = lambda n: jnp.zeros((d, n), w_in.dtype)
    w = jnp.concatenate([gate, gdn_qkv, moba, z, c_q, c_kv, ba, zeros(MLA_NOPE - 2 * GDN_HEADS), k_rope,
                         zeros(LANES - MLA_QK)], axis=1)
    assert w.shape[1] == N_PROJ
    return w.astype(MXU_DTYPE), ba.T.astype(MXU_DTYPE)


def _lane_pad(v, n=LANES):
    return jnp.pad(v, (0, n - v.shape[0]))[None, :]


def kernel(x, ffa_norm, ffa_w_in, ffa_w_out, mix_norm, w_in, mla_cq_norm, mla_ckv_norm, mla_w_uq, mla_w_ukv, mla_q_norm, mla_k_norm, gdn_conv, gdn_a_log, gdn_dt_bias, gdn_out_norm, moba_q_norm, moba_k_norm, w_branch, w_out, ffb_norm, ffb_w_in, ffb_w_out):
    b, s, d = x.shape
    t = b * s
    assert d == D_MODEL and s % 512 == 0 and s // MOBA_BLOCK <= MOBA_MAX_BLOCKS
    depth = ffa_norm.shape[0]
    tm = 512
    mla_cos, mla_sin, mla_cost, mla_sint, moba_cos, moba_sin, moba_cost, moba_sint = _rope_tables(s)
    lane_bcast = lambda v: jnp.broadcast_to(v[:, None], (LANES, LANES))
    x = x.reshape(t, d)
    for l in range(depth):
        x = _ffn(x, ffa_norm[l][None, :], ffa_w_in[l].astype(MXU_DTYPE), ffa_w_out[l].astype(MXU_DTYPE),
                 tm=tm, tf=256)

        w_proj, w_bat = _proj_weights(w_in[l])
        proj, tail, bat = _inproj(x, mix_norm[l][None, :], w_proj, w_bat, tm=tm, tn=512)

        wq = jnp.pad(mla_w_uq[l].reshape(MLA_Q_RANK, MLA_HEADS, MLA_QK),
                     ((0, 0), (0, 0), (0, LANES - MLA_QK))).reshape(MLA_Q_RANK, MLA_HEADS * LANES)
        wkv = mla_w_ukv[l].reshape(MLA_KV_RANK, MLA_HEADS, MLA_NOPE + MLA_V)
        wk = jnp.pad(wkv[:, :, :MLA_NOPE], ((0, 0), (0, 0), (0, LANES - MLA_NOPE))).reshape(MLA_KV_RANK, MLA_HEADS * LANES)
        wv = wkv[:, :, MLA_NOPE:].reshape(MLA_KV_RANK, MLA_HEADS * MLA_V).T
        q, k, v = _mla_prep(proj, tail, mla_cq_norm[l][None, :], mla_ckv_norm[l][None, :], wq.T.astype(MXU_DTYPE),
                            wk.astype(MXU_DTYPE), wv.astype(MXU_DTYPE), lane_bcast(_lane_pad(mla_q_norm[l])[0]),
                            _lane_pad(mla_k_norm[l]), mla_cost, mla_sint, mla_cos, mla_sin, b=b, s=s, tm=tm)
        o_mla = _flash(q, k, v, tq=512)

        bat_chunks = bat.reshape(8, t // GDN_CHUNK, GDN_CHUNK).transpose(1, 0, 2)
        head_pad = lambda v: jnp.pad(v, (GDN_HEADS, LANES - 2 * GDN_HEADS))[None, :]
        head_col = lambda v: jnp.pad(v, (GDN_HEADS, 0))[:, None]
        u, w, qd, kd, qk, egl = _gdn_prep(proj, tail, bat_chunks, gdn_conv[l], head_pad(gdn_a_log[l]),
                                          head_pad(gdn_dt_bias[l]), head_col(gdn_a_log[l]),
                                          head_col(gdn_dt_bias[l]), s=s, tm=tm)
        r3 = lambda a: a.reshape(b, s, a.shape[-1])
        o_gdn = _gdn_scan(r3(u), r3(w), r3(qd), r3(kd), r3(qk), egl.reshape(b, s // GDN_CHUNK, 1, -1),
                          r3(proj), gdn_out_norm[l][None, :], tt=tm)

        q, k, v = _moba_prep(proj, lane_bcast(jnp.tile(moba_q_norm[l], 2)), jnp.tile(moba_k_norm[l], 2)[None, :],
                             moba_cost, moba_sint, moba_cos, moba_sin, b=b, s=s)
        o_moba = _flash(q, k, v, tq=512)

        x = _merge(x, o_mla.reshape(t, -1), o_gdn.reshape(t, -1), o_moba.reshape(t, -1), proj,
                   w_branch[l].astype(MXU_DTYPE), w_out[l].astype(MXU_DTYPE), tm=tm)

        x = _ffn(x, ffb_norm[l][None, :], ffb_w_in[l].astype(MXU_DTYPE), ffb_w_out[l].astype(MXU_DTYPE),
                 tm=tm, tf=256)
    return x.reshape(b, s, d)
```

```python
import functools

import numpy as np
import jax
import jax.numpy as jnp
from jax import lax
from jax.experimental import pallas as pl
from jax.experimental.pallas import tpu as pltpu

F32 = jnp.float32
BF16 = jnp.bfloat16
MXU_DTYPE = BF16
HIGHEST = lax.Precision.HIGHEST

EPS = 1e-6
ROPE_THETA = 10000.0
NEG_INF = -1e30
SENTINEL = -3e38
LOG2E = 1.4426950408889634

D_MODEL = 1024
D_FF = 2816
MIX_WIDTH = 512
LANES = 128

MLA_HEADS = 8
MLA_Q_RANK = 256
MLA_KV_RANK = 128
MLA_NOPE = 64
MLA_ROPE = 32
MLA_V = 64
MLA_QK = MLA_NOPE + MLA_ROPE

GDN_HEADS = 4
GDN_DK = 128
GDN_DV = 128
GDN_CONV = 4
GDN_CHUNK = 64
GDN_HALO = 16
GDN_GROUP = 4

MOBA_HEADS = 8
MOBA_DH = 64
MOBA_BLOCK = 256
MOBA_TOPK = 3
MOBA_MAX_BLOCKS = 32
V_ROWS = 80

COL_GATE = 0
COL_GDN = 3072
COL_MOBA = 4608
COL_Z = 6144
COL_MLA = 6656
N_PROJ = 7168

VMEM_LIMIT = 48 * 1024 * 1024


def _dot(a, b, precision=None):
    return jnp.dot(a, b, preferred_element_type=F32, precision=precision)


def _dot_nt(a, b, precision=None):
    return lax.dot_general(a, b, (((1,), (1,)), ((), ())), preferred_element_type=F32,
                           precision=precision)


def _dot_tn(a, b, precision=None):
    return lax.dot_general(a, b, (((0,), (0,)), ((), ())), preferred_element_type=F32,
                           precision=precision)


def _rms(x, gain, n):
    ms = jnp.sum(x * x, axis=-1, keepdims=True) * (1.0 / n)
    return x * lax.rsqrt(ms + EPS) * gain


def _silu(x):
    return x * jax.nn.sigmoid(x)


def _softplus(x):
    return jnp.maximum(x, 0.0) + jnp.log1p(jnp.exp(-jnp.abs(x)))


def _params(sem):
    return pltpu.CompilerParams(dimension_semantics=sem, vmem_limit_bytes=VMEM_LIMIT)


def _ffn_body(x_ref, g_ref, wi_ref, wo_ref, o_ref, act_scr, *, tf):
    x = x_ref[...]
    h = _rms(x, g_ref[...], D_MODEL).astype(MXU_DTYPE)
    for k in range(D_FF // tf):
        gate = _dot(h, wi_ref[:, k * tf:(k + 1) * tf])
        up = _dot(h, wi_ref[:, D_FF + k * tf:D_FF + (k + 1) * tf])
        act_scr[:, k * tf:(k + 1) * tf] = (_silu(gate) * up).astype(MXU_DTYPE)
    o_ref[...] = x + 0.5 * _dot(act_scr[...], wo_ref[...])


def _resident(shape):
    return pl.BlockSpec(shape, lambda *_: (0,) * len(shape), pipeline_mode=pl.Buffered(1))


def _ffn(x, gain, w_in, w_out, *, tm, tf):
    t, d = x.shape
    return pl.pallas_call(
        functools.partial(_ffn_body, tf=tf),
        grid=(t // tm,),
        in_specs=[
            pl.BlockSpec((tm, d), lambda i: (i, 0)),
            _resident((1, d)),
            _resident((d, 2 * D_FF)),
            _resident((D_FF, d)),
        ],
        out_specs=pl.BlockSpec((tm, d), lambda i: (i, 0)),
        out_shape=jax.ShapeDtypeStruct((t, d), F32),
        scratch_shapes=[pltpu.VMEM((tm, D_FF), MXU_DTYPE)],
        compiler_params=_params(("parallel",)),
        name="ffn",
    )(x, gain, w_in, w_out)


def _inproj_body(x_ref, g_ref, w_ref, wbat_ref, o_ref, tail_ref, bat_ref, *, tn):
    h = _rms(x_ref[...], g_ref[...], D_MODEL).astype(MXU_DTYPE)
    bat_ref[...] = _dot_nt(wbat_ref[...], h)
    n_col = N_PROJ // tn
    for j in range(n_col):
        r = _dot(h, w_ref[:, j * tn:(j + 1) * tn])
        o_ref[:, j * tn:(j + 1) * tn] = r.astype(o_ref.dtype)
        if j == n_col - 1:
            tail_ref[...] = r[:, tn - LANES:]


def _inproj(x, gain, w, wbat, *, tm, tn):
    t, d = x.shape
    return pl.pallas_call(
        functools.partial(_inproj_body, tn=tn),
        grid=(t // tm,),
        in_specs=[
            pl.BlockSpec((tm, d), lambda i: (i, 0)),
            _resident((1, d)),
            _resident((d, N_PROJ)),
            _resident((8, d)),
        ],
        out_specs=[
            pl.BlockSpec((tm, N_PROJ), lambda i: (i, 0)),
            pl.BlockSpec((tm, LANES), lambda i: (i, 0)),
            pl.BlockSpec((8, tm), lambda i: (0, i)),
        ],
        out_shape=[jax.ShapeDtypeStruct((t, N_PROJ), MXU_DTYPE), jax.ShapeDtypeStruct((t, LANES), F32),
                   jax.ShapeDtypeStruct((8, t), F32)],
        compiler_params=_params(("parallel",)),
        name="inproj",
    )(x, gain, w, wbat)


def _mla_rot(x, lane):
    return jnp.where(lane < MLA_NOPE + MLA_ROPE // 2, -pltpu.roll(x, LANES - MLA_ROPE // 2, 1),
                     pltpu.roll(x, MLA_ROPE // 2, 1))


def _lane_tile(x, n):
    return x if n == 1 else jnp.concatenate([x] * n, axis=1)


def _value_rows(v_ref, h, vt_h):
    n = vt_h.shape[1]
    v_ref[0, h, 0:MLA_V, :] = vt_h.astype(v_ref.dtype)
    pad_rows = lax.broadcasted_iota(jnp.int32, (V_ROWS - MLA_V, n), 0)
    v_ref[0, h, MLA_V:V_ROWS, :] = jnp.where(pad_rows == 0, 1.0, 0.0).astype(v_ref.dtype)


def _mla_prep_body(lat_ref, tail_ref, cqg_ref, ckvg_ref, wqt_ref, wk_ref, wvt_ref, qgt_ref, kg_ref, cost_ref,
                   sint_ref, cos_ref, sin_ref, q_ref, k_ref, v_ref, *, scale):
    lat = lat_ref[...].astype(F32)
    tm = lat.shape[0]
    cqn = _rms(lat[:, :MLA_Q_RANK], cqg_ref[...], MLA_Q_RANK).astype(MXU_DTYPE)
    ckvn = _rms(lat[:, MLA_Q_RANK:MLA_Q_RANK + MLA_KV_RANK], ckvg_ref[...], MLA_KV_RANK).astype(MXU_DTYPE)

    qt_all = _dot_nt(wqt_ref[...], cqn)
    qgt = _lane_tile(qgt_ref[...], tm // LANES)
    cost = cost_ref[...]
    sint = sint_ref[...]
    half = MLA_ROPE // 2
    for h in range(MLA_HEADS):
        x = qt_all[h * LANES:(h + 1) * LANES, :]
        ms = jnp.sum(x * x, axis=0, keepdims=True) * (1.0 / MLA_QK)
        qn = x * lax.rsqrt(ms + EPS) * qgt
        rot = jnp.concatenate([jnp.zeros((MLA_NOPE, tm), F32), -qn[MLA_NOPE + half:MLA_QK],
                               qn[MLA_NOPE:MLA_NOPE + half], jnp.zeros((LANES - MLA_QK, tm), F32)], axis=0)
        q_ref[0, h] = ((qn * cost + rot * sint) * scale).astype(q_ref.dtype)

    lane = lax.broadcasted_iota(jnp.int32, (tm, LANES), 1)
    kr = jnp.where((lane >= MLA_NOPE) & (lane < MLA_QK), tail_ref[...], 0.0)
    kg = kg_ref[...]
    cos = cos_ref[...]
    rot_kr_sin = _mla_rot(kr * kg, lane) * sin_ref[...]
    k_all = _dot(ckvn, wk_ref[...])
    for h in range(MLA_HEADS):
        kh = k_all[:, h * LANES:(h + 1) * LANES] + kr
        r = lax.rsqrt(jnp.sum(kh * kh, axis=-1, keepdims=True) * (1.0 / MLA_QK) + EPS)
        k_ref[0, h] = (r * (kh * kg * cos + rot_kr_sin)).astype(k_ref.dtype)

    vt = _dot_nt(wvt_ref[...], ckvn)
    for h in range(MLA_HEADS):
        _value_rows(v_ref, h, vt[h * MLA_V:(h + 1) * MLA_V, :])


def _mla_prep(proj, tail, cqg, ckvg, wqt, wk, wvt, qgt, kg, cost, sint, cos, sin, *, b, s, tm):
    nt = s // tm
    scale = (MLA_QK ** -0.5) * LOG2E
    cblk = COL_MLA // 512
    full = lambda shape: pl.BlockSpec(shape, lambda bi, i: (0,) * len(shape))
    return pl.pallas_call(
        functools.partial(_mla_prep_body, scale=scale),
        grid=(b, nt),
        in_specs=[
            pl.BlockSpec((tm, 512), lambda bi, i: (bi * nt + i, cblk)),
            pl.BlockSpec((tm, LANES), lambda bi, i: (bi * nt + i, 0)),
            full((1, MLA_Q_RANK)), full((1, MLA_KV_RANK)),
            full((MLA_HEADS * LANES, MLA_Q_RANK)), full((MLA_KV_RANK, MLA_HEADS * LANES)),
            full((MIX_WIDTH, MLA_KV_RANK)), full((LANES, LANES)), full((1, LANES)),
            pl.BlockSpec((LANES, tm), lambda bi, i: (0, i)),
            pl.BlockSpec((LANES, tm), lambda bi, i: (0, i)),
            pl.BlockSpec((tm, LANES), lambda bi, i: (i, 0)),
            pl.BlockSpec((tm, LANES), lambda bi, i: (i, 0)),
        ],
        out_specs=[
            pl.BlockSpec((1, MLA_HEADS, LANES, tm), lambda bi, i: (bi, 0, 0, i)),
            pl.BlockSpec((1, MLA_HEADS, tm, LANES), lambda bi, i: (bi, 0, i, 0)),
            pl.BlockSpec((1, MLA_HEADS, V_ROWS, tm), lambda bi, i: (bi, 0, 0, i)),
        ],
        out_shape=[
            jax.ShapeDtypeStruct((b, MLA_HEADS, LANES, s), MXU_DTYPE),
            jax.ShapeDtypeStruct((b, MLA_HEADS, s, LANES), MXU_DTYPE),
            jax.ShapeDtypeStruct((b, MLA_HEADS, V_ROWS, s), MXU_DTYPE),
        ],
        compiler_params=_params(("parallel", "parallel")),
        name="mla_prep",
    )(proj, tail, cqg, ckvg, wqt, wk, wvt, qgt, kg, cost, sint, cos, sin)


def _moba_rot(x, lane):
    half = MOBA_DH // 2
    return jnp.where(lane % MOBA_DH < half, -pltpu.roll(x, LANES - half, 1), pltpu.roll(x, half, 1))


def _pair_rms(x, gain, lo_mask):
    sq = x * x
    ss_lo = jnp.sum(jnp.where(lo_mask, sq, 0.0), axis=-1, keepdims=True)
    ss_hi = jnp.sum(jnp.where(lo_mask, 0.0, sq), axis=-1, keepdims=True)
    r = jnp.where(lo_mask, lax.rsqrt(ss_lo * (1.0 / MOBA_DH) + EPS), lax.rsqrt(ss_hi * (1.0 / MOBA_DH) + EPS))
    return x * r * gain


def _select_bias(gate, blk, blk_f, j):
    g = jnp.where(blk < j, gate, SENTINEL)
    allowed = blk == j
    for _ in range(MOBA_TOPK):
        m = jnp.max(g, axis=0, keepdims=True)
        first = jnp.min(jnp.where(g == m, blk_f, 1e9), axis=0, keepdims=True)
        pick = (blk_f == first) & (m > SENTINEL)
        allowed = allowed | pick
        g = jnp.where(pick, SENTINEL, g)
    return jnp.where(allowed, 0.0, NEG_INF)


def _moba_prep_body(qkv_ref, qgt_ref, kg_ref, cost_ref, sint_ref, cos_ref, sin_ref, q_ref, k_ref, v_ref, kmt_scr,
                    *, scale):
    j = pl.program_id(1)

    @pl.when(j == 0)
    def _():
        kmt_scr[...] = jnp.zeros_like(kmt_scr)

    tq = qkv_ref.shape[0]
    hw = MOBA_HEADS * MOBA_DH
    dh, half, nblk = MOBA_DH, MOBA_DH // 2, MOBA_MAX_BLOCKS

    vt = qkv_ref[:, 2 * hw:].astype(F32).T
    for h in range(MOBA_HEADS):
        _value_rows(v_ref, h, vt[h * dh:(h + 1) * dh, :])

    qt_all = qkv_ref[:, :hw].astype(F32).T
    qgt = _lane_tile(qgt_ref[...], tq // LANES)
    cost = cost_ref[...]
    sint = sint_ref[...]
    blk = lax.broadcasted_iota(jnp.int32, (nblk, tq), 0)
    blk_f = blk.astype(F32)
    zeros = jnp.zeros((nblk, tq), F32)

    lane = lax.broadcasted_iota(jnp.int32, (tq, LANES), 1)
    lo_mask = lane < dh
    cos = cos_ref[...]
    sin = sin_ref[...]
    row_lane = lax.broadcasted_iota(jnp.int32, (1, LANES), 1)
    slot = lax.broadcasted_iota(jnp.int32, (LANES, LANES), 0)
    for p in range(MOBA_HEADS // 2):
        x = qt_all[p * LANES:(p + 1) * LANES, :]
        sq = x * x
        r_e = lax.rsqrt(jnp.sum(sq[:dh], axis=0, keepdims=True) * (1.0 / dh) + EPS)
        r_o = lax.rsqrt(jnp.sum(sq[dh:], axis=0, keepdims=True) * (1.0 / dh) + EPS)
        qn = jnp.concatenate([x[:dh] * r_e, x[dh:] * r_o], axis=0) * qgt
        rot = jnp.concatenate([-qn[half:dh], qn[:half], -qn[dh + half:], qn[dh:dh + half]], axis=0)
        q = qn * cost + rot * sint
        gate = _dot(kmt_scr[p], q, precision=HIGHEST)
        bias_e = _select_bias(gate[dh:dh + nblk], blk, blk_f, j)
        bias_o = _select_bias(gate[:nblk], blk, blk_f, j)
        qs = q * scale
        q_ref[0, 2 * p] = jnp.concatenate([qs[:dh], bias_e, zeros], axis=0).astype(q_ref.dtype)
        q_ref[0, 2 * p + 1] = jnp.concatenate([bias_o, zeros, qs[dh:]], axis=0).astype(q_ref.dtype)

        kn = _pair_rms(qkv_ref[:, hw + p * LANES:hw + (p + 1) * LANES].astype(F32), kg_ref[...], lo_mask)
        k = kn * cos + _moba_rot(kn, lane) * sin
        k_ref[0, 2 * p] = jnp.where(lo_mask, k, jnp.where(lane == dh + j, 1.0, 0.0)).astype(k_ref.dtype)
        k_ref[0, 2 * p + 1] = jnp.where(lo_mask, jnp.where(lane == j, 1.0, 0.0), k).astype(k_ref.dtype)
        km = jnp.sum(k, axis=0, keepdims=True) * (1.0 / MOBA_BLOCK)
        new_rows = jnp.where(slot == dh + j, jnp.where(row_lane < dh, km, 0.0),
                             jnp.where(row_lane < dh, 0.0, km))
        kmt_scr[p] = jnp.where((slot == dh + j) | (slot == j), new_rows, kmt_scr[p])


def _moba_prep(proj, qgt, kg, cost, sint, cos, sin, *, b, s):
    tq = MOBA_BLOCK
    nt = s // tq
    scale = (MOBA_DH ** -0.5) * LOG2E
    cblk = COL_MOBA // (3 * MOBA_HEADS * MOBA_DH)
    return pl.pallas_call(
        functools.partial(_moba_prep_body, scale=scale),
        grid=(b, nt),
        in_specs=[
            pl.BlockSpec((tq, 3 * MOBA_HEADS * MOBA_DH), lambda bi, j: (bi * nt + j, cblk)),
            pl.BlockSpec((LANES, LANES), lambda bi, j: (0, 0)),
            pl.BlockSpec((1, LANES), lambda bi, j: (0, 0)),
            pl.BlockSpec((LANES, tq), lambda bi, j: (0, j)),
            pl.BlockSpec((LANES, tq), lambda bi, j: (0, j)),
            pl.BlockSpec((tq, LANES), lambda bi, j: (j, 0)),
            pl.BlockSpec((tq, LANES), lambda bi, j: (j, 0)),
        ],
        out_specs=[
            pl.BlockSpec((1, MOBA_HEADS, LANES, tq), lambda bi, j: (bi, 0, 0, j)),
            pl.BlockSpec((1, MOBA_HEADS, tq, LANES), lambda bi, j: (bi, 0, j, 0)),
            pl.BlockSpec((1, MOBA_HEADS, V_ROWS, tq), lambda bi, j: (bi, 0, 0, j)),
        ],
        out_shape=[
            jax.ShapeDtypeStruct((b, MOBA_HEADS, LANES, s), MXU_DTYPE),
            jax.ShapeDtypeStruct((b, MOBA_HEADS, s, LANES), MXU_DTYPE),
            jax.ShapeDtypeStruct((b, MOBA_HEADS, V_ROWS, s), MXU_DTYPE),
        ],
        scratch_shapes=[pltpu.VMEM((MOBA_HEADS // 2, LANES, LANES), F32)],
        compiler_params=_params(("parallel", "arbitrary")),
        name="moba_prep",
    )(proj, qgt, kg, cost, sint, cos, sin)


def _flash_body(qt_ref, k_ref, vt_ref, o_ref, buf0, buf1, *, tq):
    i = pl.program_id(2)
    qts = (qt_ref[0, 0], qt_ref[0, 1])

    def scores(t, buf, hh, diagonal=False):
        ks = pl.ds(pl.multiple_of(t * tq, tq), tq)
        s = _dot(k_ref[0, hh, ks, :], qts[hh])
        if diagonal:
            key = lax.broadcasted_iota(jnp.int32, s.shape, 0)
            qry = lax.broadcasted_iota(jnp.int32, s.shape, 1)
            s = jnp.where(key <= qry, s, NEG_INF)
        buf[hh] = s
        return jnp.max(s, axis=0, keepdims=True)

    def consume(t, buf, hh, smax, stat):
        ks = pl.ds(pl.multiple_of(t * tq, tq), tq)
        m, acc = stat
        m_new = jnp.maximum(m, smax)
        p = jnp.exp2(buf[hh] - m_new)
        return m_new, jnp.exp2(m - m_new) * acc + _dot(vt_ref[0, hh, :, ks], p.astype(MXU_DTYPE))

    def step(t_next, buf_next, t_cur, buf_cur, carry):
        smax, stats = carry
        new_smax, new_stats = [], []
        for hh in range(2):
            new_smax.append(scores(t_next, buf_next, hh))
            new_stats.append(consume(t_cur, buf_cur, hh, smax[hh], stats[hh]))
        return tuple(new_smax), tuple(new_stats)

    def run(first_cur, t_first, carry, bufs):
        n_steps = i - t_first

        def quad(u, carry):
            t0 = t_first + 4 * u
            carry = step(t0, bufs[0], jnp.where(u == 0, first_cur, t0 - 1), bufs[1], carry)
            carry = step(t0 + 1, bufs[1], t0, bufs[0], carry)
            carry = step(t0 + 2, bufs[0], t0 + 1, bufs[1], carry)
            return step(t0 + 3, bufs[1], t0 + 2, bufs[0], carry)

        carry = lax.fori_loop(0, n_steps // 4, quad, carry)
        t_pair = t_first + 4 * (n_steps // 4)

        def pair(u, carry):
            carry = step(t_pair, bufs[0], jnp.where(t_pair == t_first, first_cur, t_pair - 1), bufs[1], carry)
            return step(t_pair + 1, bufs[1], t_pair, bufs[0], carry)

        return lax.fori_loop(0, (n_steps % 4) // 2, pair, carry)

    def finish(t_last, buf, carry):
        smax, stats = carry
        out = [consume(t_last, buf, hh, smax[hh], stats[hh])[1] for hh in range(2)]
        o_t = jnp.concatenate([a[:MLA_V] / a[MLA_V:MLA_V + 1] for a in out], axis=0)
        o_ref[0] = o_t.T.astype(o_ref.dtype)

    init = ((jnp.full((1, tq), NEG_INF, F32), jnp.zeros((V_ROWS, tq), F32)),) * 2

    @pl.when(i % 2 == 0)
    def _():
        smax = tuple(scores(i, buf1, hh, True) for hh in range(2))
        carry = run(i, 0, (smax, init), (buf0, buf1))
        finish(jnp.where(i == 0, 0, i - 1), buf1, carry)

    @pl.when(i % 2 == 1)
    def _():
        smax = tuple(scores(i, buf0, hh, True) for hh in range(2))
        carry = step(0, buf1, i, buf0, (smax, init))
        carry = run(0, 1, carry, (buf0, buf1))
        finish(i - 1, buf1, carry)


def _flash(qt, k, vt, *, tq):
    b, h, s, _ = k.shape
    return pl.pallas_call(
        functools.partial(_flash_body, tq=tq),
        grid=(b, h // 2, s // tq),
        in_specs=[
            pl.BlockSpec((1, 2, LANES, tq), lambda bi, p, i: (bi, p, 0, i)),
            pl.BlockSpec((1, 2, s, LANES), lambda bi, p, i: (bi, p, 0, 0)),
            pl.BlockSpec((1, 2, V_ROWS, s), lambda bi, p, i: (bi, p, 0, 0)),
        ],
        out_specs=pl.BlockSpec((1, tq, LANES), lambda bi, p, i: (bi, i, p)),
        out_shape=jax.ShapeDtypeStruct((b, s, MIX_WIDTH), MXU_DTYPE),
        scratch_shapes=[pltpu.VMEM((2, tq, tq), F32), pltpu.VMEM((2, tq, tq), F32)],
        compiler_params=_params(("parallel", "parallel", "arbitrary")),
        name="flash",
    )(qt, k, vt)


def _hi_lo(x):
    hi = x.astype(MXU_DTYPE)
    return hi, (x - hi.astype(F32)).astype(MXU_DTYPE)


def _unit_lower_solve(a_list, rhs_list, eye):
    c = eye.shape[0]
    n = range(len(a_list))
    t = [eye - a for a in a_list]
    pk = []
    for a in a_list:
        pwm = (-a).astype(MXU_DTYPE)
        pk.append(_dot(pwm, pwm))
    n_sq = int(np.log2(c)) - 1
    for it in range(n_sq):
        if it < n_sq - 1:
            both = [_dot(jnp.concatenate([t[j], pk[j]], axis=0).astype(MXU_DTYPE), pk[j].astype(MXU_DTYPE))
                    for j in n]
            t = [t[j] + both[j][:c] for j in n]
            pk = [both[j][c:] for j in n]
        else:
            t = [t[j] + _dot(t[j].astype(MXU_DTYPE), pk[j].astype(MXU_DTYPE)) for j in n]
    tm = [x.astype(MXU_DTYPE) for x in t]
    x1 = [_dot(tm[j], rhs_list[j].astype(MXU_DTYPE)) for j in n]
    a_split = [_hi_lo(a) for a in a_list]
    x_split = [_hi_lo(x) for x in x1]
    ax = [_dot(jnp.concatenate(a_split[j], axis=0), x_split[j][0]) for j in n]
    ax_lo = [_dot(a_split[j][0], x_split[j][1]) for j in n]
    res = [rhs_list[j] - x1[j] - (ax[j][:c] + ax[j][c:] + ax_lo[j]) for j in n]
    return [x1[j] + _dot(tm[j], res[j].astype(MXU_DTYPE)) for j in n]


def _gdn_prep_body(qkv_ref, halo_ref, ba_ref, bat_ref, conv_ref, alog_ref, dtb_ref, alogt_ref, dtbt_ref,
                   u_ref, w_ref, qd_ref, kd_ref, qk_ref, egl_ref, xe_scr, y_scr, bg_scr,
                   *, tiles_per_seq, nc):
    i = pl.program_id(0)
    tm = qkv_ref.shape[0]
    c = GDN_CHUNK
    hk = GDN_HEADS * GDN_DK

    xe_scr[pl.ds(0, GDN_HALO), :] = jnp.where(i % tiles_per_seq == 0, 0.0, halo_ref[...].astype(F32))
    xe_scr[pl.ds(GDN_HALO, tm), :] = qkv_ref[...].astype(F32)
    cw = conv_ref[...]
    xe = xe_scr[...]
    y = xe[GDN_HALO:] * cw[GDN_CONV - 1:GDN_CONV]
    for back in range(1, GDN_CONV):
        y = y + pltpu.roll(xe, back, 0)[GDN_HALO:] * cw[GDN_CONV - 1 - back:GDN_CONV - back]
    y_scr[...] = _silu(y)

    ba = ba_ref[...]
    lane = lax.broadcasted_iota(jnp.int32, ba.shape, 1)
    g_col = -jnp.exp(alog_ref[...]) * _softplus(ba + dtb_ref[...])
    bg_scr[...] = jnp.where(lane < GDN_HEADS, jax.nn.sigmoid(ba), g_col)

    ri = lax.broadcasted_iota(jnp.int32, (c, c), 0)
    ci = lax.broadcasted_iota(jnp.int32, (c, c), 1)
    tril = ci <= ri
    strict = ci < ri
    eye = jnp.where(ci == ri, 1.0, 0.0)
    lower_ones = jnp.where(tril, 1.0, 0.0)
    upper_ones = jnp.where(ci >= ri, 1.0, 0.0)

    def chunk_group(grp, _):
        chs = [grp * GDN_GROUP + j for j in range(GDN_GROUP)]
        rows = [pl.ds(pl.multiple_of(ch * c, c), c) for ch in chs]
        bg = [bg_scr[r, :] for r in rows]
        gcum_col = [_dot(lower_ones, x, precision=HIGHEST) for x in bg]
        g_row = [-jnp.exp(alogt_ref[...]) * _softplus(bat_ref[ch] + dtbt_ref[...]) for ch in chs]
        gcum_row = [_dot(x, upper_ones, precision=HIGHEST) for x in g_row]
        probs = [(j, h) for j in range(GDN_GROUP) for h in range(GDN_HEADS)]
        n = range(len(probs))
        sls = [slice(h * GDN_DK, (h + 1) * GDN_DK) for _, h in probs]
        qh = [y_scr[rows[j], sls[i]] for i, (j, h) in enumerate(probs)]
        kh = [y_scr[rows[j], pl.ds(hk + h * GDN_DK, GDN_DK)] for j, h in probs]
        vh = [y_scr[rows[j], pl.ds(2 * hk + h * GDN_DV, GDN_DV)] for j, h in probs]
        qn = [x * lax.rsqrt(jnp.sum(x * x, axis=-1, keepdims=True) + EPS) * (GDN_DK ** -0.5) for x in qh]
        kn = [x * lax.rsqrt(jnp.sum(x * x, axis=-1, keepdims=True) + EPS) for x in kh]
        beta = [bg[j][:, h:h + 1] for j, h in probs]
        gc = [gcum_col[j][:, GDN_HEADS + h:GDN_HEADS + h + 1] for j, h in probs]
        gr = [gcum_row[j][GDN_HEADS + h:GDN_HEADS + h + 1, :] for j, h in probs]
        kb = [kn[i] * beta[i] for i in n]
        knm = [x.astype(MXU_DTYPE) for x in kn]
        dec = [jnp.exp(jnp.where(tril, gc[i] - gr[i], NEG_INF)) for i in n]
        skk = [_dot_nt(kb[i].astype(MXU_DTYPE), knm[i]) for i in n]
        sqk = [_dot_nt(qn[i].astype(MXU_DTYPE), knm[i]) for i in n]
        a = [jnp.where(strict, skk[i] * dec[i], 0.0) for i in n]
        eg = [jnp.exp(x) for x in gc]
        rhs = [jnp.concatenate([vh[i] * beta[i], kb[i] * eg[i]], axis=1) for i in n]
        sol = _unit_lower_solve(a, rhs, eye)
        for i, (j, h) in enumerate(probs):
            sl, r = sls[i], rows[j]
            gl = gc[i][c - 1:c, :]
            u_ref[r, sl] = sol[i][:, :GDN_DV]
            w_ref[r, sl] = sol[i][:, GDN_DV:].astype(w_ref.dtype)
            qd_ref[r, sl] = (qn[i] * eg[i]).astype(qd_ref.dtype)
            kd_ref[r, sl] = (kn[i] * jnp.exp(gl - gc[i])).astype(kd_ref.dtype)
            qk_ref[r, pl.ds(h * LANES, c)] = (sqk[i] * dec[i]).astype(qk_ref.dtype)
            qk_ref[r, pl.ds(h * LANES + c, LANES - c)] = jnp.zeros((c, LANES - c), qk_ref.dtype)
            egl_ref[chs[j], :, sl] = jnp.broadcast_to(jnp.exp(gl), (1, GDN_DV))
        return 0

    lax.fori_loop(0, nc // GDN_GROUP, chunk_group, 0)


def _gdn_prep(proj, tail, bat, conv, alog, dtb, alogt, dtbt, *, s, tm):
    t = proj.shape[0]
    nc = tm // GDN_CHUNK
    hk = GDN_HEADS * GDN_DK
    width = 3 * hk
    cblk = COL_GDN // width
    full = lambda shape: pl.BlockSpec(shape, lambda i: (0,) * len(shape))
    row = lambda w: pl.BlockSpec((tm, w), lambda i: (i, 0))
    return pl.pallas_call(
        functools.partial(_gdn_prep_body, tiles_per_seq=s // tm, nc=nc),
        grid=(t // tm,),
        in_specs=[
            pl.BlockSpec((tm, width), lambda i: (i, cblk)),
            pl.BlockSpec((GDN_HALO, width), lambda i: (jnp.maximum(i * (tm // GDN_HALO) - 1, 0), cblk)),
            pl.BlockSpec((tm, LANES), lambda i: (i, 0)),
            pl.BlockSpec((nc, 8, GDN_CHUNK), lambda i: (i, 0, 0)),
            full((GDN_CONV, width)), full((1, LANES)), full((1, LANES)), full((8, 1)), full((8, 1)),
        ],
        out_specs=[row(hk), row(hk), row(hk), row(hk), row(GDN_HEADS * LANES),
                   pl.BlockSpec((nc, 1, hk), lambda i: (i, 0, 0))],
        out_shape=[
            jax.ShapeDtypeStruct((t, hk), F32),
            jax.ShapeDtypeStruct((t, hk), MXU_DTYPE),
            jax.ShapeDtypeStruct((t, hk), MXU_DTYPE),
            jax.ShapeDtypeStruct((t, hk), MXU_DTYPE),
            jax.ShapeDtypeStruct((t, GDN_HEADS * LANES), MXU_DTYPE),
            jax.ShapeDtypeStruct((t // GDN_CHUNK, 1, hk), F32),
        ],
        scratch_shapes=[
            pltpu.VMEM((tm + GDN_HALO, width), F32),
            pltpu.VMEM((tm, width), F32),
            pltpu.VMEM((tm, LANES), F32),
        ],
        compiler_params=_params(("parallel",)),
        name="gdn_prep",
    )(proj, proj, tail, bat, conv, alog, dtb, alogt, dtbt)


def _gdn_scan_body(u_ref, w_ref, qd_ref, kd_ref, qk_ref, egl_ref, z_ref, og_ref, o_ref, s_scr, *, nb, nc):
    @pl.when(pl.program_id(0) == 0)
    def _():
        s_scr[...] = jnp.zeros_like(s_scr)

    c = GDN_CHUNK
    og = og_ref[...]

    def chunk(ch, _):
        rows = pl.ds(pl.multiple_of(ch * c, c), c)
        probs = [(bi, h) for bi in range(nb) for h in range(GDN_HEADS)]
        n = range(len(probs))
        sls = [slice(h * GDN_DV, (h + 1) * GDN_DV) for _, h in probs]
        st = [s_scr[bi * GDN_HEADS + h] for bi, h in probs]
        stm = [x.astype(MXU_DTYPE) for x in st]
        wq = [jnp.concatenate([w_ref[bi, rows, sls[i]], qd_ref[bi, rows, sls[i]]], axis=0)
              for i, (bi, h) in enumerate(probs)]
        r1 = [_dot(wq[i], stm[i]) for i in n]
        vm = [(u_ref[bi, rows, sls[i]] - r1[i][:c]).astype(MXU_DTYPE) for i, (bi, h) in enumerate(probs)]
        o2 = [_dot(qk_ref[bi, rows, pl.ds(h * LANES, c)], vm[i]) for i, (bi, h) in enumerate(probs)]
        sd = [_dot_tn(kd_ref[bi, rows, sls[i]], vm[i]) for i, (bi, h) in enumerate(probs)]
        for i, (bi, h) in enumerate(probs):
            s_scr[bi * GDN_HEADS + h] = st[i] * egl_ref[bi, ch, :, sls[i]] + sd[i]
            o = r1[i][c:] + o2[i]
            o_ref[bi, rows, sls[i]] = (_rms(o, og, GDN_DV)
                                       * _silu(z_ref[bi, rows, sls[i]].astype(F32))).astype(o_ref.dtype)
        return 0

    lax.fori_loop(0, nc, chunk, 0)


def _gdn_scan(u, w, qd, kd, qk, egl, proj3, og, *, tt):
    b, s, hk = u.shape
    nc = tt // GDN_CHUNK
    blk = lambda width: pl.BlockSpec((b, tt, width), lambda i: (0, i, 0))
    return pl.pallas_call(
        functools.partial(_gdn_scan_body, nb=b, nc=nc),
        grid=(s // tt,),
        in_specs=[
            blk(hk), blk(hk), blk(hk), blk(hk), blk(GDN_HEADS * LANES),
            pl.BlockSpec((b, nc, 1, hk), lambda i: (0, i, 0, 0)),
            pl.BlockSpec((b, tt, hk), lambda i: (0, i, COL_Z // hk)),
            pl.BlockSpec((1, GDN_DV), lambda i: (0, 0)),
        ],
        out_specs=blk(hk),
        out_shape=jax.ShapeDtypeStruct((b, s, hk), MXU_DTYPE),
        scratch_shapes=[pltpu.VMEM((b * GDN_HEADS, GDN_DK, GDN_DV), F32)],
        compiler_params=_params(("arbitrary",)),
        name="gdn_scan",
    )(u, w, qd, kd, qk, egl, proj3, og)


def _merge_body(x_ref, om_ref, og_ref, ob_ref, gl_ref, wb_ref, wo_ref, o_ref):
    mixed = None
    for n, br in enumerate((om_ref, og_ref, ob_ref)):
        up = _dot(br[...], wb_ref[n])
        term = jax.nn.sigmoid(gl_ref[:, n * D_MODEL:(n + 1) * D_MODEL].astype(F32)) * up
        mixed = term if mixed is None else mixed + term
    o_ref[...] = x_ref[...] + _dot(mixed.astype(MXU_DTYPE), wo_ref[...])


def _merge(x, o_mla, o_gdn, o_moba, proj, wb, wo, *, tm):
    t, d = x.shape
    row = lambda w: pl.BlockSpec((tm, w), lambda i: (i, 0))
    return pl.pallas_call(
        _merge_body,
        grid=(t // tm,),
        in_specs=[
            row(d), row(MIX_WIDTH), row(MIX_WIDTH), row(MIX_WIDTH), row(3 * d),
            _resident((3, MIX_WIDTH, d)),
            _resident((d, d)),
        ],
        out_specs=row(d),
        out_shape=jax.ShapeDtypeStruct((t, d), F32),
        compiler_params=_params(("parallel",)),
        name="merge",
    )(x, o_mla, o_gdn, o_moba, proj, wb, wo)


def _rope_tables(s):
    pos = jnp.arange(s, dtype=jnp.int32).astype(F32)[:, None]

    def cs(d):
        half = d // 2
        inv_freq = ROPE_THETA ** (-jnp.arange(half, dtype=F32) * 2.0 / d)
        ang = pos * inv_freq[None, :]
        return jnp.cos(ang), jnp.sin(ang)

    c, sn = cs(MLA_ROPE)
    pad = LANES - MLA_QK
    mla_cos = jnp.concatenate([jnp.ones((s, MLA_NOPE), F32), c, c, jnp.zeros((s, pad), F32)], axis=1)
    mla_sin = jnp.concatenate([jnp.zeros((s, MLA_NOPE), F32), sn, sn, jnp.zeros((s, pad), F32)], axis=1)
    c, sn = cs(MOBA_DH)
    moba_cos, moba_sin = jnp.tile(c, (1, 4)), jnp.tile(sn, (1, 4))
    return mla_cos, mla_sin, mla_cos.T, mla_sin.T, moba_cos, moba_sin, moba_cos.T, moba_sin.T


def _proj_weights(w_in):
    d = w_in.shape[0]
    w_in = w_in.astype(MXU_DTYPE)
    o = np.cumsum((0, MLA_Q_RANK, MLA_KV_RANK, MLA_ROPE, 512, 512, 512, GDN_HEADS, GDN_HEADS, 512, 1536, 3 * D_MODEL))
    c_q, c_kv, k_rope = w_in[:, o[0]:o[1]], w_in[:, o[1]:o[2]], w_in[:, o[2]:o[3]]
    gdn_qkv = w_in[:, o[3]:o[6]]
    ba = w_in[:, o[6]:o[8]]
    z = w_in[:, o[8]:o[9]]
    moba = w_in[:, o[9]:o[10]]
    gate = w_in[:, o[10]:o[11]]
    zeros = lambda n: jnp.zeros((d, n), w_in.dtype)
    w = jnp.concatenate([gate, gdn_qkv, moba, z, c_q, c_kv, ba, zeros(MLA_NOPE - 2 * GDN_HEADS), k_rope,
                         zeros(LANES - MLA_QK)], axis=1)
    assert w.shape[1] == N_PROJ
    return w, ba.T


def _lane_pad(v, n=LANES):
    return jnp.pad(v, (0, n - v.shape[0]))[None, :]


def kernel(x, ffa_norm, ffa_w_in, ffa_w_out, mix_norm, w_in, mla_cq_norm, mla_ckv_norm, mla_w_uq, mla_w_ukv, mla_q_norm, mla_k_norm, gdn_conv, gdn_a_log, gdn_dt_bias, gdn_out_norm, moba_q_norm, moba_k_norm, w_branch, w_out, ffb_norm, ffb_w_in, ffb_w_out):
    b, s, d = x.shape
    t = b * s
    assert d == D_MODEL and s % 512 == 0 and s // MOBA_BLOCK <= MOBA_MAX_BLOCKS
    depth = ffa_norm.shape[0]
    tm = 512
    mla_cos, mla_sin, mla_cost, mla_sint, moba_cos, moba_sin, moba_cost, moba_sint = _rope_tables(s)
    lane_bcast = lambda v: jnp.broadcast_to(v[:, None], (LANES, LANES))
    x = x.reshape(t, d)
    for l in range(depth):
        x = _ffn(x, ffa_norm[l][None, :], ffa_w_in[l].astype(MXU_DTYPE), ffa_w_out[l].astype(MXU_DTYPE),
                 tm=tm, tf=256)

        w_proj, w_bat = _proj_weights(w_in[l])
        proj, tail, bat = _inproj(x, mix_norm[l][None, :], w_proj, w_bat, tm=tm, tn=512)

        wq = jnp.pad(mla_w_uq[l].reshape(MLA_Q_RANK, MLA_HEADS, MLA_QK),
                     ((0, 0), (0, 0), (0, LANES - MLA_QK))).reshape(MLA_Q_RANK, MLA_HEADS * LANES)
        wkv = mla_w_ukv[l].reshape(MLA_KV_RANK, MLA_HEADS, MLA_NOPE + MLA_V)
        wk = jnp.pad(wkv[:, :, :MLA_NOPE], ((0, 0), (0, 0), (0, LANES - MLA_NOPE))).reshape(MLA_KV_RANK, MLA_HEADS * LANES)
        wv = wkv[:, :, MLA_NOPE:].reshape(MLA_KV_RANK, MLA_HEADS * MLA_V).T
        q, k, v = _mla_prep(proj, tail, mla_cq_norm[l][None, :], mla_ckv_norm[l][None, :], wq.T.astype(MXU_DTYPE),
                            wk.astype(MXU_DTYPE), wv.astype(MXU_DTYPE), lane_bcast(_lane_pad(mla_q_norm[l])[0]),
                            _lane_pad(mla_k_norm[l]), mla_cost, mla_sint, mla_cos, mla_sin, b=b, s=s, tm=tm)
        o_mla = _flash(q, k, v, tq=512)

        bat_chunks = bat.reshape(8, t // GDN_CHUNK, GDN_CHUNK).transpose(1, 0, 2)
        head_pad = lambda v: jnp.pad(v, (GDN_HEADS, LANES - 2 * GDN_HEADS))[None, :]
        head_col = lambda v: jnp.pad(v, (GDN_HEADS, 0))[:, None]
        u, w, qd, kd, qk, egl = _gdn_prep(proj, tail, bat_chunks, gdn_conv[l], head_pad(gdn_a_log[l]),
                                          head_pad(gdn_dt_bias[l]), head_col(gdn_a_log[l]),
                                          head_col(gdn_dt_bias[l]), s=s, tm=tm)
        r3 = lambda a: a.reshape(b, s, a.shape[-1])
        o_gdn = _gdn_scan(r3(u), r3(w), r3(qd), r3(kd), r3(qk), egl.reshape(b, s // GDN_CHUNK, 1, -1),
                          r3(proj), gdn_out_norm[l][None, :], tt=tm)

        q, k, v = _moba_prep(proj, lane_bcast(jnp.tile(moba_q_norm[l], 2)), jnp.tile(moba_k_norm[l], 2)[None, :],
                             moba_cost, moba_sint, moba_cos, moba_sin, b=b, s=s)
        o_moba = _flash(q, k, v, tq=512)

        x = _merge(x, o_mla.reshape(t, -1), o_gdn.reshape(t, -1), o_moba.reshape(t, -1), proj,
                   w_branch[l].astype(MXU_DTYPE), w_out[l].astype(MXU_DTYPE), tm=tm)

        x = _ffn(x, ffb_norm[l][None, :], ffb_w_in[l].astype(MXU_DTYPE), ffb_w_out[l].astype(MXU_DTYPE),
                 tm=tm, tf=256)
    return x.reshape(b, s, d)
```

```python
import functools

import numpy as np
import jax
import jax.numpy as jnp
from jax import lax
from jax.experimental import pallas as pl
from jax.experimental.pallas import tpu as pltpu

F32 = jnp.float32
BF16 = jnp.bfloat16
MXU_DTYPE = BF16
HIGHEST = lax.Precision.HIGHEST

EPS = 1e-6
ROPE_THETA = 10000.0
NEG_INF = -1e30
SENTINEL = -3e38
LOG2E = 1.4426950408889634

D_MODEL = 1024
D_FF = 2816
MIX_WIDTH = 512
LANES = 128

MLA_HEADS = 8
MLA_Q_RANK = 256
MLA_KV_RANK = 128
MLA_NOPE = 64
MLA_ROPE = 32
MLA_V = 64
MLA_QK = MLA_NOPE + MLA_ROPE

GDN_HEADS = 4
GDN_DK = 128
GDN_DV = 128
GDN_CONV = 4
GDN_CHUNK = 64
GDN_HALO = 16
GDN_GROUP = 4

MOBA_HEADS = 8
MOBA_DH = 64
MOBA_BLOCK = 256
MOBA_TOPK = 3
MOBA_MAX_BLOCKS = 32
V_ROWS = 80

COL_GATE = 0
COL_GDN = 3072
COL_MOBA = 4608
COL_Z = 6144
COL_MLA = 6656
N_PROJ = 7168

VMEM_LIMIT = 56 * 1024 * 1024


def _dot(a, b, precision=None):
    return jnp.dot(a, b, preferred_element_type=F32, precision=precision)


def _dot_nt(a, b, precision=None):
    return lax.dot_general(a, b, (((1,), (1,)), ((), ())), preferred_element_type=F32,
                           precision=precision)


def _dot_tn(a, b, precision=None):
    return lax.dot_general(a, b, (((0,), (0,)), ((), ())), preferred_element_type=F32,
                           precision=precision)


def _rms(x, gain, n):
    ms = jnp.sum(x * x, axis=-1, keepdims=True) * (1.0 / n)
    return x * lax.rsqrt(ms + EPS) * gain


def _silu(x):
    return x * jax.nn.sigmoid(x)


def _softplus(x):
    return jnp.maximum(x, 0.0) + jnp.log1p(jnp.exp(-jnp.abs(x)))


def _params(sem):
    return pltpu.CompilerParams(dimension_semantics=sem, vmem_limit_bytes=VMEM_LIMIT)


def _ffn_body(x_ref, g_ref, wi_ref, wo_ref, o_ref, act_scr, *, tf):
    x = x_ref[...]
    h = _rms(x, g_ref[...], D_MODEL).astype(MXU_DTYPE)
    for k in range(D_FF // tf):
        gate = _dot(h, wi_ref[:, k * tf:(k + 1) * tf].astype(MXU_DTYPE))
        up = _dot(h, wi_ref[:, D_FF + k * tf:D_FF + (k + 1) * tf].astype(MXU_DTYPE))
        act_scr[:, k * tf:(k + 1) * tf] = (_silu(gate) * up).astype(MXU_DTYPE)
    o_ref[...] = x + 0.5 * _dot(act_scr[...], wo_ref[...].astype(MXU_DTYPE))


def _resident(shape):
    return pl.BlockSpec(shape, lambda *_: (0,) * len(shape), pipeline_mode=pl.Buffered(1))


def _ffn(x, gain, w_in, w_out, *, tm, tf):
    t, d = x.shape
    return pl.pallas_call(
        functools.partial(_ffn_body, tf=tf),
        grid=(t // tm,),
        in_specs=[
            pl.BlockSpec((tm, d), lambda i: (i, 0)),
            _resident((1, d)),
            _resident((d, 2 * D_FF)),
            _resident((D_FF, d)),
        ],
        out_specs=pl.BlockSpec((tm, d), lambda i: (i, 0)),
        out_shape=jax.ShapeDtypeStruct((t, d), F32),
        scratch_shapes=[pltpu.VMEM((tm, D_FF), MXU_DTYPE)],
        compiler_params=_params(("parallel",)),
        name="ffn",
    )(x, gain, w_in, w_out)


def _inproj_body(x_ref, g_ref, *refs, widths, tn):
    w_refs, wbat_ref = refs[:len(widths)], refs[len(widths)]
    o_ref, tail_ref, bat_ref = refs[len(widths) + 1:]
    h = _rms(x_ref[...], g_ref[...], D_MODEL).astype(MXU_DTYPE)
    bat_ref[...] = _dot_nt(wbat_ref[...], h)
    off = 0
    for w_ref, width in zip(w_refs, widths):
        for a in range(0, width, tn):
            n = min(tn, width - a)
            r = _dot(h, w_ref[:, a:a + n])
            o_ref[:, off + a:off + a + n] = r.astype(o_ref.dtype)
        off += width
    tail_ref[...] = r


def _inproj(x, gain, w_groups, wbat, *, tm, tn):
    t, d = x.shape
    widths = tuple(w.shape[1] for w in w_groups)
    assert sum(widths) == N_PROJ and widths[-1] == LANES
    return pl.pallas_call(
        functools.partial(_inproj_body, widths=widths, tn=tn),
        grid=(t // tm,),
        in_specs=[pl.BlockSpec((tm, d), lambda i: (i, 0)), _resident((1, d))]
                 + [_resident((d, w)) for w in widths] + [_resident((8, d))],
        out_specs=[
            pl.BlockSpec((tm, N_PROJ), lambda i: (i, 0)),
            pl.BlockSpec((tm, LANES), lambda i: (i, 0)),
            pl.BlockSpec((8, tm), lambda i: (0, i)),
        ],
        out_shape=[jax.ShapeDtypeStruct((t, N_PROJ), MXU_DTYPE), jax.ShapeDtypeStruct((t, LANES), F32),
                   jax.ShapeDtypeStruct((8, t), F32)],
        compiler_params=_params(("parallel",)),
        name="inproj",
    )(x, gain, *w_groups, wbat)


def _mla_rot(x, lane):
    return jnp.where(lane < MLA_NOPE + MLA_ROPE // 2, -pltpu.roll(x, LANES - MLA_ROPE // 2, 1),
                     pltpu.roll(x, MLA_ROPE // 2, 1))


def _lane_tile(x, n):
    return x if n == 1 else jnp.concatenate([x] * n, axis=1)


def _value_rows(v_ref, h, vt_h):
    n = vt_h.shape[1]
    v_ref[0, h, 0:MLA_V, :] = vt_h.astype(v_ref.dtype)
    pad_rows = lax.broadcasted_iota(jnp.int32, (V_ROWS - MLA_V, n), 0)
    v_ref[0, h, MLA_V:V_ROWS, :] = jnp.where(pad_rows == 0, 1.0, 0.0).astype(v_ref.dtype)


def _mla_prep_body(lat_ref, tail_ref, cqg_ref, ckvg_ref, wqt_ref, wk_ref, wvt_ref, qgt_ref, kg_ref, cost_ref,
                   sint_ref, cos_ref, sin_ref, q_ref, k_ref, v_ref, *, scale):
    lat = lat_ref[...].astype(F32)
    tm = lat.shape[0]
    cqn = _rms(lat[:, :MLA_Q_RANK], cqg_ref[...], MLA_Q_RANK).astype(MXU_DTYPE)
    ckvn = _rms(lat[:, MLA_Q_RANK:MLA_Q_RANK + MLA_KV_RANK], ckvg_ref[...], MLA_KV_RANK).astype(MXU_DTYPE)

    qt_all = _dot_nt(wqt_ref[...], cqn)
    qgt = _lane_tile(qgt_ref[...], tm // LANES)
    cost = cost_ref[...]
    sint = sint_ref[...]
    half = MLA_ROPE // 2
    for h in range(MLA_HEADS):
        x = qt_all[h * LANES:(h + 1) * LANES, :]
        ms = jnp.sum(x * x, axis=0, keepdims=True) * (1.0 / MLA_QK)
        qn = x * lax.rsqrt(ms + EPS) * qgt
        rot = jnp.concatenate([jnp.zeros((MLA_NOPE, tm), F32), -qn[MLA_NOPE + half:MLA_QK],
                               qn[MLA_NOPE:MLA_NOPE + half], jnp.zeros((LANES - MLA_QK, tm), F32)], axis=0)
        q_ref[0, h] = ((qn * cost + rot * sint) * scale).astype(q_ref.dtype)

    lane = lax.broadcasted_iota(jnp.int32, (tm, LANES), 1)
    kr = jnp.where((lane >= MLA_NOPE) & (lane < MLA_QK), tail_ref[...], 0.0)
    kg = kg_ref[...]
    cos = cos_ref[...]
    rot_kr_sin = _mla_rot(kr * kg, lane) * sin_ref[...]
    k_all = _dot(ckvn, wk_ref[...])
    for h in range(MLA_HEADS):
        kh = k_all[:, h * LANES:(h + 1) * LANES] + kr
        r = lax.rsqrt(jnp.sum(kh * kh, axis=-1, keepdims=True) * (1.0 / MLA_QK) + EPS)
        k_ref[0, h] = (r * (kh * kg * cos + rot_kr_sin)).astype(k_ref.dtype)

    vt = _dot_nt(wvt_ref[...], ckvn)
    for h in range(MLA_HEADS):
        _value_rows(v_ref, h, vt[h * MLA_V:(h + 1) * MLA_V, :])


def _mla_prep(proj, tail, cqg, ckvg, wqt, wk, wvt, qgt, kg, cost, sint, cos, sin, *, b, s, tm):
    nt = s // tm
    scale = (MLA_QK ** -0.5) * LOG2E
    cblk = COL_MLA // 512
    full = lambda shape: pl.BlockSpec(shape, lambda bi, i: (0,) * len(shape))
    return pl.pallas_call(
        functools.partial(_mla_prep_body, scale=scale),
        grid=(b, nt),
        in_specs=[
            pl.BlockSpec((tm, 512), lambda bi, i: (bi * nt + i, cblk)),
            pl.BlockSpec((tm, LANES), lambda bi, i: (bi * nt + i, 0)),
            full((1, MLA_Q_RANK)), full((1, MLA_KV_RANK)),
            full((MLA_HEADS * LANES, MLA_Q_RANK)), full((MLA_KV_RANK, MLA_HEADS * LANES)),
            full((MIX_WIDTH, MLA_KV_RANK)), full((LANES, LANES)), full((1, LANES)),
            pl.BlockSpec((LANES, tm), lambda bi, i: (0, i)),
            pl.BlockSpec((LANES, tm), lambda bi, i: (0, i)),
            pl.BlockSpec((tm, LANES), lambda bi, i: (i, 0)),
            pl.BlockSpec((tm, LANES), lambda bi, i: (i, 0)),
        ],
        out_specs=[
            pl.BlockSpec((1, MLA_HEADS, LANES, tm), lambda bi, i: (bi, 0, 0, i)),
            pl.BlockSpec((1, MLA_HEADS, tm, LANES), lambda bi, i: (bi, 0, i, 0)),
            pl.BlockSpec((1, MLA_HEADS, V_ROWS, tm), lambda bi, i: (bi, 0, 0, i)),
        ],
        out_shape=[
            jax.ShapeDtypeStruct((b, MLA_HEADS, LANES, s), MXU_DTYPE),
            jax.ShapeDtypeStruct((b, MLA_HEADS, s, LANES), MXU_DTYPE),
            jax.ShapeDtypeStruct((b, MLA_HEADS, V_ROWS, s), MXU_DTYPE),
        ],
        compiler_params=_params(("parallel", "parallel")),
        name="mla_prep",
    )(proj, tail, cqg, ckvg, wqt, wk, wvt, qgt, kg, cost, sint, cos, sin)


def _moba_rot(x, lane):
    half = MOBA_DH // 2
    return jnp.where(lane % MOBA_DH < half, -pltpu.roll(x, LANES - half, 1), pltpu.roll(x, half, 1))


def _pair_rms(x, gain, lo_mask):
    sq = x * x
    ss_lo = jnp.sum(jnp.where(lo_mask, sq, 0.0), axis=-1, keepdims=True)
    ss_hi = jnp.sum(jnp.where(lo_mask, 0.0, sq), axis=-1, keepdims=True)
    r = jnp.where(lo_mask, lax.rsqrt(ss_lo * (1.0 / MOBA_DH) + EPS), lax.rsqrt(ss_hi * (1.0 / MOBA_DH) + EPS))
    return x * r * gain


def _select_bias(gate, blk, blk_f, j):
    g = jnp.where(blk < j, gate, SENTINEL)
    allowed = blk == j
    for _ in range(MOBA_TOPK):
        m = jnp.max(g, axis=0, keepdims=True)
        first = jnp.min(jnp.where(g == m, blk_f, 1e9), axis=0, keepdims=True)
        pick = (blk_f == first) & (m > SENTINEL)
        allowed = allowed | pick
        g = jnp.where(pick, SENTINEL, g)
    return jnp.where(allowed, 0.0, NEG_INF)


def _moba_prep_body(qkv_ref, qgt_ref, kg_ref, cost_ref, sint_ref, cos_ref, sin_ref, q_ref, k_ref, v_ref, kmt_scr,
                    *, scale):
    j = pl.program_id(1)

    @pl.when(j == 0)
    def _():
        kmt_scr[...] = jnp.zeros_like(kmt_scr)

    tq = qkv_ref.shape[0]
    hw = MOBA_HEADS * MOBA_DH
    dh, half, nblk = MOBA_DH, MOBA_DH // 2, MOBA_MAX_BLOCKS

    vt = qkv_ref[:, 2 * hw:].astype(F32).T
    for h in range(MOBA_HEADS):
        _value_rows(v_ref, h, vt[h * dh:(h + 1) * dh, :])

    qt_all = qkv_ref[:, :hw].astype(F32).T
    qgt = _lane_tile(qgt_ref[...], tq // LANES)
    cost = cost_ref[...]
    sint = sint_ref[...]
    blk = lax.broadcasted_iota(jnp.int32, (nblk, tq), 0)
    blk_f = blk.astype(F32)
    zeros = jnp.zeros((nblk, tq), F32)

    lane = lax.broadcasted_iota(jnp.int32, (tq, LANES), 1)
    lo_mask = lane < dh
    cos = cos_ref[...]
    sin = sin_ref[...]
    row_lane = lax.broadcasted_iota(jnp.int32, (1, LANES), 1)
    slot = lax.broadcasted_iota(jnp.int32, (LANES, LANES), 0)
    for p in range(MOBA_HEADS // 2):
        x = qt_all[p * LANES:(p + 1) * LANES, :]
        sq = x * x
        r_e = lax.rsqrt(jnp.sum(sq[:dh], axis=0, keepdims=True) * (1.0 / dh) + EPS)
        r_o = lax.rsqrt(jnp.sum(sq[dh:], axis=0, keepdims=True) * (1.0 / dh) + EPS)
        qn = jnp.concatenate([x[:dh] * r_e, x[dh:] * r_o], axis=0) * qgt
        rot = jnp.concatenate([-qn[half:dh], qn[:half], -qn[dh + half:], qn[dh:dh + half]], axis=0)
        q = qn * cost + rot * sint
        gate = _dot(kmt_scr[p], q, precision=HIGHEST)
        bias_e = _select_bias(gate[dh:dh + nblk], blk, blk_f, j)
        bias_o = _select_bias(gate[:nblk], blk, blk_f, j)
        qs = q * scale
        q_ref[0, 2 * p] = jnp.concatenate([qs[:dh], bias_e, zeros], axis=0).astype(q_ref.dtype)
        q_ref[0, 2 * p + 1] = jnp.concatenate([bias_o, zeros, qs[dh:]], axis=0).astype(q_ref.dtype)

        kn = _pair_rms(qkv_ref[:, hw + p * LANES:hw + (p + 1) * LANES].astype(F32), kg_ref[...], lo_mask)
        k = kn * cos + _moba_rot(kn, lane) * sin
        k_ref[0, 2 * p] = jnp.where(lo_mask, k, jnp.where(lane == dh + j, 1.0, 0.0)).astype(k_ref.dtype)
        k_ref[0, 2 * p + 1] = jnp.where(lo_mask, jnp.where(lane == j, 1.0, 0.0), k).astype(k_ref.dtype)
        km = jnp.sum(k, axis=0, keepdims=True) * (1.0 / MOBA_BLOCK)
        new_rows = jnp.where(slot == dh + j, jnp.where(row_lane < dh, km, 0.0),
                             jnp.where(row_lane < dh, 0.0, km))
        kmt_scr[p] = jnp.where((slot == dh + j) | (slot == j), new_rows, kmt_scr[p])


def _moba_prep(proj, qgt, kg, cost, sint, cos, sin, *, b, s):
    tq = MOBA_BLOCK
    nt = s // tq
    scale = (MOBA_DH ** -0.5) * LOG2E
    cblk = COL_MOBA // (3 * MOBA_HEADS * MOBA_DH)
    return pl.pallas_call(
        functools.partial(_moba_prep_body, scale=scale),
        grid=(b, nt),
        in_specs=[
            pl.BlockSpec((tq, 3 * MOBA_HEADS * MOBA_DH), lambda bi, j: (bi * nt + j, cblk)),
            pl.BlockSpec((LANES, LANES), lambda bi, j: (0, 0)),
            pl.BlockSpec((1, LANES), lambda bi, j: (0, 0)),
            pl.BlockSpec((LANES, tq), lambda bi, j: (0, j)),
            pl.BlockSpec((LANES, tq), lambda bi, j: (0, j)),
            pl.BlockSpec((tq, LANES), lambda bi, j: (j, 0)),
            pl.BlockSpec((tq, LANES), lambda bi, j: (j, 0)),
        ],
        out_specs=[
            pl.BlockSpec((1, MOBA_HEADS, LANES, tq), lambda bi, j: (bi, 0, 0, j)),
            pl.BlockSpec((1, MOBA_HEADS, tq, LANES), lambda bi, j: (bi, 0, j, 0)),
            pl.BlockSpec((1, MOBA_HEADS, V_ROWS, tq), lambda bi, j: (bi, 0, 0, j)),
        ],
        out_shape=[
            jax.ShapeDtypeStruct((b, MOBA_HEADS, LANES, s), MXU_DTYPE),
            jax.ShapeDtypeStruct((b, MOBA_HEADS, s, LANES), MXU_DTYPE),
            jax.ShapeDtypeStruct((b, MOBA_HEADS, V_ROWS, s), MXU_DTYPE),
        ],
        scratch_shapes=[pltpu.VMEM((MOBA_HEADS // 2, LANES, LANES), F32)],
        compiler_params=_params(("parallel", "arbitrary")),
        name="moba_prep",
    )(proj, qgt, kg, cost, sint, cos, sin)


def _flash_body(qt_ref, k_ref, vt_ref, o_ref, buf0, buf1, *, tq):
    j = pl.program_id(2)

    def scores(qts, t, buf, hh, diagonal=False):
        ks = pl.ds(pl.multiple_of(t * tq, tq), tq)
        s = _dot(k_ref[0, hh, ks, :], qts[hh])
        if diagonal:
            key = lax.broadcasted_iota(jnp.int32, s.shape, 0)
            qry = lax.broadcasted_iota(jnp.int32, s.shape, 1)
            s = jnp.where(key <= qry, s, NEG_INF)
        buf[hh] = s
        return jnp.max(s, axis=0, keepdims=True)

    def consume(t, buf, hh, smax, stat):
        ks = pl.ds(pl.multiple_of(t * tq, tq), tq)
        m, acc = stat
        m_new = jnp.maximum(m, smax)
        p = jnp.exp2(buf[hh] - m_new)
        return m_new, jnp.exp2(m - m_new) * acc + _dot(vt_ref[0, hh, :, ks], p.astype(MXU_DTYPE))

    def step(qts, t_next, buf_next, t_cur, buf_cur, carry):
        smax, stats = carry
        new_smax, new_stats = [], []
        for hh in range(2):
            new_smax.append(scores(qts, t_next, buf_next, hh))
            new_stats.append(consume(t_cur, buf_cur, hh, smax[hh], stats[hh]))
        return tuple(new_smax), tuple(new_stats)

    def run(qts, i, first_cur, t_first, carry, bufs):
        n_steps = i - t_first

        def quad(u, carry):
            t0 = t_first + 4 * u
            carry = step(qts, t0, bufs[0], jnp.where(u == 0, first_cur, t0 - 1), bufs[1], carry)
            carry = step(qts, t0 + 1, bufs[1], t0, bufs[0], carry)
            carry = step(qts, t0 + 2, bufs[0], t0 + 1, bufs[1], carry)
            return step(qts, t0 + 3, bufs[1], t0 + 2, bufs[0], carry)

        carry = lax.fori_loop(0, n_steps // 4, quad, carry)
        t_pair = t_first + 4 * (n_steps // 4)

        def pair(u, carry):
            carry = step(qts, t_pair, bufs[0], jnp.where(t_pair == t_first, first_cur, t_pair - 1), bufs[1], carry)
            return step(qts, t_pair + 1, bufs[1], t_pair, bufs[0], carry)

        return lax.fori_loop(0, (n_steps % 4) // 2, pair, carry)

    def finish(t_last, buf, carry, rows):
        smax, stats = carry
        out = [consume(t_last, buf, hh, smax[hh], stats[hh])[1] for hh in range(2)]
        o_t = jnp.concatenate([a[:MLA_V] / a[MLA_V:MLA_V + 1] for a in out], axis=0)
        o_ref[0, rows, :] = o_t.T.astype(o_ref.dtype)

    init = ((jnp.full((1, tq), NEG_INF, F32), jnp.zeros((V_ROWS, tq), F32)),) * 2
    q_even = (qt_ref[0, 0, :, :tq], qt_ref[0, 1, :, :tq])
    q_odd = (qt_ref[0, 0, :, tq:], qt_ref[0, 1, :, tq:])
    i_even, i_odd = 2 * j, 2 * j + 1

    smax = tuple(scores(q_even, i_even, buf1, hh, True) for hh in range(2))
    carry = run(q_even, i_even, i_even, 0, (smax, init), (buf0, buf1))
    smax_odd = tuple(scores(q_odd, i_odd, buf0, hh, True) for hh in range(2))
    finish(jnp.where(j == 0, 0, i_even - 1), buf1, carry, slice(0, tq))
    carry = step(q_odd, 0, buf1, i_odd, buf0, (smax_odd, init))
    carry = run(q_odd, i_odd, 0, 1, carry, (buf0, buf1))
    finish(i_odd - 1, buf1, carry, slice(tq, 2 * tq))


def _flash(qt, k, vt, *, tq):
    b, h, s, _ = k.shape
    return pl.pallas_call(
        functools.partial(_flash_body, tq=tq),
        grid=(b, h // 2, s // (2 * tq)),
        in_specs=[
            pl.BlockSpec((1, 2, LANES, 2 * tq), lambda bi, p, j: (bi, p, 0, j)),
            pl.BlockSpec((1, 2, s, LANES), lambda bi, p, j: (bi, p, 0, 0)),
            pl.BlockSpec((1, 2, V_ROWS, s), lambda bi, p, j: (bi, p, 0, 0)),
        ],
        out_specs=pl.BlockSpec((1, 2 * tq, LANES), lambda bi, p, j: (bi, j, p)),
        out_shape=jax.ShapeDtypeStruct((b, s, MIX_WIDTH), MXU_DTYPE),
        scratch_shapes=[pltpu.VMEM((2, tq, tq), F32), pltpu.VMEM((2, tq, tq), F32)],
        compiler_params=_params(("parallel", "parallel", "arbitrary")),
        name="flash",
    )(qt, k, vt)


def _hi_lo(x):
    hi = x.astype(MXU_DTYPE)
    return hi, (x - hi.astype(F32)).astype(MXU_DTYPE)


def _unit_lower_solve(a_list, rhs_list, eye):
    c = eye.shape[0]
    n = range(len(a_list))
    t = [eye - a for a in a_list]
    pk = []
    for a in a_list:
        pwm = (-a).astype(MXU_DTYPE)
        pk.append(_dot(pwm, pwm))
    n_sq = int(np.log2(c)) - 1
    for it in range(n_sq):
        if it < n_sq - 1:
            both = [_dot(jnp.concatenate([t[j], pk[j]], axis=0).astype(MXU_DTYPE), pk[j].astype(MXU_DTYPE))
                    for j in n]
            t = [t[j] + both[j][:c] for j in n]
            pk = [both[j][c:] for j in n]
        else:
            t = [t[j] + _dot(t[j].astype(MXU_DTYPE), pk[j].astype(MXU_DTYPE)) for j in n]
    tm = [x.astype(MXU_DTYPE) for x in t]
    x1 = [_dot(tm[j], rhs_list[j].astype(MXU_DTYPE)) for j in n]
    a_split = [_hi_lo(a) for a in a_list]
    x_split = [_hi_lo(x) for x in x1]
    ax = [_dot(jnp.concatenate(a_split[j], axis=0), x_split[j][0]) for j in n]
    ax_lo = [_dot(a_split[j][0], x_split[j][1]) for j in n]
    res = [rhs_list[j] - x1[j] - (ax[j][:c] + ax[j][c:] + ax_lo[j]) for j in n]
    return [x1[j] + _dot(tm[j], res[j].astype(MXU_DTYPE)) for j in n]


def _gdn_prep_body(qkv_ref, halo_ref, ba_ref, bat_ref, conv_ref, alog_ref, dtb_ref, alogt_ref, dtbt_ref,
                   u_ref, w_ref, qd_ref, kd_ref, qk_ref, egl_ref, xe_scr, y_scr, bg_scr,
                   *, tiles_per_seq, nc):
    i = pl.program_id(0)
    tm = qkv_ref.shape[0]
    c = GDN_CHUNK
    hk = GDN_HEADS * GDN_DK

    xe_scr[pl.ds(0, GDN_HALO), :] = jnp.where(i % tiles_per_seq == 0, 0.0, halo_ref[...].astype(F32))
    xe_scr[pl.ds(GDN_HALO, tm), :] = qkv_ref[...].astype(F32)
    cw = conv_ref[...]
    xe = xe_scr[...]
    y = xe[GDN_HALO:] * cw[GDN_CONV - 1:GDN_CONV]
    for back in range(1, GDN_CONV):
        y = y + pltpu.roll(xe, back, 0)[GDN_HALO:] * cw[GDN_CONV - 1 - back:GDN_CONV - back]
    y_scr[...] = _silu(y)

    ba = ba_ref[...]
    lane = lax.broadcasted_iota(jnp.int32, ba.shape, 1)
    g_col = -jnp.exp(alog_ref[...]) * _softplus(ba + dtb_ref[...])
    bg_scr[...] = jnp.where(lane < GDN_HEADS, jax.nn.sigmoid(ba), g_col)

    ri = lax.broadcasted_iota(jnp.int32, (c, c), 0)
    ci = lax.broadcasted_iota(jnp.int32, (c, c), 1)
    tril = ci <= ri
    strict = ci < ri
    eye = jnp.where(ci == ri, 1.0, 0.0)
    lower_ones = jnp.where(tril, 1.0, 0.0)
    upper_ones = jnp.where(ci >= ri, 1.0, 0.0)

    def chunk_group(grp, _):
        chs = [grp * GDN_GROUP + j for j in range(GDN_GROUP)]
        rows = [pl.ds(pl.multiple_of(ch * c, c), c) for ch in chs]
        bg = [bg_scr[r, :] for r in rows]
        gcum_col = [_dot(lower_ones, x, precision=HIGHEST) for x in bg]
        g_row = [-jnp.exp(alogt_ref[...]) * _softplus(bat_ref[ch] + dtbt_ref[...]) for ch in chs]
        gcum_row = [_dot(x, upper_ones, precision=HIGHEST) for x in g_row]
        probs = [(j, h) for j in range(GDN_GROUP) for h in range(GDN_HEADS)]
        n = range(len(probs))
        sls = [slice(h * GDN_DK, (h + 1) * GDN_DK) for _, h in probs]
        qh = [y_scr[rows[j], sls[i]] for i, (j, h) in enumerate(probs)]
        kh = [y_scr[rows[j], pl.ds(hk + h * GDN_DK, GDN_DK)] for j, h in probs]
        vh = [y_scr[rows[j], pl.ds(2 * hk + h * GDN_DV, GDN_DV)] for j, h in probs]
        qn = [x * lax.rsqrt(jnp.sum(x * x, axis=-1, keepdims=True) + EPS) * (GDN_DK ** -0.5) for x in qh]
        kn = [x * lax.rsqrt(jnp.sum(x * x, axis=-1, keepdims=True) + EPS) for x in kh]
        beta = [bg[j][:, h:h + 1] for j, h in probs]
        gc = [gcum_col[j][:, GDN_HEADS + h:GDN_HEADS + h + 1] for j, h in probs]
        gr = [gcum_row[j][GDN_HEADS + h:GDN_HEADS + h + 1, :] for j, h in probs]
        kb = [kn[i] * beta[i] for i in n]
        knm = [x.astype(MXU_DTYPE) for x in kn]
        dec = [jnp.exp(jnp.where(tril, gc[i] - gr[i], NEG_INF)) for i in n]
        skk = [_dot_nt(kb[i].astype(MXU_DTYPE), knm[i]) for i in n]
        sqk = [_dot_nt(qn[i].astype(MXU_DTYPE), knm[i]) for i in n]
        a = [jnp.where(strict, skk[i] * dec[i], 0.0) for i in n]
        eg = [jnp.exp(x) for x in gc]
        rhs = [jnp.concatenate([vh[i] * beta[i], kb[i] * eg[i]], axis=1) for i in n]
        sol = _unit_lower_solve(a, rhs, eye)
        for i, (j, h) in enumerate(probs):
            sl, r = sls[i], rows[j]
            gl = gc[i][c - 1:c, :]
            u_ref[r, sl] = sol[i][:, :GDN_DV]
            w_ref[r, sl] = sol[i][:, GDN_DV:].astype(w_ref.dtype)
            qd_ref[r, sl] = (qn[i] * eg[i]).astype(qd_ref.dtype)
            kd_ref[r, sl] = (kn[i] * jnp.exp(gl - gc[i])).astype(kd_ref.dtype)
            qk_ref[r, pl.ds(h * LANES, c)] = (sqk[i] * dec[i]).astype(qk_ref.dtype)
            qk_ref[r, pl.ds(h * LANES + c, LANES - c)] = jnp.zeros((c, LANES - c), qk_ref.dtype)
            egl_ref[chs[j], :, sl] = jnp.broadcast_to(jnp.exp(gl), (1, GDN_DV))
        return 0

    lax.fori_loop(0, nc // GDN_GROUP, chunk_group, 0)


def _gdn_prep(proj, tail, bat, conv, alog, dtb, alogt, dtbt, *, s, tm):
    t = proj.shape[0]
    nc = tm // GDN_CHUNK
    hk = GDN_HEADS * GDN_DK
    width = 3 * hk
    cblk = COL_GDN // width
    full = lambda shape: pl.BlockSpec(shape, lambda i: (0,) * len(shape))
    row = lambda w: pl.BlockSpec((tm, w), lambda i: (i, 0))
    return pl.pallas_call(
        functools.partial(_gdn_prep_body, tiles_per_seq=s // tm, nc=nc),
        grid=(t // tm,),
        in_specs=[
            pl.BlockSpec((tm, width), lambda i: (i, cblk)),
            pl.BlockSpec((GDN_HALO, width), lambda i: (jnp.maximum(i * (tm // GDN_HALO) - 1, 0), cblk)),
            pl.BlockSpec((tm, LANES), lambda i: (i, 0)),
            pl.BlockSpec((nc, 8, GDN_CHUNK), lambda i: (i, 0, 0)),
            full((GDN_CONV, width)), full((1, LANES)), full((1, LANES)), full((8, 1)), full((8, 1)),
        ],
        out_specs=[row(hk), row(hk), row(hk), row(hk), row(GDN_HEADS * LANES),
                   pl.BlockSpec((nc, 1, hk), lambda i: (i, 0, 0))],
        out_shape=[
            jax.ShapeDtypeStruct((t, hk), F32),
            jax.ShapeDtypeStruct((t, hk), MXU_DTYPE),
            jax.ShapeDtypeStruct((t, hk), MXU_DTYPE),
            jax.ShapeDtypeStruct((t, hk), MXU_DTYPE),
            jax.ShapeDtypeStruct((t, GDN_HEADS * LANES), MXU_DTYPE),
            jax.ShapeDtypeStruct((t // GDN_CHUNK, 1, hk), F32),
        ],
        scratch_shapes=[
            pltpu.VMEM((tm + GDN_HALO, width), F32),
            pltpu.VMEM((tm, width), F32),
            pltpu.VMEM((tm, LANES), F32),
        ],
        compiler_params=_params(("parallel",)),
        name="gdn_prep",
    )(proj, proj, tail, bat, conv, alog, dtb, alogt, dtbt)


def _gdn_scan_body(u_ref, w_ref, qd_ref, kd_ref, qk_ref, egl_ref, z_ref, og_ref, o_ref, s_scr, *, nb, nc):
    @pl.when(pl.program_id(0) == 0)
    def _():
        s_scr[...] = jnp.zeros_like(s_scr)

    c = GDN_CHUNK
    og = og_ref[...]

    def chunk(ch, _):
        rows = pl.ds(pl.multiple_of(ch * c, c), c)
        probs = [(bi, h) for bi in range(nb) for h in range(GDN_HEADS)]
        n = range(len(probs))
        sls = [slice(h * GDN_DV, (h + 1) * GDN_DV) for _, h in probs]
        st = [s_scr[bi * GDN_HEADS + h] for bi, h in probs]
        stm = [x.astype(MXU_DTYPE) for x in st]
        wq = [jnp.concatenate([w_ref[bi, rows, sls[i]], qd_ref[bi, rows, sls[i]]], axis=0)
              for i, (bi, h) in enumerate(probs)]
        r1 = [_dot(wq[i], stm[i]) for i in n]
        vm = [(u_ref[bi, rows, sls[i]] - r1[i][:c]).astype(MXU_DTYPE) for i, (bi, h) in enumerate(probs)]
        o2 = [_dot(qk_ref[bi, rows, pl.ds(h * LANES, c)], vm[i]) for i, (bi, h) in enumerate(probs)]
        sd = [_dot_tn(kd_ref[bi, rows, sls[i]], vm[i]) for i, (bi, h) in enumerate(probs)]
        for i, (bi, h) in enumerate(probs):
            s_scr[bi * GDN_HEADS + h] = st[i] * egl_ref[bi, ch, :, sls[i]] + sd[i]
            o = r1[i][c:] + o2[i]
            o_ref[bi, rows, sls[i]] = (_rms(o, og, GDN_DV)
                                       * _silu(z_ref[bi, rows, sls[i]].astype(F32))).astype(o_ref.dtype)
        return 0

    lax.fori_loop(0, nc, chunk, 0)


def _gdn_scan(u, w, qd, kd, qk, egl, proj3, og, *, tt):
    b, s, hk = u.shape
    nc = tt // GDN_CHUNK
    blk = lambda width: pl.BlockSpec((b, tt, width), lambda i: (0, i, 0))
    return pl.pallas_call(
        functools.partial(_gdn_scan_body, nb=b, nc=nc),
        grid=(s // tt,),
        in_specs=[
            blk(hk), blk(hk), blk(hk), blk(hk), blk(GDN_HEADS * LANES),
            pl.BlockSpec((b, nc, 1, hk), lambda i: (0, i, 0, 0)),
            pl.BlockSpec((b, tt, hk), lambda i: (0, i, COL_Z // hk)),
            pl.BlockSpec((1, GDN_DV), lambda i: (0, 0)),
        ],
        out_specs=blk(hk),
        out_shape=jax.ShapeDtypeStruct((b, s, hk), MXU_DTYPE),
        scratch_shapes=[pltpu.VMEM((b * GDN_HEADS, GDN_DK, GDN_DV), F32)],
        compiler_params=_params(("arbitrary",)),
        name="gdn_scan",
    )(u, w, qd, kd, qk, egl, proj3, og)


def _merge_body(x_ref, om_ref, og_ref, ob_ref, gl_ref, wb_ref, wo_ref, o_ref):
    mixed = None
    for n, br in enumerate((om_ref, og_ref, ob_ref)):
        up = _dot(br[...], wb_ref[n].astype(MXU_DTYPE))
        term = jax.nn.sigmoid(gl_ref[:, n * D_MODEL:(n + 1) * D_MODEL].astype(F32)) * up
        mixed = term if mixed is None else mixed + term
    o_ref[...] = x_ref[...] + _dot(mixed.astype(MXU_DTYPE), wo_ref[...].astype(MXU_DTYPE))


def _merge(x, o_mla, o_gdn, o_moba, proj, wb, wo, *, tm):
    t, d = x.shape
    row = lambda w: pl.BlockSpec((tm, w), lambda i: (i, 0))
    return pl.pallas_call(
        _merge_body,
        grid=(t // tm,),
        in_specs=[
            row(d), row(MIX_WIDTH), row(MIX_WIDTH), row(MIX_WIDTH), row(3 * d),
            _resident((3, MIX_WIDTH, d)),
            _resident((d, d)),
        ],
        out_specs=row(d),
        out_shape=jax.ShapeDtypeStruct((t, d), F32),
        compiler_params=_params(("parallel",)),
        name="merge",
    )(x, o_mla, o_gdn, o_moba, proj, wb, wo)


def _rope_tables(s):
    pos = jnp.arange(s, dtype=jnp.int32).astype(F32)[:, None]

    def cs(d):
        half = d // 2
        inv_freq = ROPE_THETA ** (-jnp.arange(half, dtype=F32) * 2.0 / d)
        ang = pos * inv_freq[None, :]
        return jnp.cos(ang), jnp.sin(ang)

    c, sn = cs(MLA_ROPE)
    pad = LANES - MLA_QK
    mla_cos = jnp.concatenate([jnp.ones((s, MLA_NOPE), F32), c, c, jnp.zeros((s, pad), F32)], axis=1)
    mla_sin = jnp.concatenate([jnp.zeros((s, MLA_NOPE), F32), sn, sn, jnp.zeros((s, pad), F32)], axis=1)
    c, sn = cs(MOBA_DH)
    moba_cos, moba_sin = jnp.tile(c, (1, 4)), jnp.tile(sn, (1, 4))
    return mla_cos, mla_sin, mla_cos.T, mla_sin.T, moba_cos, moba_sin, moba_cos.T, moba_sin.T


def _proj_weights(w_in):
    d = w_in.shape[0]
    o = np.cumsum((0, MLA_Q_RANK, MLA_KV_RANK, MLA_ROPE, 512, 512, 512, GDN_HEADS, GDN_HEADS, 512, 1536, 3 * D_MODEL))
    cut = lambda a, b: w_in[:, o[a]:o[b]].astype(MXU_DTYPE)
    ba = cut(6, 8)
    tail = jnp.zeros((d, LANES), MXU_DTYPE).at[:, :2 * GDN_HEADS].set(ba).at[:, MLA_NOPE:MLA_QK].set(cut(2, 3))
    groups = [cut(10, 11), cut(3, 6), cut(9, 10), cut(8, 9), cut(0, 1), cut(1, 2), tail]
    return groups, ba.T


def _lane_pad(v, n=LANES):
    return jnp.pad(v, (0, n - v.shape[0]))[None, :]


def kernel(x, ffa_norm, ffa_w_in, ffa_w_out, mix_norm, w_in, mla_cq_norm, mla_ckv_norm, mla_w_uq, mla_w_ukv, mla_q_norm, mla_k_norm, gdn_conv, gdn_a_log, gdn_dt_bias, gdn_out_norm, moba_q_norm, moba_k_norm, w_branch, w_out, ffb_norm, ffb_w_in, ffb_w_out):
    b, s, d = x.shape
    t = b * s
    assert d == D_MODEL and s % 1024 == 0 and s // MOBA_BLOCK <= MOBA_MAX_BLOCKS
    depth = ffa_norm.shape[0]
    tm = 512
    mla_cos, mla_sin, mla_cost, mla_sint, moba_cos, moba_sin, moba_cost, moba_sint = _rope_tables(s)
    lane_bcast = lambda v: jnp.broadcast_to(v[:, None], (LANES, LANES))
    x = x.reshape(t, d)
    for l in range(depth):
        x = _ffn(x, ffa_norm[l][None, :], ffa_w_in[l], ffa_w_out[l],
                 tm=tm, tf=256)

        w_proj, w_bat = _proj_weights(w_in[l])
        proj, tail, bat = _inproj(x, mix_norm[l][None, :], w_proj, w_bat, tm=tm, tn=512)

        wq = jnp.pad(mla_w_uq[l].reshape(MLA_Q_RANK, MLA_HEADS, MLA_QK),
                     ((0, 0), (0, 0), (0, LANES - MLA_QK))).reshape(MLA_Q_RANK, MLA_HEADS * LANES)
        wkv = mla_w_ukv[l].reshape(MLA_KV_RANK, MLA_HEADS, MLA_NOPE + MLA_V)
        wk = jnp.pad(wkv[:, :, :MLA_NOPE], ((0, 0), (0, 0), (0, LANES - MLA_NOPE))).reshape(MLA_KV_RANK, MLA_HEADS * LANES)
        wv = wkv[:, :, MLA_NOPE:].reshape(MLA_KV_RANK, MLA_HEADS * MLA_V).T
        q, k, v = _mla_prep(proj, tail, mla_cq_norm[l][None, :], mla_ckv_norm[l][None, :], wq.T.astype(MXU_DTYPE),
                            wk.astype(MXU_DTYPE), wv.astype(MXU_DTYPE), lane_bcast(_lane_pad(mla_q_norm[l])[0]),
                            _lane_pad(mla_k_norm[l]), mla_cost, mla_sint, mla_cos, mla_sin, b=b, s=s, tm=tm)
        o_mla = _flash(q, k, v, tq=512)

        bat_chunks = bat.reshape(8, t // GDN_CHUNK, GDN_CHUNK).transpose(1, 0, 2)
        head_pad = lambda v: jnp.pad(v, (GDN_HEADS, LANES - 2 * GDN_HEADS))[None, :]
        head_col = lambda v: jnp.pad(v, (GDN_HEADS, 0))[:, None]
        u, w, qd, kd, qk, egl = _gdn_prep(proj, tail, bat_chunks, gdn_conv[l], head_pad(gdn_a_log[l]),
                                          head_pad(gdn_dt_bias[l]), head_col(gdn_a_log[l]),
                                          head_col(gdn_dt_bias[l]), s=s, tm=tm)
        r3 = lambda a: a.reshape(b, s, a.shape[-1])
        o_gdn = _gdn_scan(r3(u), r3(w), r3(qd), r3(kd), r3(qk), egl.reshape(b, s // GDN_CHUNK, 1, -1),
                          r3(proj), gdn_out_norm[l][None, :], tt=tm)

        q, k, v = _moba_prep(proj, lane_bcast(jnp.tile(moba_q_norm[l], 2)), jnp.tile(moba_k_norm[l], 2)[None, :],
                             moba_cost, moba_sint, moba_cos, moba_sin, b=b, s=s)
        o_moba = _flash(q, k, v, tq=512)

        x = _merge(x, o_mla.reshape(t, -1), o_gdn.reshape(t, -1), o_moba.reshape(t, -1), proj,
                   w_branch[l], w_out[l], tm=tm)

        x = _ffn(x, ffb_norm[l][None, :], ffb_w_in[l], ffb_w_out[l],
                 tm=tm, tf=256)
    return x.reshape(b, s, d)
```

```python
import functools

import numpy as np
import jax
import jax.numpy as jnp
from jax import lax
from jax.experimental import pallas as pl
from jax.experimental.pallas import tpu as pltpu

F32 = jnp.float32
BF16 = jnp.bfloat16
MXU_DTYPE = BF16
HIGHEST = lax.Precision.HIGHEST

EPS = 1e-6
ROPE_THETA = 10000.0
NEG_INF = -1e30
SENTINEL = -3e38
LOG2E = 1.4426950408889634

D_MODEL = 1024
D_FF = 2816
MIX_WIDTH = 512
LANES = 128

MLA_HEADS = 8
MLA_Q_RANK = 256
MLA_KV_RANK = 128
MLA_NOPE = 64
MLA_ROPE = 32
MLA_V = 64
MLA_QK = MLA_NOPE + MLA_ROPE

GDN_HEADS = 4
GDN_DK = 128
GDN_DV = 128
GDN_CONV = 4
GDN_CHUNK = 64
GDN_HALO = 16
GDN_GROUP = 4

MOBA_HEADS = 8
MOBA_DH = 64
MOBA_BLOCK = 256
MOBA_TOPK = 3
MOBA_MAX_BLOCKS = 32
V_ROWS = 80

COL_GATE = 0
COL_GDN = 3072
COL_MOBA = 4608
COL_Z = 6144
COL_MLA = 6656
N_PROJ = 7168

VMEM_LIMIT = 56 * 1024 * 1024


def _dot(a, b, precision=None):
    return jnp.dot(a, b, preferred_element_type=F32, precision=precision)


def _dot_nt(a, b, precision=None):
    return lax.dot_general(a, b, (((1,), (1,)), ((), ())), preferred_element_type=F32,
                           precision=precision)


def _dot_tn(a, b, precision=None):
    return lax.dot_general(a, b, (((0,), (0,)), ((), ())), preferred_element_type=F32,
                           precision=precision)


def _rms(x, gain, n):
    ms = jnp.sum(x * x, axis=-1, keepdims=True) * (1.0 / n)
    return x * lax.rsqrt(ms + EPS) * gain


def _silu(x):
    return x * jax.nn.sigmoid(x)


def _softplus(x):
    return jnp.maximum(x, 0.0) + jnp.log1p(jnp.exp(-jnp.abs(x)))


def _params(sem):
    return pltpu.CompilerParams(dimension_semantics=sem, vmem_limit_bytes=VMEM_LIMIT)


def _ffn_body(x_ref, g_ref, wi_ref, wo_ref, o_ref, act_scr, *, tf):
    x = x_ref[...]
    h = _rms(x, g_ref[...], D_MODEL).astype(MXU_DTYPE)
    for k in range(D_FF // tf):
        gate = _dot(h, wi_ref[:, k * tf:(k + 1) * tf].astype(MXU_DTYPE))
        up = _dot(h, wi_ref[:, D_FF + k * tf:D_FF + (k + 1) * tf].astype(MXU_DTYPE))
        act_scr[:, k * tf:(k + 1) * tf] = (_silu(gate) * up).astype(MXU_DTYPE)
    o_ref[...] = x + 0.5 * _dot(act_scr[...], wo_ref[...].astype(MXU_DTYPE))


def _resident(shape, layer=None):
    if layer is None:
        return pl.BlockSpec(shape, lambda *_: (0,) * len(shape), pipeline_mode=pl.Buffered(1))
    return pl.BlockSpec((None,) + tuple(shape), lambda *_: (layer,) + (0,) * len(shape),
                        pipeline_mode=pl.Buffered(1))


def _ffn(x, gain, w_in, w_out, layer, *, tm, tf):
    t, d = x.shape
    return pl.pallas_call(
        functools.partial(_ffn_body, tf=tf),
        grid=(t // tm,),
        in_specs=[
            pl.BlockSpec((tm, d), lambda i: (i, 0)),
            _resident((1, d)),
            _resident((d, 2 * D_FF), layer),
            _resident((D_FF, d), layer),
        ],
        out_specs=pl.BlockSpec((tm, d), lambda i: (i, 0)),
        out_shape=jax.ShapeDtypeStruct((t, d), F32),
        scratch_shapes=[pltpu.VMEM((tm, D_FF), MXU_DTYPE)],
        compiler_params=_params(("parallel",)),
        name="ffn",
    )(x, gain, w_in, w_out)


def _inproj_body(x_ref, g_ref, *refs, widths, tn):
    w_refs, wbat_ref = refs[:len(widths)], refs[len(widths)]
    o_ref, tail_ref, bat_ref = refs[len(widths) + 1:]
    h = _rms(x_ref[...], g_ref[...], D_MODEL).astype(MXU_DTYPE)
    bat_ref[...] = _dot_nt(wbat_ref[...], h)
    off = 0
    for w_ref, width in zip(w_refs, widths):
        for a in range(0, width, tn):
            n = min(tn, width - a)
            r = _dot(h, w_ref[:, a:a + n])
            o_ref[:, off + a:off + a + n] = r.astype(o_ref.dtype)
        off += width
    tail_ref[...] = r


def _inproj(x, gain, w_groups, wbat, *, tm, tn):
    t, d = x.shape
    widths = tuple(w.shape[1] for w in w_groups)
    assert sum(widths) == N_PROJ and widths[-1] == LANES
    return pl.pallas_call(
        functools.partial(_inproj_body, widths=widths, tn=tn),
        grid=(t // tm,),
        in_specs=[pl.BlockSpec((tm, d), lambda i: (i, 0)), _resident((1, d))]
                 + [_resident((d, w)) for w in widths] + [_resident((8, d))],
        out_specs=[
            pl.BlockSpec((tm, N_PROJ), lambda i: (i, 0)),
            pl.BlockSpec((tm, LANES), lambda i: (i, 0)),
            pl.BlockSpec((8, tm), lambda i: (0, i)),
        ],
        out_shape=[jax.ShapeDtypeStruct((t, N_PROJ), MXU_DTYPE), jax.ShapeDtypeStruct((t, LANES), F32),
                   jax.ShapeDtypeStruct((8, t), F32)],
        compiler_params=_params(("parallel",)),
        name="inproj",
    )(x, gain, *w_groups, wbat)


def _mla_rot(x, lane):
    return jnp.where(lane < MLA_NOPE + MLA_ROPE // 2, -pltpu.roll(x, LANES - MLA_ROPE // 2, 1),
                     pltpu.roll(x, MLA_ROPE // 2, 1))


def _lane_tile(x, n):
    return x if n == 1 else jnp.concatenate([x] * n, axis=1)


def _value_rows(v_ref, h, vt_h):
    n = vt_h.shape[1]
    v_ref[0, h, 0:MLA_V, :] = vt_h.astype(v_ref.dtype)
    pad_rows = lax.broadcasted_iota(jnp.int32, (V_ROWS - MLA_V, n), 0)
    v_ref[0, h, MLA_V:V_ROWS, :] = jnp.where(pad_rows == 0, 1.0, 0.0).astype(v_ref.dtype)


def _mla_prep_body(lat_ref, tail_ref, cqg_ref, ckvg_ref, wqt_ref, wk_ref, wvt_ref, qgt_ref, kg_ref, cost_ref,
                   sint_ref, cos_ref, sin_ref, q_ref, k_ref, v_ref, *, scale):
    lat = lat_ref[...].astype(F32)
    tm = lat.shape[0]
    cqn = _rms(lat[:, :MLA_Q_RANK], cqg_ref[...], MLA_Q_RANK).astype(MXU_DTYPE)
    ckvn = _rms(lat[:, MLA_Q_RANK:MLA_Q_RANK + MLA_KV_RANK], ckvg_ref[...], MLA_KV_RANK).astype(MXU_DTYPE)

    qt_all = _dot_nt(wqt_ref[...], cqn)
    qgt = _lane_tile(qgt_ref[...], tm // LANES)
    cost = cost_ref[...]
    sint = sint_ref[...]
    half = MLA_ROPE // 2
    for h in range(MLA_HEADS):
        x = qt_all[h * LANES:(h + 1) * LANES, :]
        ms = jnp.sum(x * x, axis=0, keepdims=True) * (1.0 / MLA_QK)
        qn = x * lax.rsqrt(ms + EPS) * qgt
        rot = jnp.concatenate([jnp.zeros((MLA_NOPE, tm), F32), -qn[MLA_NOPE + half:MLA_QK],
                               qn[MLA_NOPE:MLA_NOPE + half], jnp.zeros((LANES - MLA_QK, tm), F32)], axis=0)
        q_ref[0, h] = ((qn * cost + rot * sint) * scale).astype(q_ref.dtype)

    lane = lax.broadcasted_iota(jnp.int32, (tm, LANES), 1)
    kr = jnp.where((lane >= MLA_NOPE) & (lane < MLA_QK), tail_ref[...], 0.0)
    kg = kg_ref[...]
    cos = cos_ref[...]
    rot_kr_sin = _mla_rot(kr * kg, lane) * sin_ref[...]
    k_all = _dot(ckvn, wk_ref[...])
    for h in range(MLA_HEADS):
        kh = k_all[:, h * LANES:(h + 1) * LANES] + kr
        r = lax.rsqrt(jnp.sum(kh * kh, axis=-1, keepdims=True) * (1.0 / MLA_QK) + EPS)
        k_ref[0, h] = (r * (kh * kg * cos + rot_kr_sin)).astype(k_ref.dtype)

    vt = _dot_nt(wvt_ref[...], ckvn)
    for h in range(MLA_HEADS):
        _value_rows(v_ref, h, vt[h * MLA_V:(h + 1) * MLA_V, :])


def _mla_prep(proj, tail, cqg, ckvg, wqt, wk, wvt, qgt, kg, cost, sint, cos, sin, *, b, s, tm):
    nt = s // tm
    scale = (MLA_QK ** -0.5) * LOG2E
    cblk = COL_MLA // 512
    full = lambda shape: pl.BlockSpec(shape, lambda bi, i: (0,) * len(shape))
    return pl.pallas_call(
        functools.partial(_mla_prep_body, scale=scale),
        grid=(b, nt),
        in_specs=[
            pl.BlockSpec((tm, 512), lambda bi, i: (bi * nt + i, cblk)),
            pl.BlockSpec((tm, LANES), lambda bi, i: (bi * nt + i, 0)),
            full((1, MLA_Q_RANK)), full((1, MLA_KV_RANK)),
            full((MLA_HEADS * LANES, MLA_Q_RANK)), full((MLA_KV_RANK, MLA_HEADS * LANES)),
            full((MIX_WIDTH, MLA_KV_RANK)), full((LANES, LANES)), full((1, LANES)),
            pl.BlockSpec((LANES, tm), lambda bi, i: (0, i)),
            pl.BlockSpec((LANES, tm), lambda bi, i: (0, i)),
            pl.BlockSpec((tm, LANES), lambda bi, i: (i, 0)),
            pl.BlockSpec((tm, LANES), lambda bi, i: (i, 0)),
        ],
        out_specs=[
            pl.BlockSpec((1, MLA_HEADS, LANES, tm), lambda bi, i: (bi, 0, 0, i)),
            pl.BlockSpec((1, MLA_HEADS, tm, LANES), lambda bi, i: (bi, 0, i, 0)),
            pl.BlockSpec((1, MLA_HEADS, V_ROWS, tm), lambda bi, i: (bi, 0, 0, i)),
        ],
        out_shape=[
            jax.ShapeDtypeStruct((b, MLA_HEADS, LANES, s), MXU_DTYPE),
            jax.ShapeDtypeStruct((b, MLA_HEADS, s, LANES), MXU_DTYPE),
            jax.ShapeDtypeStruct((b, MLA_HEADS, V_ROWS, s), MXU_DTYPE),
        ],
        compiler_params=_params(("parallel", "parallel")),
        name="mla_prep",
    )(proj, tail, cqg, ckvg, wqt, wk, wvt, qgt, kg, cost, sint, cos, sin)


def _moba_rot(x, lane):
    half = MOBA_DH // 2
    return jnp.where(lane % MOBA_DH < half, -pltpu.roll(x, LANES - half, 1), pltpu.roll(x, half, 1))


def _pair_rms(x, gain, lo_mask):
    sq = x * x
    ss_lo = jnp.sum(jnp.where(lo_mask, sq, 0.0), axis=-1, keepdims=True)
    ss_hi = jnp.sum(jnp.where(lo_mask, 0.0, sq), axis=-1, keepdims=True)
    r = jnp.where(lo_mask, lax.rsqrt(ss_lo * (1.0 / MOBA_DH) + EPS), lax.rsqrt(ss_hi * (1.0 / MOBA_DH) + EPS))
    return x * r * gain


def _select_bias(gate, blk, blk_f, j):
    g = jnp.where(blk < j, gate, SENTINEL)
    allowed = blk == j
    for _ in range(MOBA_TOPK):
        m = jnp.max(g, axis=0, keepdims=True)
        first = jnp.min(jnp.where(g == m, blk_f, 1e9), axis=0, keepdims=True)
        pick = (blk_f == first) & (m > SENTINEL)
        allowed = allowed | pick
        g = jnp.where(pick, SENTINEL, g)
    return jnp.where(allowed, 0.0, NEG_INF)


def _moba_prep_body(qkv_ref, qgt_ref, kg_ref, cost_ref, sint_ref, cos_ref, sin_ref, q_ref, k_ref, v_ref, kmt_scr,
                    *, scale):
    j = pl.program_id(1)

    @pl.when(j == 0)
    def _():
        kmt_scr[...] = jnp.zeros_like(kmt_scr)

    tq = qkv_ref.shape[0]
    hw = MOBA_HEADS * MOBA_DH
    dh, half, nblk = MOBA_DH, MOBA_DH // 2, MOBA_MAX_BLOCKS

    vt = qkv_ref[:, 2 * hw:].astype(F32).T
    for h in range(MOBA_HEADS):
        _value_rows(v_ref, h, vt[h * dh:(h + 1) * dh, :])

    qt_all = qkv_ref[:, :hw].astype(F32).T
    qgt = _lane_tile(qgt_ref[...], tq // LANES)
    cost = cost_ref[...]
    sint = sint_ref[...]
    blk = lax.broadcasted_iota(jnp.int32, (nblk, tq), 0)
    blk_f = blk.astype(F32)
    zeros = jnp.zeros((nblk, tq), F32)

    lane = lax.broadcasted_iota(jnp.int32, (tq, LANES), 1)
    lo_mask = lane < dh
    cos = cos_ref[...]
    sin = sin_ref[...]
    row_lane = lax.broadcasted_iota(jnp.int32, (1, LANES), 1)
    slot = lax.broadcasted_iota(jnp.int32, (LANES, LANES), 0)
    for p in range(MOBA_HEADS // 2):
        x = qt_all[p * LANES:(p + 1) * LANES, :]
        sq = x * x
        r_e = lax.rsqrt(jnp.sum(sq[:dh], axis=0, keepdims=True) * (1.0 / dh) + EPS)
        r_o = lax.rsqrt(jnp.sum(sq[dh:], axis=0, keepdims=True) * (1.0 / dh) + EPS)
        qn = jnp.concatenate([x[:dh] * r_e, x[dh:] * r_o], axis=0) * qgt
        rot = jnp.concatenate([-qn[half:dh], qn[:half], -qn[dh + half:], qn[dh:dh + half]], axis=0)
        q = qn * cost + rot * sint
        gate = _dot(kmt_scr[p], q, precision=HIGHEST)
        bias_e = _select_bias(gate[dh:dh + nblk], blk, blk_f, j)
        bias_o = _select_bias(gate[:nblk], blk, blk_f, j)
        qs = q * scale
        q_ref[0, 2 * p] = jnp.concatenate([qs[:dh], bias_e, zeros], axis=0).astype(q_ref.dtype)
        q_ref[0, 2 * p + 1] = jnp.concatenate([bias_o, zeros, qs[dh:]], axis=0).astype(q_ref.dtype)

        kn = _pair_rms(qkv_ref[:, hw + p * LANES:hw + (p + 1) * LANES].astype(F32), kg_ref[...], lo_mask)
        k = kn * cos + _moba_rot(kn, lane) * sin
        k_ref[0, 2 * p] = jnp.where(lo_mask, k, jnp.where(lane == dh + j, 1.0, 0.0)).astype(k_ref.dtype)
        k_ref[0, 2 * p + 1] = jnp.where(lo_mask, jnp.where(lane == j, 1.0, 0.0), k).astype(k_ref.dtype)
        km = jnp.sum(k, axis=0, keepdims=True) * (1.0 / MOBA_BLOCK)
        new_rows = jnp.where(slot == dh + j, jnp.where(row_lane < dh, km, 0.0),
                             jnp.where(row_lane < dh, 0.0, km))
        kmt_scr[p] = jnp.where((slot == dh + j) | (slot == j), new_rows, kmt_scr[p])


def _moba_prep(proj, qgt, kg, cost, sint, cos, sin, *, b, s):
    tq = MOBA_BLOCK
    nt = s // tq
    scale = (MOBA_DH ** -0.5) * LOG2E
    cblk = COL_MOBA // (3 * MOBA_HEADS * MOBA_DH)
    return pl.pallas_call(
        functools.partial(_moba_prep_body, scale=scale),
        grid=(b, nt),
        in_specs=[
            pl.BlockSpec((tq, 3 * MOBA_HEADS * MOBA_DH), lambda bi, j: (bi * nt + j, cblk)),
            pl.BlockSpec((LANES, LANES), lambda bi, j: (0, 0)),
            pl.BlockSpec((1, LANES), lambda bi, j: (0, 0)),
            pl.BlockSpec((LANES, tq), lambda bi, j: (0, j)),
            pl.BlockSpec((LANES, tq), lambda bi, j: (0, j)),
            pl.BlockSpec((tq, LANES), lambda bi, j: (j, 0)),
            pl.BlockSpec((tq, LANES), lambda bi, j: (j, 0)),
        ],
        out_specs=[
            pl.BlockSpec((1, MOBA_HEADS, LANES, tq), lambda bi, j: (bi, 0, 0, j)),
            pl.BlockSpec((1, MOBA_HEADS, tq, LANES), lambda bi, j: (bi, 0, j, 0)),
            pl.BlockSpec((1, MOBA_HEADS, V_ROWS, tq), lambda bi, j: (bi, 0, 0, j)),
        ],
        out_shape=[
            jax.ShapeDtypeStruct((b, MOBA_HEADS, LANES, s), MXU_DTYPE),
            jax.ShapeDtypeStruct((b, MOBA_HEADS, s, LANES), MXU_DTYPE),
            jax.ShapeDtypeStruct((b, MOBA_HEADS, V_ROWS, s), MXU_DTYPE),
        ],
        scratch_shapes=[pltpu.VMEM((MOBA_HEADS // 2, LANES, LANES), F32)],
        compiler_params=_params(("parallel", "arbitrary")),
        name="moba_prep",
    )(proj, qgt, kg, cost, sint, cos, sin)


def _flash_body(qt_ref, k_ref, vt_ref, o_ref, buf0, buf1, *, tq):
    j = pl.program_id(2)

    def scores(qts, t, buf, hh, diagonal=False):
        ks = pl.ds(pl.multiple_of(t * tq, tq), tq)
        s = _dot(k_ref[0, hh, ks, :], qts[hh])
        if diagonal:
            key = lax.broadcasted_iota(jnp.int32, s.shape, 0)
            qry = lax.broadcasted_iota(jnp.int32, s.shape, 1)
            s = jnp.where(key <= qry, s, NEG_INF)
        buf[hh] = s
        return jnp.max(s, axis=0, keepdims=True)

    def consume(t, buf, hh, smax, stat):
        ks = pl.ds(pl.multiple_of(t * tq, tq), tq)
        m, acc = stat
        m_new = jnp.maximum(m, smax)
        p = jnp.exp2(buf[hh] - m_new)
        return m_new, jnp.exp2(m - m_new) * acc + _dot(vt_ref[0, hh, :, ks], p.astype(MXU_DTYPE))

    def step(qts, t_next, buf_next, t_cur, buf_cur, carry):
        smax, stats = carry
        new_smax, new_stats = [], []
        for hh in range(2):
            new_smax.append(scores(qts, t_next, buf_next, hh))
            new_stats.append(consume(t_cur, buf_cur, hh, smax[hh], stats[hh]))
        return tuple(new_smax), tuple(new_stats)

    def run(qts, i, first_cur, t_first, carry, bufs):
        n_steps = i - t_first

        def quad(u, carry):
            t0 = t_first + 4 * u
            carry = step(qts, t0, bufs[0], jnp.where(u == 0, first_cur, t0 - 1), bufs[1], carry)
            carry = step(qts, t0 + 1, bufs[1], t0, bufs[0], carry)
            carry = step(qts, t0 + 2, bufs[0], t0 + 1, bufs[1], carry)
            return step(qts, t0 + 3, bufs[1], t0 + 2, bufs[0], carry)

        carry = lax.fori_loop(0, n_steps // 4, quad, carry)
        t_pair = t_first + 4 * (n_steps // 4)

        def pair(u, carry):
            carry = step(qts, t_pair, bufs[0], jnp.where(t_pair == t_first, first_cur, t_pair - 1), bufs[1], carry)
            return step(qts, t_pair + 1, bufs[1], t_pair, bufs[0], carry)

        return lax.fori_loop(0, (n_steps % 4) // 2, pair, carry)

    def finish(t_last, buf, carry, rows):
        smax, stats = carry
        out = [consume(t_last, buf, hh, smax[hh], stats[hh])[1] for hh in range(2)]
        o_t = jnp.concatenate([a[:MLA_V] / a[MLA_V:MLA_V + 1] for a in out], axis=0)
        o_ref[0, rows, :] = o_t.T.astype(o_ref.dtype)

    init = ((jnp.full((1, tq), NEG_INF, F32), jnp.zeros((V_ROWS, tq), F32)),) * 2
    q_even = (qt_ref[0, 0, :, :tq], qt_ref[0, 1, :, :tq])
    q_odd = (qt_ref[0, 0, :, tq:], qt_ref[0, 1, :, tq:])
    i_even, i_odd = 2 * j, 2 * j + 1

    smax = tuple(scores(q_even, i_even, buf1, hh, True) for hh in range(2))
    carry = run(q_even, i_even, i_even, 0, (smax, init), (buf0, buf1))
    smax_odd = tuple(scores(q_odd, i_odd, buf0, hh, True) for hh in range(2))
    finish(jnp.where(j == 0, 0, i_even - 1), buf1, carry, slice(0, tq))
    carry = step(q_odd, 0, buf1, i_odd, buf0, (smax_odd, init))
    carry = run(q_odd, i_odd, 0, 1, carry, (buf0, buf1))
    finish(i_odd - 1, buf1, carry, slice(tq, 2 * tq))


def _flash(qt, k, vt, *, tq):
    b, h, s, _ = k.shape
    return pl.pallas_call(
        functools.partial(_flash_body, tq=tq),
        grid=(b, h // 2, s // (2 * tq)),
        in_specs=[
            pl.BlockSpec((1, 2, LANES, 2 * tq), lambda bi, p, j: (bi, p, 0, j)),
            pl.BlockSpec((1, 2, s, LANES), lambda bi, p, j: (bi, p, 0, 0)),
            pl.BlockSpec((1, 2, V_ROWS, s), lambda bi, p, j: (bi, p, 0, 0)),
        ],
        out_specs=pl.BlockSpec((1, 2 * tq, LANES), lambda bi, p, j: (bi, j, p)),
        out_shape=jax.ShapeDtypeStruct((b, s, MIX_WIDTH), MXU_DTYPE),
        scratch_shapes=[pltpu.VMEM((2, tq, tq), F32), pltpu.VMEM((2, tq, tq), F32)],
        compiler_params=_params(("parallel", "parallel", "arbitrary")),
        name="flash",
    )(qt, k, vt)


def _hi_lo(x):
    hi = x.astype(MXU_DTYPE)
    return hi, (x - hi.astype(F32)).astype(MXU_DTYPE)


def _unit_lower_solve(a_list, rhs_list, eye):
    c = eye.shape[0]
    n = range(len(a_list))
    t = [eye - a for a in a_list]
    pk = []
    for a in a_list:
        pwm = (-a).astype(MXU_DTYPE)
        pk.append(_dot(pwm, pwm))
    n_sq = int(np.log2(c)) - 1
    for it in range(n_sq):
        if it < n_sq - 1:
            both = [_dot(jnp.concatenate([t[j], pk[j]], axis=0).astype(MXU_DTYPE), pk[j].astype(MXU_DTYPE))
                    for j in n]
            t = [t[j] + both[j][:c] for j in n]
            pk = [both[j][c:] for j in n]
        else:
            t = [t[j] + _dot(t[j].astype(MXU_DTYPE), pk[j].astype(MXU_DTYPE)) for j in n]
    tm = [x.astype(MXU_DTYPE) for x in t]
    x1 = [_dot(tm[j], rhs_list[j].astype(MXU_DTYPE)) for j in n]
    a_split = [_hi_lo(a) for a in a_list]
    x_split = [_hi_lo(x) for x in x1]
    ax = [_dot(jnp.concatenate(a_split[j], axis=0), x_split[j][0]) for j in n]
    ax_lo = [_dot(a_split[j][0], x_split[j][1]) for j in n]
    res = [rhs_list[j] - x1[j] - (ax[j][:c] + ax[j][c:] + ax_lo[j]) for j in n]
    return [x1[j] + _dot(tm[j], res[j].astype(MXU_DTYPE)) for j in n]


def _gdn_prep_body(qkv_ref, halo_ref, ba_ref, bat_ref, conv_ref, alog_ref, dtb_ref, alogt_ref, dtbt_ref,
                   u_ref, w_ref, qd_ref, kd_ref, qk_ref, egl_ref, xe_scr, y_scr, bg_scr,
                   *, tiles_per_seq, nc):
    i = pl.program_id(0)
    tm = qkv_ref.shape[0]
    c = GDN_CHUNK
    hk = GDN_HEADS * GDN_DK

    xe_scr[pl.ds(0, GDN_HALO), :] = jnp.where(i % tiles_per_seq == 0, 0.0, halo_ref[...].astype(F32))
    xe_scr[pl.ds(GDN_HALO, tm), :] = qkv_ref[...].astype(F32)
    cw = conv_ref[...]
    xe = xe_scr[...]
    y = xe[GDN_HALO:] * cw[GDN_CONV - 1:GDN_CONV]
    for back in range(1, GDN_CONV):
        y = y + pltpu.roll(xe, back, 0)[GDN_HALO:] * cw[GDN_CONV - 1 - back:GDN_CONV - back]
    y_scr[...] = _silu(y)

    ba = ba_ref[...]
    lane = lax.broadcasted_iota(jnp.int32, ba.shape, 1)
    g_col = -jnp.exp(alog_ref[...]) * _softplus(ba + dtb_ref[...])
    bg_scr[...] = jnp.where(lane < GDN_HEADS, jax.nn.sigmoid(ba), g_col)

    ri = lax.broadcasted_iota(jnp.int32, (c, c), 0)
    ci = lax.broadcasted_iota(jnp.int32, (c, c), 1)
    tril = ci <= ri
    strict = ci < ri
    eye = jnp.where(ci == ri, 1.0, 0.0)
    lower_ones = jnp.where(tril, 1.0, 0.0)
    upper_ones = jnp.where(ci >= ri, 1.0, 0.0)

    def chunk_group(grp, _):
        chs = [grp * GDN_GROUP + j for j in range(GDN_GROUP)]
        rows = [pl.ds(pl.multiple_of(ch * c, c), c) for ch in chs]
        bg = [bg_scr[r, :] for r in rows]
        gcum_col = [_dot(lower_ones, x, precision=HIGHEST) for x in bg]
        g_row = [-jnp.exp(alogt_ref[...]) * _softplus(bat_ref[ch] + dtbt_ref[...]) for ch in chs]
        gcum_row = [_dot(x, upper_ones, precision=HIGHEST) for x in g_row]
        probs = [(j, h) for j in range(GDN_GROUP) for h in range(GDN_HEADS)]
        n = range(len(probs))
        sls = [slice(h * GDN_DK, (h + 1) * GDN_DK) for _, h in probs]
        qh = [y_scr[rows[j], sls[i]] for i, (j, h) in enumerate(probs)]
        kh = [y_scr[rows[j], pl.ds(hk + h * GDN_DK, GDN_DK)] for j, h in probs]
        vh = [y_scr[rows[j], pl.ds(2 * hk + h * GDN_DV, GDN_DV)] for j, h in probs]
        qn = [x * lax.rsqrt(jnp.sum(x * x, axis=-1, keepdims=True) + EPS) * (GDN_DK ** -0.5) for x in qh]
        kn = [x * lax.rsqrt(jnp.sum(x * x, axis=-1, keepdims=True) + EPS) for x in kh]
        beta = [bg[j][:, h:h + 1] for j, h in probs]
        gc = [gcum_col[j][:, GDN_HEADS + h:GDN_HEADS + h + 1] for j, h in probs]
        gr = [gcum_row[j][GDN_HEADS + h:GDN_HEADS + h + 1, :] for j, h in probs]
        kb = [kn[i] * beta[i] for i in n]
        knm = [x.astype(MXU_DTYPE) for x in kn]
        dec = [jnp.exp(jnp.where(tril, gc[i] - gr[i], NEG_INF)) for i in n]
        skk = [_dot_nt(kb[i].astype(MXU_DTYPE), knm[i]) for i in n]
        sqk = [_dot_nt(qn[i].astype(MXU_DTYPE), knm[i]) for i in n]
        a = [jnp.where(strict, skk[i] * dec[i], 0.0) for i in n]
        eg = [jnp.exp(x) for x in gc]
        rhs = [jnp.concatenate([vh[i] * beta[i], kb[i] * eg[i]], axis=1) for i in n]
        sol = _unit_lower_solve(a, rhs, eye)
        for i, (j, h) in enumerate(probs):
            sl, r = sls[i], rows[j]
            gl = gc[i][c - 1:c, :]
            u_ref[r, sl] = sol[i][:, :GDN_DV]
            w_ref[r, sl] = sol[i][:, GDN_DV:].astype(w_ref.dtype)
            qd_ref[r, sl] = (qn[i] * eg[i]).astype(qd_ref.dtype)
            kd_ref[r, sl] = (kn[i] * jnp.exp(gl - gc[i])).astype(kd_ref.dtype)
            qk_ref[r, pl.ds(h * LANES, c)] = (sqk[i] * dec[i]).astype(qk_ref.dtype)
            qk_ref[r, pl.ds(h * LANES + c, LANES - c)] = jnp.zeros((c, LANES - c), qk_ref.dtype)
            egl_ref[chs[j], :, sl] = jnp.broadcast_to(jnp.exp(gl), (1, GDN_DV))
        return 0

    lax.fori_loop(0, nc // GDN_GROUP, chunk_group, 0)


def _gdn_prep(proj, tail, bat, conv, alog, dtb, alogt, dtbt, *, s, tm):
    t = proj.shape[0]
    nc = tm // GDN_CHUNK
    hk = GDN_HEADS * GDN_DK
    width = 3 * hk
    cblk = COL_GDN // width
    full = lambda shape: pl.BlockSpec(shape, lambda i: (0,) * len(shape))
    row = lambda w: pl.BlockSpec((tm, w), lambda i: (i, 0))
    return pl.pallas_call(
        functools.partial(_gdn_prep_body, tiles_per_seq=s // tm, nc=nc),
        grid=(t // tm,),
        in_specs=[
            pl.BlockSpec((tm, width), lambda i: (i, cblk)),
            pl.BlockSpec((GDN_HALO, width), lambda i: (jnp.maximum(i * (tm // GDN_HALO) - 1, 0), cblk)),
            pl.BlockSpec((tm, LANES), lambda i: (i, 0)),
            pl.BlockSpec((nc, 8, GDN_CHUNK), lambda i: (i, 0, 0)),
            full((GDN_CONV, width)), full((1, LANES)), full((1, LANES)), full((8, 1)), full((8, 1)),
        ],
        out_specs=[row(hk), row(hk), row(hk), row(hk), row(GDN_HEADS * LANES),
                   pl.BlockSpec((nc, 1, hk), lambda i: (i, 0, 0))],
        out_shape=[
            jax.ShapeDtypeStruct((t, hk), F32),
            jax.ShapeDtypeStruct((t, hk), MXU_DTYPE),
            jax.ShapeDtypeStruct((t, hk), MXU_DTYPE),
            jax.ShapeDtypeStruct((t, hk), MXU_DTYPE),
            jax.ShapeDtypeStruct((t, GDN_HEADS * LANES), MXU_DTYPE),
            jax.ShapeDtypeStruct((t // GDN_CHUNK, 1, hk), F32),
        ],
        scratch_shapes=[
            pltpu.VMEM((tm + GDN_HALO, width), F32),
            pltpu.VMEM((tm, width), F32),
            pltpu.VMEM((tm, LANES), F32),
        ],
        compiler_params=_params(("parallel",)),
        name="gdn_prep",
    )(proj, proj, tail, bat, conv, alog, dtb, alogt, dtbt)


def _gdn_scan_body(u_ref, w_ref, qd_ref, kd_ref, qk_ref, egl_ref, z_ref, og_ref, o_ref, s_scr, *, nb, nc):
    @pl.when(pl.program_id(0) == 0)
    def _():
        s_scr[...] = jnp.zeros_like(s_scr)

    c = GDN_CHUNK
    og = og_ref[...]

    def chunk(ch, _):
        rows = pl.ds(pl.multiple_of(ch * c, c), c)
        probs = [(bi, h) for bi in range(nb) for h in range(GDN_HEADS)]
        n = range(len(probs))
        sls = [slice(h * GDN_DV, (h + 1) * GDN_DV) for _, h in probs]
        st = [s_scr[bi * GDN_HEADS + h] for bi, h in probs]
        stm = [x.astype(MXU_DTYPE) for x in st]
        wq = [jnp.concatenate([w_ref[bi, rows, sls[i]], qd_ref[bi, rows, sls[i]]], axis=0)
              for i, (bi, h) in enumerate(probs)]
        r1 = [_dot(wq[i], stm[i]) for i in n]
        vm = [(u_ref[bi, rows, sls[i]] - r1[i][:c]).astype(MXU_DTYPE) for i, (bi, h) in enumerate(probs)]
        o2 = [_dot(qk_ref[bi, rows, pl.ds(h * LANES, c)], vm[i]) for i, (bi, h) in enumerate(probs)]
        sd = [_dot_tn(kd_ref[bi, rows, sls[i]], vm[i]) for i, (bi, h) in enumerate(probs)]
        for i, (bi, h) in enumerate(probs):
            s_scr[bi * GDN_HEADS + h] = st[i] * egl_ref[bi, ch, :, sls[i]] + sd[i]
            o = r1[i][c:] + o2[i]
            o_ref[bi, rows, sls[i]] = (_rms(o, og, GDN_DV)
                                       * _silu(z_ref[bi, rows, sls[i]].astype(F32))).astype(o_ref.dtype)
        return 0

    lax.fori_loop(0, nc, chunk, 0)


def _gdn_scan(u, w, qd, kd, qk, egl, proj3, og, *, tt):
    b, s, hk = u.shape
    nc = tt // GDN_CHUNK
    blk = lambda width: pl.BlockSpec((b, tt, width), lambda i: (0, i, 0))
    return pl.pallas_call(
        functools.partial(_gdn_scan_body, nb=b, nc=nc),
        grid=(s // tt,),
        in_specs=[
            blk(hk), blk(hk), blk(hk), blk(hk), blk(GDN_HEADS * LANES),
            pl.BlockSpec((b, nc, 1, hk), lambda i: (0, i, 0, 0)),
            pl.BlockSpec((b, tt, hk), lambda i: (0, i, COL_Z // hk)),
            pl.BlockSpec((1, GDN_DV), lambda i: (0, 0)),
        ],
        out_specs=blk(hk),
        out_shape=jax.ShapeDtypeStruct((b, s, hk), MXU_DTYPE),
        scratch_shapes=[pltpu.VMEM((b * GDN_HEADS, GDN_DK, GDN_DV), F32)],
        compiler_params=_params(("arbitrary",)),
        name="gdn_scan",
    )(u, w, qd, kd, qk, egl, proj3, og)


def _merge_body(x_ref, om_ref, og_ref, ob_ref, gl_ref, wb_ref, wo_ref, o_ref):
    mixed = None
    for n, br in enumerate((om_ref, og_ref, ob_ref)):
        up = _dot(br[...], wb_ref[n].astype(MXU_DTYPE))
        term = jax.nn.sigmoid(gl_ref[:, n * D_MODEL:(n + 1) * D_MODEL].astype(F32)) * up
        mixed = term if mixed is None else mixed + term
    o_ref[...] = x_ref[...] + _dot(mixed.astype(MXU_DTYPE), wo_ref[...].astype(MXU_DTYPE))


def _merge(x, o_mla, o_gdn, o_moba, proj, wb, wo, layer, *, tm):
    t, d = x.shape
    row = lambda w: pl.BlockSpec((tm, w), lambda i: (i, 0))
    return pl.pallas_call(
        _merge_body,
        grid=(t // tm,),
        in_specs=[
            row(d), row(MIX_WIDTH), row(MIX_WIDTH), row(MIX_WIDTH), row(3 * d),
            _resident((3, MIX_WIDTH, d), layer),
            _resident((d, d), layer),
        ],
        out_specs=row(d),
        out_shape=jax.ShapeDtypeStruct((t, d), F32),
        compiler_params=_params(("parallel",)),
        name="merge",
    )(x, o_mla, o_gdn, o_moba, proj, wb, wo)


def _rope_tables(s):
    pos = jnp.arange(s, dtype=jnp.int32).astype(F32)[:, None]

    def cs(d):
        half = d // 2
        inv_freq = ROPE_THETA ** (-jnp.arange(half, dtype=F32) * 2.0 / d)
        ang = pos * inv_freq[None, :]
        return jnp.cos(ang), jnp.sin(ang)

    def mla(c, sn, axis):
        n = lambda k: (s, k) if axis == 1 else (k, s)
        pad = LANES - MLA_QK
        return (jnp.concatenate([jnp.ones(n(MLA_NOPE), F32), c, c, jnp.zeros(n(pad), F32)], axis=axis),
                jnp.concatenate([jnp.zeros(n(MLA_NOPE), F32), sn, sn, jnp.zeros(n(pad), F32)], axis=axis))

    c, sn = cs(MLA_ROPE)
    mla_cos, mla_sin = mla(c, sn, 1)
    mla_cost, mla_sint = mla(c.T, sn.T, 0)
    c, sn = cs(MOBA_DH)
    moba_cos, moba_sin = jnp.tile(c, (1, 4)), jnp.tile(sn, (1, 4))
    moba_cost, moba_sint = jnp.tile(c.T, (4, 1)), jnp.tile(sn.T, (4, 1))
    return mla_cos, mla_sin, mla_cost, mla_sint, moba_cos, moba_sin, moba_cost, moba_sint


def _proj_weights(w_in):
    d = w_in.shape[0]
    o = np.cumsum((0, MLA_Q_RANK, MLA_KV_RANK, MLA_ROPE, 512, 512, 512, GDN_HEADS, GDN_HEADS, 512, 1536, 3 * D_MODEL))
    cut = lambda a, b: w_in[:, o[a]:o[b]].astype(MXU_DTYPE)
    ba = cut(6, 8)
    tail = jnp.zeros((d, LANES), MXU_DTYPE).at[:, :2 * GDN_HEADS].set(ba).at[:, MLA_NOPE:MLA_QK].set(cut(2, 3))
    groups = [cut(10, 11), cut(3, 6), cut(9, 10), cut(8, 9), cut(0, 1), cut(1, 2), tail]
    return groups, ba.T


def _lane_pad(v, n=LANES):
    return jnp.pad(v, (0, n - v.shape[0]))[None, :]


def kernel(x, ffa_norm, ffa_w_in, ffa_w_out, mix_norm, w_in, mla_cq_norm, mla_ckv_norm, mla_w_uq, mla_w_ukv, mla_q_norm, mla_k_norm, gdn_conv, gdn_a_log, gdn_dt_bias, gdn_out_norm, moba_q_norm, moba_k_norm, w_branch, w_out, ffb_norm, ffb_w_in, ffb_w_out):
    b, s, d = x.shape
    t = b * s
    assert d == D_MODEL and s % 1024 == 0 and s // MOBA_BLOCK <= MOBA_MAX_BLOCKS
    depth = ffa_norm.shape[0]
    tm = 512
    mla_cos, mla_sin, mla_cost, mla_sint, moba_cos, moba_sin, moba_cost, moba_sint = _rope_tables(s)
    lane_bcast = lambda v: jnp.broadcast_to(v[:, None], (LANES, LANES))
    x = x.reshape(t, d)
    for l in range(depth):
        x = _ffn(x, ffa_norm[l][None, :], ffa_w_in, ffa_w_out, l,
                 tm=tm, tf=256)

        w_proj, w_bat = _proj_weights(w_in[l])
        proj, tail, bat = _inproj(x, mix_norm[l][None, :], w_proj, w_bat, tm=tm, tn=512)

        wq = jnp.pad(mla_w_uq[l].reshape(MLA_Q_RANK, MLA_HEADS, MLA_QK),
                     ((0, 0), (0, 0), (0, LANES - MLA_QK))).reshape(MLA_Q_RANK, MLA_HEADS * LANES)
        wkv = mla_w_ukv[l].reshape(MLA_KV_RANK, MLA_HEADS, MLA_NOPE + MLA_V)
        wk = jnp.pad(wkv[:, :, :MLA_NOPE], ((0, 0), (0, 0), (0, LANES - MLA_NOPE))).reshape(MLA_KV_RANK, MLA_HEADS * LANES)
        wv = wkv[:, :, MLA_NOPE:].reshape(MLA_KV_RANK, MLA_HEADS * MLA_V).T
        q, k, v = _mla_prep(proj, tail, mla_cq_norm[l][None, :], mla_ckv_norm[l][None, :], wq.T.astype(MXU_DTYPE),
                            wk.astype(MXU_DTYPE), wv.astype(MXU_DTYPE), lane_bcast(_lane_pad(mla_q_norm[l])[0]),
                            _lane_pad(mla_k_norm[l]), mla_cost, mla_sint, mla_cos, mla_sin, b=b, s=s, tm=tm)
        o_mla = _flash(q, k, v, tq=512)

        bat_chunks = bat.reshape(8, t // GDN_CHUNK, GDN_CHUNK).transpose(1, 0, 2)
        head_pad = lambda v: jnp.pad(v, (GDN_HEADS, LANES - 2 * GDN_HEADS))[None, :]
        head_col = lambda v: jnp.pad(v, (GDN_HEADS, 0))[:, None]
        u, w, qd, kd, qk, egl = _gdn_prep(proj, tail, bat_chunks, gdn_conv[l], head_pad(gdn_a_log[l]),
                                          head_pad(gdn_dt_bias[l]), head_col(gdn_a_log[l]),
                                          head_col(gdn_dt_bias[l]), s=s, tm=tm)
        r3 = lambda a: a.reshape(b, s, a.shape[-1])
        o_gdn = _gdn_scan(r3(u), r3(w), r3(qd), r3(kd), r3(qk), egl.reshape(b, s // GDN_CHUNK, 1, -1),
                          r3(proj), gdn_out_norm[l][None, :], tt=tm)

        q, k, v = _moba_prep(proj, lane_bcast(jnp.tile(moba_q_norm[l], 2)), jnp.tile(moba_k_norm[l], 2)[None, :],
                             moba_cost, moba_sint, moba_cos, moba_sin, b=b, s=s)
        o_moba = _flash(q, k, v, tq=512)

        x = _merge(x, o_mla.reshape(t, -1), o_gdn.reshape(t, -1), o_moba.reshape(t, -1), proj,
                   w_branch, w_out, l, tm=tm)

        x = _ffn(x, ffb_norm[l][None, :], ffb_w_in, ffb_w_out, l,
                 tm=tm, tf=256)
    return x.reshape(b, s, d)
```

```python
import functools

import numpy as np
import jax
import jax.numpy as jnp
from jax import lax
from jax.experimental import pallas as pl
from jax.experimental.pallas import tpu as pltpu

F32 = jnp.float32
BF16 = jnp.bfloat16
MXU_DTYPE = BF16
HIGHEST = lax.Precision.HIGHEST

EPS = 1e-6
ROPE_THETA = 10000.0
NEG_INF = -1e30
SENTINEL = -3e38
LOG2E = 1.4426950408889634

D_MODEL = 1024
D_FF = 2816
MIX_WIDTH = 512
LANES = 128

MLA_HEADS = 8
MLA_Q_RANK = 256
MLA_KV_RANK = 128
MLA_NOPE = 64
MLA_ROPE = 32
MLA_V = 64
MLA_QK = MLA_NOPE + MLA_ROPE

GDN_HEADS = 4
GDN_DK = 128
GDN_DV = 128
GDN_CONV = 4
GDN_CHUNK = 64
GDN_HALO = 16
GDN_GROUP = 4

MOBA_HEADS = 8
MOBA_DH = 64
MOBA_BLOCK = 256
MOBA_TOPK = 3
MOBA_MAX_BLOCKS = 32
V_ROWS = 80

COL_GATE = 0
COL_GDN = 3072
COL_MOBA = 4608
COL_Z = 6144
COL_MLA = 6656
N_PROJ = 7168

VMEM_LIMIT = 56 * 1024 * 1024


def _dot(a, b, precision=None):
    return jnp.dot(a, b, preferred_element_type=F32, precision=precision)


def _dot_nt(a, b, precision=None):
    return lax.dot_general(a, b, (((1,), (1,)), ((), ())), preferred_element_type=F32,
                           precision=precision)


def _dot_tn(a, b, precision=None):
    return lax.dot_general(a, b, (((0,), (0,)), ((), ())), preferred_element_type=F32,
                           precision=precision)


def _rms(x, gain, n):
    ms = jnp.sum(x * x, axis=-1, keepdims=True) * (1.0 / n)
    return x * lax.rsqrt(ms + EPS) * gain


def _silu(x):
    return x * jax.nn.sigmoid(x)


def _softplus(x):
    return jnp.maximum(x, 0.0) + jnp.log1p(jnp.exp(-jnp.abs(x)))


def _params(sem):
    return pltpu.CompilerParams(dimension_semantics=sem, vmem_limit_bytes=VMEM_LIMIT)


def _ffn_body(x_ref, g_ref, wi_ref, wo_ref, o_ref, act_scr, *, tf):
    x = x_ref[...]
    h = _rms(x, g_ref[...], D_MODEL).astype(MXU_DTYPE)
    for k in range(D_FF // tf):
        gate = _dot(h, wi_ref[:, k * tf:(k + 1) * tf].astype(MXU_DTYPE))
        up = _dot(h, wi_ref[:, D_FF + k * tf:D_FF + (k + 1) * tf].astype(MXU_DTYPE))
        act_scr[:, k * tf:(k + 1) * tf] = (_silu(gate) * up).astype(MXU_DTYPE)
    o_ref[...] = x + 0.5 * _dot(act_scr[...], wo_ref[...].astype(MXU_DTYPE))


def _resident(shape, layer=None):
    if layer is None:
        return pl.BlockSpec(shape, lambda *_: (0,) * len(shape), pipeline_mode=pl.Buffered(1))
    return pl.BlockSpec((None,) + tuple(shape), lambda *_: (layer,) + (0,) * len(shape),
                        pipeline_mode=pl.Buffered(1))


def _ffn(x, gain, w_in, w_out, layer, *, tm, tf):
    t, d = x.shape
    return pl.pallas_call(
        functools.partial(_ffn_body, tf=tf),
        grid=(t // tm,),
        in_specs=[
            pl.BlockSpec((tm, d), lambda i: (i, 0)),
            _resident((1, d)),
            _resident((d, 2 * D_FF), layer),
            _resident((D_FF, d), layer),
        ],
        out_specs=pl.BlockSpec((tm, d), lambda i: (i, 0)),
        out_shape=jax.ShapeDtypeStruct((t, d), F32),
        scratch_shapes=[pltpu.VMEM((tm, D_FF), MXU_DTYPE)],
        compiler_params=_params(("parallel",)),
        name="ffn",
    )(x, gain, w_in, w_out)


def _inproj_body(x_ref, g_ref, *refs, widths, tn):
    w_refs, wbat_ref = refs[:len(widths)], refs[len(widths)]
    o_ref, tail_ref, bat_ref = refs[len(widths) + 1:]
    h = _rms(x_ref[...], g_ref[...], D_MODEL).astype(MXU_DTYPE)
    bat_ref[...] = _dot_nt(wbat_ref[...], h)
    off = 0
    for w_ref, width in zip(w_refs, widths):
        for a in range(0, width, tn):
            n = min(tn, width - a)
            r = _dot(h, w_ref[:, a:a + n])
            o_ref[:, off + a:off + a + n] = r.astype(o_ref.dtype)
        off += width
    tail_ref[...] = r


def _inproj(x, gain, w_groups, wbat, *, tm, tn):
    t, d = x.shape
    widths = tuple(w.shape[1] for w in w_groups)
    assert sum(widths) == N_PROJ and widths[-1] == LANES
    return pl.pallas_call(
        functools.partial(_inproj_body, widths=widths, tn=tn),
        grid=(t // tm,),
        in_specs=[pl.BlockSpec((tm, d), lambda i: (i, 0)), _resident((1, d))]
                 + [_resident((d, w)) for w in widths] + [_resident((8, d))],
        out_specs=[
            pl.BlockSpec((tm, N_PROJ), lambda i: (i, 0)),
            pl.BlockSpec((tm, LANES), lambda i: (i, 0)),
            pl.BlockSpec((8, tm), lambda i: (0, i)),
        ],
        out_shape=[jax.ShapeDtypeStruct((t, N_PROJ), MXU_DTYPE), jax.ShapeDtypeStruct((t, LANES), F32),
                   jax.ShapeDtypeStruct((8, t), F32)],
        compiler_params=_params(("parallel",)),
        name="inproj",
    )(x, gain, *w_groups, wbat)


def _mla_rot(x, lane):
    return jnp.where(lane < MLA_NOPE + MLA_ROPE // 2, -pltpu.roll(x, LANES - MLA_ROPE // 2, 1),
                     pltpu.roll(x, MLA_ROPE // 2, 1))


def _lane_tile(x, n):
    return x if n == 1 else jnp.concatenate([x] * n, axis=1)


def _value_rows(v_ref, h, vt_h):
    n = vt_h.shape[1]
    v_ref[0, h, 0:MLA_V, :] = vt_h.astype(v_ref.dtype)
    pad_rows = lax.broadcasted_iota(jnp.int32, (V_ROWS - MLA_V, n), 0)
    v_ref[0, h, MLA_V:V_ROWS, :] = jnp.where(pad_rows == 0, 1.0, 0.0).astype(v_ref.dtype)


def _mla_prep_body(lat_ref, tail_ref, cqg_ref, ckvg_ref, wqt_ref, wk_ref, wvt_ref, qgt_ref, kg_ref, cost_ref,
                   sint_ref, cos_ref, sin_ref, q_ref, k_ref, v_ref, *, scale):
    lat = lat_ref[...].astype(F32)
    tm = lat.shape[0]
    cqn = _rms(lat[:, :MLA_Q_RANK], cqg_ref[...], MLA_Q_RANK).astype(MXU_DTYPE)
    ckvn = _rms(lat[:, MLA_Q_RANK:MLA_Q_RANK + MLA_KV_RANK], ckvg_ref[...], MLA_KV_RANK).astype(MXU_DTYPE)

    qt_all = _dot_nt(wqt_ref[...], cqn)
    qgt = _lane_tile(qgt_ref[...], tm // LANES)
    cost = cost_ref[...]
    sint = sint_ref[...]
    half = MLA_ROPE // 2
    for h in range(MLA_HEADS):
        x = qt_all[h * LANES:(h + 1) * LANES, :]
        ms = jnp.sum(x * x, axis=0, keepdims=True) * (1.0 / MLA_QK)
        qn = x * lax.rsqrt(ms + EPS) * qgt
        rot = jnp.concatenate([jnp.zeros((MLA_NOPE, tm), F32), -qn[MLA_NOPE + half:MLA_QK],
                               qn[MLA_NOPE:MLA_NOPE + half], jnp.zeros((LANES - MLA_QK, tm), F32)], axis=0)
        q_ref[0, h] = ((qn * cost + rot * sint) * scale).astype(q_ref.dtype)

    lane = lax.broadcasted_iota(jnp.int32, (tm, LANES), 1)
    kr = jnp.where((lane >= MLA_NOPE) & (lane < MLA_QK), tail_ref[...], 0.0)
    kg = kg_ref[...]
    cos = cos_ref[...]
    rot_kr_sin = _mla_rot(kr * kg, lane) * sin_ref[...]
    k_all = _dot(ckvn, wk_ref[...])
    for h in range(MLA_HEADS):
        kh = k_all[:, h * LANES:(h + 1) * LANES] + kr
        r = lax.rsqrt(jnp.sum(kh * kh, axis=-1, keepdims=True) * (1.0 / MLA_QK) + EPS)
        k_ref[0, h] = (r * (kh * kg * cos + rot_kr_sin)).astype(k_ref.dtype)

    vt = _dot_nt(wvt_ref[...], ckvn)
    for h in range(MLA_HEADS):
        _value_rows(v_ref, h, vt[h * MLA_V:(h + 1) * MLA_V, :])


def _mla_prep(proj, tail, cqg, ckvg, wqt, wk, wvt, qgt, kg, cost, sint, cos, sin, *, b, s, tm):
    nt = s // tm
    scale = (MLA_QK ** -0.5) * LOG2E
    cblk = COL_MLA // 512
    full = lambda shape: pl.BlockSpec(shape, lambda bi, i: (0,) * len(shape))
    return pl.pallas_call(
        functools.partial(_mla_prep_body, scale=scale),
        grid=(b, nt),
        in_specs=[
            pl.BlockSpec((tm, 512), lambda bi, i: (bi * nt + i, cblk)),
            pl.BlockSpec((tm, LANES), lambda bi, i: (bi * nt + i, 0)),
            full((1, MLA_Q_RANK)), full((1, MLA_KV_RANK)),
            full((MLA_HEADS * LANES, MLA_Q_RANK)), full((MLA_KV_RANK, MLA_HEADS * LANES)),
            full((MIX_WIDTH, MLA_KV_RANK)), full((LANES, LANES)), full((1, LANES)),
            pl.BlockSpec((LANES, tm), lambda bi, i: (0, i)),
            pl.BlockSpec((LANES, tm), lambda bi, i: (0, i)),
            pl.BlockSpec((tm, LANES), lambda bi, i: (i, 0)),
            pl.BlockSpec((tm, LANES), lambda bi, i: (i, 0)),
        ],
        out_specs=[
            pl.BlockSpec((1, MLA_HEADS, LANES, tm), lambda bi, i: (bi, 0, 0, i)),
            pl.BlockSpec((1, MLA_HEADS, tm, LANES), lambda bi, i: (bi, 0, i, 0)),
            pl.BlockSpec((1, MLA_HEADS, V_ROWS, tm), lambda bi, i: (bi, 0, 0, i)),
        ],
        out_shape=[
            jax.ShapeDtypeStruct((b, MLA_HEADS, LANES, s), MXU_DTYPE),
            jax.ShapeDtypeStruct((b, MLA_HEADS, s, LANES), MXU_DTYPE),
            jax.ShapeDtypeStruct((b, MLA_HEADS, V_ROWS, s), MXU_DTYPE),
        ],
        compiler_params=_params(("parallel", "parallel")),
        name="mla_prep",
    )(proj, tail, cqg, ckvg, wqt, wk, wvt, qgt, kg, cost, sint, cos, sin)


def _moba_rot(x, lane):
    half = MOBA_DH // 2
    return jnp.where(lane % MOBA_DH < half, -pltpu.roll(x, LANES - half, 1), pltpu.roll(x, half, 1))


def _pair_rms(x, gain, lo_mask):
    sq = x * x
    ss_lo = jnp.sum(jnp.where(lo_mask, sq, 0.0), axis=-1, keepdims=True)
    ss_hi = jnp.sum(jnp.where(lo_mask, 0.0, sq), axis=-1, keepdims=True)
    r = jnp.where(lo_mask, lax.rsqrt(ss_lo * (1.0 / MOBA_DH) + EPS), lax.rsqrt(ss_hi * (1.0 / MOBA_DH) + EPS))
    return x * r * gain


def _select_bias(gate, blk, blk_f, j):
    g = jnp.where(blk < j, gate, SENTINEL)
    allowed = blk == j
    for _ in range(MOBA_TOPK):
        m = jnp.max(g, axis=0, keepdims=True)
        first = jnp.min(jnp.where(g == m, blk_f, 1e9), axis=0, keepdims=True)
        pick = (blk_f == first) & (m > SENTINEL)
        allowed = allowed | pick
        g = jnp.where(pick, SENTINEL, g)
    return jnp.where(allowed, 0.0, NEG_INF)


def _moba_prep_body(qkv_ref, qgt_ref, kg_ref, cost_ref, sint_ref, cos_ref, sin_ref, q_ref, k_ref, v_ref, kmt_scr,
                    *, scale):
    j = pl.program_id(1)

    @pl.when(j == 0)
    def _():
        kmt_scr[...] = jnp.zeros_like(kmt_scr)

    tq = qkv_ref.shape[0]
    hw = MOBA_HEADS * MOBA_DH
    dh, half, nblk = MOBA_DH, MOBA_DH // 2, MOBA_MAX_BLOCKS

    vt = qkv_ref[:, 2 * hw:].astype(F32).T
    for h in range(MOBA_HEADS):
        _value_rows(v_ref, h, vt[h * dh:(h + 1) * dh, :])

    qt_all = qkv_ref[:, :hw].astype(F32).T
    qgt = _lane_tile(qgt_ref[...], tq // LANES)
    cost = cost_ref[...]
    sint = sint_ref[...]
    blk = lax.broadcasted_iota(jnp.int32, (nblk, tq), 0)
    blk_f = blk.astype(F32)
    zeros = jnp.zeros((nblk, tq), F32)

    lane = lax.broadcasted_iota(jnp.int32, (tq, LANES), 1)
    lo_mask = lane < dh
    cos = cos_ref[...]
    sin = sin_ref[...]
    row_lane = lax.broadcasted_iota(jnp.int32, (1, LANES), 1)
    slot = lax.broadcasted_iota(jnp.int32, (LANES, LANES), 0)
    for p in range(MOBA_HEADS // 2):
        x = qt_all[p * LANES:(p + 1) * LANES, :]
        sq = x * x
        r_e = lax.rsqrt(jnp.sum(sq[:dh], axis=0, keepdims=True) * (1.0 / dh) + EPS)
        r_o = lax.rsqrt(jnp.sum(sq[dh:], axis=0, keepdims=True) * (1.0 / dh) + EPS)
        qn = jnp.concatenate([x[:dh] * r_e, x[dh:] * r_o], axis=0) * qgt
        rot = jnp.concatenate([-qn[half:dh], qn[:half], -qn[dh + half:], qn[dh:dh + half]], axis=0)
        q = qn * cost + rot * sint
        gate = _dot(kmt_scr[p], q, precision=HIGHEST)
        bias_e = _select_bias(gate[dh:dh + nblk], blk, blk_f, j)
        bias_o = _select_bias(gate[:nblk], blk, blk_f, j)
        qs = q * scale
        q_ref[0, 2 * p] = jnp.concatenate([qs[:dh], bias_e, zeros], axis=0).astype(q_ref.dtype)
        q_ref[0, 2 * p + 1] = jnp.concatenate([bias_o, zeros, qs[dh:]], axis=0).astype(q_ref.dtype)

        kn = _pair_rms(qkv_ref[:, hw + p * LANES:hw + (p + 1) * LANES].astype(F32), kg_ref[...], lo_mask)
        k = kn * cos + _moba_rot(kn, lane) * sin
        k_ref[0, 2 * p] = jnp.where(lo_mask, k, jnp.where(lane == dh + j, 1.0, 0.0)).astype(k_ref.dtype)
        k_ref[0, 2 * p + 1] = jnp.where(lo_mask, jnp.where(lane == j, 1.0, 0.0), k).astype(k_ref.dtype)
        km = jnp.sum(k, axis=0, keepdims=True) * (1.0 / MOBA_BLOCK)
        new_rows = jnp.where(slot == dh + j, jnp.where(row_lane < dh, km, 0.0),
                             jnp.where(row_lane < dh, 0.0, km))
        kmt_scr[p] = jnp.where((slot == dh + j) | (slot == j), new_rows, kmt_scr[p])


def _moba_prep(proj, qgt, kg, cost, sint, cos, sin, *, b, s):
    tq = MOBA_BLOCK
    nt = s // tq
    scale = (MOBA_DH ** -0.5) * LOG2E
    cblk = COL_MOBA // (3 * MOBA_HEADS * MOBA_DH)
    return pl.pallas_call(
        functools.partial(_moba_prep_body, scale=scale),
        grid=(b, nt),
        in_specs=[
            pl.BlockSpec((tq, 3 * MOBA_HEADS * MOBA_DH), lambda bi, j: (bi * nt + j, cblk)),
            pl.BlockSpec((LANES, LANES), lambda bi, j: (0, 0)),
            pl.BlockSpec((1, LANES), lambda bi, j: (0, 0)),
            pl.BlockSpec((LANES, tq), lambda bi, j: (0, j)),
            pl.BlockSpec((LANES, tq), lambda bi, j: (0, j)),
            pl.BlockSpec((tq, LANES), lambda bi, j: (j, 0)),
            pl.BlockSpec((tq, LANES), lambda bi, j: (j, 0)),
        ],
        out_specs=[
            pl.BlockSpec((1, MOBA_HEADS, LANES, tq), lambda bi, j: (bi, 0, 0, j)),
            pl.BlockSpec((1, MOBA_HEADS, tq, LANES), lambda bi, j: (bi, 0, j, 0)),
            pl.BlockSpec((1, MOBA_HEADS, V_ROWS, tq), lambda bi, j: (bi, 0, 0, j)),
        ],
        out_shape=[
            jax.ShapeDtypeStruct((b, MOBA_HEADS, LANES, s), MXU_DTYPE),
            jax.ShapeDtypeStruct((b, MOBA_HEADS, s, LANES), MXU_DTYPE),
            jax.ShapeDtypeStruct((b, MOBA_HEADS, V_ROWS, s), MXU_DTYPE),
        ],
        scratch_shapes=[pltpu.VMEM((MOBA_HEADS // 2, LANES, LANES), F32)],
        compiler_params=_params(("parallel", "arbitrary")),
        name="moba_prep",
    )(proj, qgt, kg, cost, sint, cos, sin)


def _flash_body(qt_ref, k_ref, vt_ref, o_ref, diag, ev0, ev1, od0, od1, *, tq, n_tiles):
    def q_tile(i):
        qs = pl.ds(pl.multiple_of(i * tq, tq), tq)
        return (qt_ref[0, 0, :, qs], qt_ref[0, 1, :, qs])

    def scores(qts, t, buf, hh, diagonal=False):
        ks = pl.ds(pl.multiple_of(t * tq, tq), tq)
        s = _dot(k_ref[0, hh, ks, :], qts[hh])
        if diagonal:
            key = lax.broadcasted_iota(jnp.int32, s.shape, 0)
            qry = lax.broadcasted_iota(jnp.int32, s.shape, 1)
            s = jnp.where(key <= qry, s, NEG_INF)
        buf[hh] = s
        return jnp.max(s, axis=0, keepdims=True)

    def diag_scores(i, buf):
        qts = q_tile(i)
        return tuple(scores(qts, i, buf, hh, True) for hh in range(2))

    def consume(t, buf, hh, smax, stat):
        ks = pl.ds(pl.multiple_of(t * tq, tq), tq)
        m, acc = stat
        m_new = jnp.maximum(m, smax)
        p = jnp.exp2(buf[hh] - m_new)
        return m_new, jnp.exp2(m - m_new) * acc + _dot(vt_ref[0, hh, :, ks], p.astype(MXU_DTYPE))

    def step(qts, t_next, buf_next, t_cur, buf_cur, carry):
        smax, stats = carry
        new_smax, new_stats = [], []
        for hh in range(2):
            new_smax.append(scores(qts, t_next, buf_next, hh))
            new_stats.append(consume(t_cur, buf_cur, hh, smax[hh], stats[hh]))
        return tuple(new_smax), tuple(new_stats)

    def run(qts, i, t_first, carry, bufs):
        n_steps = i - t_first

        def quad(u, carry):
            t0 = t_first + 4 * u
            carry = step(qts, t0, bufs[0], t0 - 1, bufs[1], carry)
            carry = step(qts, t0 + 1, bufs[1], t0, bufs[0], carry)
            carry = step(qts, t0 + 2, bufs[0], t0 + 1, bufs[1], carry)
            return step(qts, t0 + 3, bufs[1], t0 + 2, bufs[0], carry)

        carry = lax.fori_loop(0, n_steps // 4, quad, carry)
        t_pair = t_first + 4 * (n_steps // 4)

        def pair(u, carry):
            carry = step(qts, t_pair, bufs[0], t_pair - 1, bufs[1], carry)
            return step(qts, t_pair + 1, bufs[1], t_pair, bufs[0], carry)

        return lax.fori_loop(0, (n_steps % 4) // 2, pair, carry)

    def finish(i, t_last, buf, carry):
        smax, stats = carry
        out = [consume(t_last, buf, hh, smax[hh], stats[hh])[1] for hh in range(2)]
        o_t = jnp.concatenate([a[:MLA_V] / a[MLA_V:MLA_V + 1] for a in out], axis=0)
        o_ref[0, pl.ds(pl.multiple_of(i * tq, tq), tq), :] = o_t.T.astype(o_ref.dtype)

    init = ((jnp.full((1, tq), NEG_INF, F32), jnp.zeros((V_ROWS, tq), F32)),) * 2

    smax0 = diag_scores(0, diag)
    smax1 = diag_scores(1, ev0)
    finish(0, 0, diag, (smax0, init))
    carry = step(q_tile(1), 0, ev1, 1, ev0, (smax1, init))
    smax2 = diag_scores(2, diag)
    finish(1, 0, ev1, carry)

    def tile_pair(j, smax_even):
        i = 2 * j
        qts = q_tile(i)
        carry = step(qts, 0, ev0, i, diag, (smax_even, init))
        carry = step(qts, 1, ev1, 0, ev0, carry)
        carry = run(qts, i, 2, carry, (ev0, ev1))
        smax_odd = diag_scores(i + 1, diag)
        finish(i, i - 1, ev1, carry)

        i = 2 * j + 1
        qts = q_tile(i)
        carry = step(qts, 0, od0, i, diag, (smax_odd, init))
        carry = step(qts, 1, od1, 0, od0, carry)
        carry = step(qts, 2, od0, 1, od1, carry)
        carry = run(qts, i, 3, carry, (od1, od0))
        smax_next = diag_scores(jnp.minimum(i + 1, n_tiles - 1), diag)
        finish(i, i - 1, od0, carry)
        return smax_next

    lax.fori_loop(1, n_tiles // 2, tile_pair, smax2)


def _flash(qt, k, vt, *, tq):
    b, h, s, _ = k.shape
    n_tiles = s // tq
    assert n_tiles % 2 == 0 and n_tiles >= 4
    buf = pltpu.VMEM((2, tq, tq), F32)
    return pl.pallas_call(
        functools.partial(_flash_body, tq=tq, n_tiles=n_tiles),
        grid=(b, h // 2),
        in_specs=[
            pl.BlockSpec((1, 2, LANES, s), lambda bi, p: (bi, p, 0, 0)),
            pl.BlockSpec((1, 2, s, LANES), lambda bi, p: (bi, p, 0, 0)),
            pl.BlockSpec((1, 2, V_ROWS, s), lambda bi, p: (bi, p, 0, 0)),
        ],
        out_specs=pl.BlockSpec((1, s, LANES), lambda bi, p: (bi, 0, p)),
        out_shape=jax.ShapeDtypeStruct((b, s, MIX_WIDTH), MXU_DTYPE),
        scratch_shapes=[buf] * 5,
        compiler_params=_params(("parallel", "parallel")),
        name="flash",
    )(qt, k, vt)


def _hi_lo(x):
    hi = x.astype(MXU_DTYPE)
    return hi, (x - hi.astype(F32)).astype(MXU_DTYPE)


def _unit_lower_solve(a_list, rhs_list, eye):
    c = eye.shape[0]
    n = range(len(a_list))
    t = [eye - a for a in a_list]
    pk = []
    for a in a_list:
        pwm = (-a).astype(MXU_DTYPE)
        pk.append(_dot(pwm, pwm))
    n_sq = int(np.log2(c)) - 1
    for it in range(n_sq):
        if it < n_sq - 1:
            both = [_dot(jnp.concatenate([t[j], pk[j]], axis=0).astype(MXU_DTYPE), pk[j].astype(MXU_DTYPE))
                    for j in n]
            t = [t[j] + both[j][:c] for j in n]
            pk = [both[j][c:] for j in n]
        else:
            t = [t[j] + _dot(t[j].astype(MXU_DTYPE), pk[j].astype(MXU_DTYPE)) for j in n]
    tm = [x.astype(MXU_DTYPE) for x in t]
    x1 = [_dot(tm[j], rhs_list[j].astype(MXU_DTYPE)) for j in n]
    a_split = [_hi_lo(a) for a in a_list]
    x_split = [_hi_lo(x) for x in x1]
    ax = [_dot(jnp.concatenate(a_split[j], axis=0), x_split[j][0]) for j in n]
    ax_lo = [_dot(a_split[j][0], x_split[j][1]) for j in n]
    res = [rhs_list[j] - x1[j] - (ax[j][:c] + ax[j][c:] + ax_lo[j]) for j in n]
    return [x1[j] + _dot(tm[j], res[j].astype(MXU_DTYPE)) for j in n]


def _gdn_prep_body(qkv_ref, halo_ref, ba_ref, bat_ref, conv_ref, alog_ref, dtb_ref, alogt_ref, dtbt_ref,
                   u_ref, w_ref, qd_ref, kd_ref, qk_ref, egl_ref, xe_scr, y_scr, bg_scr,
                   *, tiles_per_seq, nc):
    i = pl.program_id(0)
    tm = qkv_ref.shape[0]
    c = GDN_CHUNK
    hk = GDN_HEADS * GDN_DK

    xe_scr[pl.ds(0, GDN_HALO), :] = jnp.where(i % tiles_per_seq == 0, 0.0, halo_ref[...].astype(F32))
    xe_scr[pl.ds(GDN_HALO, tm), :] = qkv_ref[...].astype(F32)
    cw = conv_ref[...]
    xe = xe_scr[...]
    y = xe[GDN_HALO:] * cw[GDN_CONV - 1:GDN_CONV]
    for back in range(1, GDN_CONV):
        y = y + pltpu.roll(xe, back, 0)[GDN_HALO:] * cw[GDN_CONV - 1 - back:GDN_CONV - back]
    y_scr[...] = _silu(y)

    ba = ba_ref[...]
    lane = lax.broadcasted_iota(jnp.int32, ba.shape, 1)
    g_col = -jnp.exp(alog_ref[...]) * _softplus(ba + dtb_ref[...])
    bg_scr[...] = jnp.where(lane < GDN_HEADS, jax.nn.sigmoid(ba), g_col)

    ri = lax.broadcasted_iota(jnp.int32, (c, c), 0)
    ci = lax.broadcasted_iota(jnp.int32, (c, c), 1)
    tril = ci <= ri
    strict = ci < ri
    eye = jnp.where(ci == ri, 1.0, 0.0)
    lower_ones = jnp.where(tril, 1.0, 0.0)
    upper_ones = jnp.where(ci >= ri, 1.0, 0.0)

    def chunk_group(grp, _):
        chs = [grp * GDN_GROUP + j for j in range(GDN_GROUP)]
        rows = [pl.ds(pl.multiple_of(ch * c, c), c) for ch in chs]
        bg = [bg_scr[r, :] for r in rows]
        gcum_col = [_dot(lower_ones, x, precision=HIGHEST) for x in bg]
        g_row = [-jnp.exp(alogt_ref[...]) * _softplus(bat_ref[ch] + dtbt_ref[...]) for ch in chs]
        gcum_row = [_dot(x, upper_ones, precision=HIGHEST) for x in g_row]
        probs = [(j, h) for j in range(GDN_GROUP) for h in range(GDN_HEADS)]
        n = range(len(probs))
        sls = [slice(h * GDN_DK, (h + 1) * GDN_DK) for _, h in probs]
        qh = [y_scr[rows[j], sls[i]] for i, (j, h) in enumerate(probs)]
        kh = [y_scr[rows[j], pl.ds(hk + h * GDN_DK, GDN_DK)] for j, h in probs]
        vh = [y_scr[rows[j], pl.ds(2 * hk + h * GDN_DV, GDN_DV)] for j, h in probs]
        qn = [x * lax.rsqrt(jnp.sum(x * x, axis=-1, keepdims=True) + EPS) * (GDN_DK ** -0.5) for x in qh]
        kn = [x * lax.rsqrt(jnp.sum(x * x, axis=-1, keepdims=True) + EPS) for x in kh]
        beta = [bg[j][:, h:h + 1] for j, h in probs]
        gc = [gcum_col[j][:, GDN_HEADS + h:GDN_HEADS + h + 1] for j, h in probs]
        gr = [gcum_row[j][GDN_HEADS + h:GDN_HEADS + h + 1, :] for j, h in probs]
        kb = [kn[i] * beta[i] for i in n]
        knm = [x.astype(MXU_DTYPE) for x in kn]
        dec = [jnp.exp(jnp.where(tril, gc[i] - gr[i], NEG_INF)) for i in n]
        skk = [_dot_nt(kb[i].astype(MXU_DTYPE), knm[i]) for i in n]
        sqk = [_dot_nt(qn[i].astype(MXU_DTYPE), knm[i]) for i in n]
        a = [jnp.where(strict, skk[i] * dec[i], 0.0) for i in n]
        eg = [jnp.exp(x) for x in gc]
        rhs = [jnp.concatenate([vh[i] * beta[i], kb[i] * eg[i]], axis=1) for i in n]
        sol = _unit_lower_solve(a, rhs, eye)
        for i, (j, h) in enumerate(probs):
            sl, r = sls[i], rows[j]
            gl = gc[i][c - 1:c, :]
            u_ref[r, sl] = sol[i][:, :GDN_DV]
            w_ref[r, sl] = sol[i][:, GDN_DV:].astype(w_ref.dtype)
            qd_ref[r, sl] = (qn[i] * eg[i]).astype(qd_ref.dtype)
            kd_ref[r, sl] = (kn[i] * jnp.exp(gl - gc[i])).astype(kd_ref.dtype)
            qk_ref[r, pl.ds(h * LANES, c)] = (sqk[i] * dec[i]).astype(qk_ref.dtype)
            qk_ref[r, pl.ds(h * LANES + c, LANES - c)] = jnp.zeros((c, LANES - c), qk_ref.dtype)
            egl_ref[chs[j], :, sl] = jnp.broadcast_to(jnp.exp(gl), (1, GDN_DV))
        return 0

    lax.fori_loop(0, nc // GDN_GROUP, chunk_group, 0)


def _gdn_prep(proj, tail, bat, conv, alog, dtb, alogt, dtbt, *, s, tm):
    t = proj.shape[0]
    nc = tm // GDN_CHUNK
    hk = GDN_HEADS * GDN_DK
    width = 3 * hk
    cblk = COL_GDN // width
    full = lambda shape: pl.BlockSpec(shape, lambda i: (0,) * len(shape))
    row = lambda w: pl.BlockSpec((tm, w), lambda i: (i, 0))
    return pl.pallas_call(
        functools.partial(_gdn_prep_body, tiles_per_seq=s // tm, nc=nc),
        grid=(t // tm,),
        in_specs=[
            pl.BlockSpec((tm, width), lambda i: (i, cblk)),
            pl.BlockSpec((GDN_HALO, width), lambda i: (jnp.maximum(i * (tm // GDN_HALO) - 1, 0), cblk)),
            pl.BlockSpec((tm, LANES), lambda i: (i, 0)),
            pl.BlockSpec((nc, 8, GDN_CHUNK), lambda i: (i, 0, 0)),
            full((GDN_CONV, width)), full((1, LANES)), full((1, LANES)), full((8, 1)), full((8, 1)),
        ],
        out_specs=[row(hk), row(hk), row(hk), row(hk), row(GDN_HEADS * LANES),
                   pl.BlockSpec((nc, 1, hk), lambda i: (i, 0, 0))],
        out_shape=[
            jax.ShapeDtypeStruct((t, hk), F32),
            jax.ShapeDtypeStruct((t, hk), MXU_DTYPE),
            jax.ShapeDtypeStruct((t, hk), MXU_DTYPE),
            jax.ShapeDtypeStruct((t, hk), MXU_DTYPE),
            jax.ShapeDtypeStruct((t, GDN_HEADS * LANES), MXU_DTYPE),
            jax.ShapeDtypeStruct((t // GDN_CHUNK, 1, hk), F32),
        ],
        scratch_shapes=[
            pltpu.VMEM((tm + GDN_HALO, width), F32),
            pltpu.VMEM((tm, width), F32),
            pltpu.VMEM((tm, LANES), F32),
        ],
        compiler_params=_params(("parallel",)),
        name="gdn_prep",
    )(proj, proj, tail, bat, conv, alog, dtb, alogt, dtbt)


def _gdn_scan_body(u_ref, w_ref, qd_ref, kd_ref, qk_ref, egl_ref, z_ref, og_ref, o_ref, s_scr, *, nb, nc):
    @pl.when(pl.program_id(0) == 0)
    def _():
        s_scr[...] = jnp.zeros_like(s_scr)

    c = GDN_CHUNK
    og = og_ref[...]

    def chunk(ch, _):
        rows = pl.ds(pl.multiple_of(ch * c, c), c)
        probs = [(bi, h) for bi in range(nb) for h in range(GDN_HEADS)]
        n = range(len(probs))
        sls = [slice(h * GDN_DV, (h + 1) * GDN_DV) for _, h in probs]
        st = [s_scr[bi * GDN_HEADS + h] for bi, h in probs]
        stm = [x.astype(MXU_DTYPE) for x in st]
        wq = [jnp.concatenate([w_ref[bi, rows, sls[i]], qd_ref[bi, rows, sls[i]]], axis=0)
              for i, (bi, h) in enumerate(probs)]
        r1 = [_dot(wq[i], stm[i]) for i in n]
        vm = [(u_ref[bi, rows, sls[i]] - r1[i][:c]).astype(MXU_DTYPE) for i, (bi, h) in enumerate(probs)]
        o2 = [_dot(qk_ref[bi, rows, pl.ds(h * LANES, c)], vm[i]) for i, (bi, h) in enumerate(probs)]
        sd = [_dot_tn(kd_ref[bi, rows, sls[i]], vm[i]) for i, (bi, h) in enumerate(probs)]
        for i, (bi, h) in enumerate(probs):
            s_scr[bi * GDN_HEADS + h] = st[i] * egl_ref[bi, ch, :, sls[i]] + sd[i]
            o = r1[i][c:] + o2[i]
            o_ref[bi, rows, sls[i]] = (_rms(o, og, GDN_DV)
                                       * _silu(z_ref[bi, rows, sls[i]].astype(F32))).astype(o_ref.dtype)
        return 0

    lax.fori_loop(0, nc, chunk, 0)


def _gdn_scan(u, w, qd, kd, qk, egl, proj3, og, *, tt):
    b, s, hk = u.shape
    nc = tt // GDN_CHUNK
    blk = lambda width: pl.BlockSpec((b, tt, width), lambda i: (0, i, 0))
    return pl.pallas_call(
        functools.partial(_gdn_scan_body, nb=b, nc=nc),
        grid=(s // tt,),
        in_specs=[
            blk(hk), blk(hk), blk(hk), blk(hk), blk(GDN_HEADS * LANES),
            pl.BlockSpec((b, nc, 1, hk), lambda i: (0, i, 0, 0)),
            pl.BlockSpec((b, tt, hk), lambda i: (0, i, COL_Z // hk)),
            pl.BlockSpec((1, GDN_DV), lambda i: (0, 0)),
        ],
        out_specs=blk(hk),
        out_shape=jax.ShapeDtypeStruct((b, s, hk), MXU_DTYPE),
        scratch_shapes=[pltpu.VMEM((b * GDN_HEADS, GDN_DK, GDN_DV), F32)],
        compiler_params=_params(("arbitrary",)),
        name="gdn_scan",
    )(u, w, qd, kd, qk, egl, proj3, og)


def _merge_body(x_ref, om_ref, og_ref, ob_ref, gl_ref, wb_ref, wo_ref, o_ref):
    mixed = None
    for n, br in enumerate((om_ref, og_ref, ob_ref)):
        up = _dot(br[...], wb_ref[n].astype(MXU_DTYPE))
        term = jax.nn.sigmoid(gl_ref[:, n * D_MODEL:(n + 1) * D_MODEL].astype(F32)) * up
        mixed = term if mixed is None else mixed + term
    o_ref[...] = x_ref[...] + _dot(mixed.astype(MXU_DTYPE), wo_ref[...].astype(MXU_DTYPE))


def _merge(x, o_mla, o_gdn, o_moba, proj, wb, wo, layer, *, tm):
    t, d = x.shape
    row = lambda w: pl.BlockSpec((tm, w), lambda i: (i, 0))
    return pl.pallas_call(
        _merge_body,
        grid=(t // tm,),
        in_specs=[
            row(d), row(MIX_WIDTH), row(MIX_WIDTH), row(MIX_WIDTH), row(3 * d),
            _resident((3, MIX_WIDTH, d), layer),
            _resident((d, d), layer),
        ],
        out_specs=row(d),
        out_shape=jax.ShapeDtypeStruct((t, d), F32),
        compiler_params=_params(("parallel",)),
        name="merge",
    )(x, o_mla, o_gdn, o_moba, proj, wb, wo)


def _rope_tables(s):
    pos = jnp.arange(s, dtype=jnp.int32).astype(F32)[:, None]

    def cs(d):
        half = d // 2
        inv_freq = ROPE_THETA ** (-jnp.arange(half, dtype=F32) * 2.0 / d)
        ang = pos * inv_freq[None, :]
        return jnp.cos(ang), jnp.sin(ang)

    def mla(c, sn, axis):
        n = lambda k: (s, k) if axis == 1 else (k, s)
        pad = LANES - MLA_QK
        return (jnp.concatenate([jnp.ones(n(MLA_NOPE), F32), c, c, jnp.zeros(n(pad), F32)], axis=axis),
                jnp.concatenate([jnp.zeros(n(MLA_NOPE), F32), sn, sn, jnp.zeros(n(pad), F32)], axis=axis))

    c, sn = cs(MLA_ROPE)
    mla_cos, mla_sin = mla(c, sn, 1)
    mla_cost, mla_sint = mla(c.T, sn.T, 0)
    c, sn = cs(MOBA_DH)
    moba_cos, moba_sin = jnp.tile(c, (1, 4)), jnp.tile(sn, (1, 4))
    moba_cost, moba_sint = jnp.tile(c.T, (4, 1)), jnp.tile(sn.T, (4, 1))
    return mla_cos, mla_sin, mla_cost, mla_sint, moba_cos, moba_sin, moba_cost, moba_sint


def _proj_weights(w_in):
    d = w_in.shape[0]
    o = np.cumsum((0, MLA_Q_RANK, MLA_KV_RANK, MLA_ROPE, 512, 512, 512, GDN_HEADS, GDN_HEADS, 512, 1536, 3 * D_MODEL))
    cut = lambda a, b: w_in[:, o[a]:o[b]].astype(MXU_DTYPE)
    ba = cut(6, 8)
    tail = jnp.zeros((d, LANES), MXU_DTYPE).at[:, :2 * GDN_HEADS].set(ba).at[:, MLA_NOPE:MLA_QK].set(cut(2, 3))
    groups = [cut(10, 11), cut(3, 6), cut(9, 10), cut(8, 9), cut(0, 1), cut(1, 2), tail]
    return groups, ba.T


def _lane_pad(v, n=LANES):
    return jnp.pad(v, (0, n - v.shape[0]))[None, :]


def kernel(x, ffa_norm, ffa_w_in, ffa_w_out, mix_norm, w_in, mla_cq_norm, mla_ckv_norm, mla_w_uq, mla_w_ukv, mla_q_norm, mla_k_norm, gdn_conv, gdn_a_log, gdn_dt_bias, gdn_out_norm, moba_q_norm, moba_k_norm, w_branch, w_out, ffb_norm, ffb_w_in, ffb_w_out):
    b, s, d = x.shape
    t = b * s
    assert d == D_MODEL and s % 2048 == 0 and s // MOBA_BLOCK <= MOBA_MAX_BLOCKS
    depth = ffa_norm.shape[0]
    tm = 512
    mla_cos, mla_sin, mla_cost, mla_sint, moba_cos, moba_sin, moba_cost, moba_sint = _rope_tables(s)
    lane_bcast = lambda v: jnp.broadcast_to(v[:, None], (LANES, LANES))
    x = x.reshape(t, d)
    for l in range(depth):
        x = _ffn(x, ffa_norm[l][None, :], ffa_w_in, ffa_w_out, l,
                 tm=tm, tf=256)

        w_proj, w_bat = _proj_weights(w_in[l])
        proj, tail, bat = _inproj(x, mix_norm[l][None, :], w_proj, w_bat, tm=tm, tn=512)

        wq = jnp.pad(mla_w_uq[l].reshape(MLA_Q_RANK, MLA_HEADS, MLA_QK),
                     ((0, 0), (0, 0), (0, LANES - MLA_QK))).reshape(MLA_Q_RANK, MLA_HEADS * LANES)
        wkv = mla_w_ukv[l].reshape(MLA_KV_RANK, MLA_HEADS, MLA_NOPE + MLA_V)
        wk = jnp.pad(wkv[:, :, :MLA_NOPE], ((0, 0), (0, 0), (0, LANES - MLA_NOPE))).reshape(MLA_KV_RANK, MLA_HEADS * LANES)
        wv = wkv[:, :, MLA_NOPE:].reshape(MLA_KV_RANK, MLA_HEADS * MLA_V).T
        q, k, v = _mla_prep(proj, tail, mla_cq_norm[l][None, :], mla_ckv_norm[l][None, :], wq.T.astype(MXU_DTYPE),
                            wk.astype(MXU_DTYPE), wv.astype(MXU_DTYPE), lane_bcast(_lane_pad(mla_q_norm[l])[0]),
                            _lane_pad(mla_k_norm[l]), mla_cost, mla_sint, mla_cos, mla_sin, b=b, s=s, tm=tm)
        o_mla = _flash(q, k, v, tq=512)

        bat_chunks = bat.reshape(8, t // GDN_CHUNK, GDN_CHUNK).transpose(1, 0, 2)
        head_pad = lambda v: jnp.pad(v, (GDN_HEADS, LANES - 2 * GDN_HEADS))[None, :]
        head_col = lambda v: jnp.pad(v, (GDN_HEADS, 0))[:, None]
        u, w, qd, kd, qk, egl = _gdn_prep(proj, tail, bat_chunks, gdn_conv[l], head_pad(gdn_a_log[l]),
                                          head_pad(gdn_dt_bias[l]), head_col(gdn_a_log[l]),
                                          head_col(gdn_dt_bias[l]), s=s, tm=tm)
        r3 = lambda a: a.reshape(b, s, a.shape[-1])
        o_gdn = _gdn_scan(r3(u), r3(w), r3(qd), r3(kd), r3(qk), egl.reshape(b, s // GDN_CHUNK, 1, -1),
                          r3(proj), gdn_out_norm[l][None, :], tt=tm)

        q, k, v = _moba_prep(proj, lane_bcast(jnp.tile(moba_q_norm[l], 2)), jnp.tile(moba_k_norm[l], 2)[None, :],
                             moba_cost, moba_sint, moba_cos, moba_sin, b=b, s=s)
        o_moba = _flash(q, k, v, tq=512)

        x = _merge(x, o_mla.reshape(t, -1), o_gdn.reshape(t, -1), o_moba.reshape(t, -1), proj,
                   w_branch, w_out, l, tm=tm)

        x = _ffn(x, ffb_norm[l][None, :], ffb_w_in, ffb_w_out, l,
                 tm=tm, tf=256)
    return x.reshape(b, s, d)
```

```python
import functools

import numpy as np
import jax
import jax.numpy as jnp
from jax import lax
from jax.experimental import pallas as pl
from jax.experimental.pallas import tpu as pltpu

F32 = jnp.float32
BF16 = jnp.bfloat16
MXU_DTYPE = BF16
HIGHEST = lax.Precision.HIGHEST

EPS = 1e-6
ROPE_THETA = 10000.0
NEG_INF = -1e30
SENTINEL = -3e38
LOG2E = 1.4426950408889634

D_MODEL = 1024
D_FF = 2816
MIX_WIDTH = 512
LANES = 128

MLA_HEADS = 8
MLA_Q_RANK = 256
MLA_KV_RANK = 128
MLA_NOPE = 64
MLA_ROPE = 32
MLA_V = 64
MLA_QK = MLA_NOPE + MLA_ROPE

GDN_HEADS = 4
GDN_DK = 128
GDN_DV = 128
GDN_CONV = 4
GDN_CHUNK = 64
GDN_HALO = 8
GDN_GROUP = 8

MOBA_HEADS = 8
MOBA_DH = 64
MOBA_BLOCK = 256
MOBA_TOPK = 3
MOBA_MAX_BLOCKS = 32
V_ROWS = 80

COL_GATE = 0
COL_GDN = 3072
COL_MOBA = 4608
COL_Z = 6144
COL_MLA = 6656
N_PROJ = 7168

VMEM_LIMIT = 56 * 1024 * 1024


def _dot(a, b, precision=None):
    return jnp.dot(a, b, preferred_element_type=F32, precision=precision)


def _dot_nt(a, b, precision=None):
    return lax.dot_general(a, b, (((1,), (1,)), ((), ())), preferred_element_type=F32,
                           precision=precision)


def _dot_tn(a, b, precision=None):
    return lax.dot_general(a, b, (((0,), (0,)), ((), ())), preferred_element_type=F32,
                           precision=precision)


def _rms(x, gain, n):
    ms = jnp.sum(x * x, axis=-1, keepdims=True) * (1.0 / n)
    return x * lax.rsqrt(ms + EPS) * gain


def _silu(x):
    return x * jax.nn.sigmoid(x)


def _softplus(x):
    return jnp.maximum(x, 0.0) + jnp.log1p(jnp.exp(-jnp.abs(x)))


def _params(sem):
    return pltpu.CompilerParams(dimension_semantics=sem, vmem_limit_bytes=VMEM_LIMIT)


def _ffn_body(x_ref, g_ref, wi_ref, wo_ref, o_ref, act_scr, *, tf):
    x = x_ref[...]
    h = _rms(x, g_ref[...], D_MODEL).astype(MXU_DTYPE)
    for k in range(D_FF // tf):
        gate = _dot(h, wi_ref[:, k * tf:(k + 1) * tf].astype(MXU_DTYPE))
        up = _dot(h, wi_ref[:, D_FF + k * tf:D_FF + (k + 1) * tf].astype(MXU_DTYPE))
        act_scr[:, k * tf:(k + 1) * tf] = (_silu(gate) * up).astype(MXU_DTYPE)
    o_ref[...] = x + 0.5 * _dot(act_scr[...], wo_ref[...].astype(MXU_DTYPE))


def _resident(shape, layer=None):
    if layer is None:
        return pl.BlockSpec(shape, lambda *_: (0,) * len(shape), pipeline_mode=pl.Buffered(1))
    return pl.BlockSpec((None,) + tuple(shape), lambda *_: (layer,) + (0,) * len(shape),
                        pipeline_mode=pl.Buffered(1))


def _ffn(x, gain, w_in, w_out, layer, *, tm, tf):
    t, d = x.shape
    return pl.pallas_call(
        functools.partial(_ffn_body, tf=tf),
        grid=(t // tm,),
        in_specs=[
            pl.BlockSpec((tm, d), lambda i: (i, 0)),
            _resident((1, d)),
            _resident((d, 2 * D_FF), layer),
            _resident((D_FF, d), layer),
        ],
        out_specs=pl.BlockSpec((tm, d), lambda i: (i, 0)),
        out_shape=jax.ShapeDtypeStruct((t, d), F32),
        scratch_shapes=[pltpu.VMEM((tm, D_FF), MXU_DTYPE)],
        compiler_params=_params(("parallel",)),
        name="ffn",
    )(x, gain, w_in, w_out)


def _inproj_body(x_ref, g_ref, conv_ref, *refs, widths, tn, conv_group, tiles_per_seq):
    w_refs, wbat_ref = refs[:len(widths)], refs[len(widths)]
    o_ref, tail_ref, bat_ref, halo_scr = refs[len(widths) + 1:]
    tm = x_ref.shape[0]
    seq_start = pl.program_id(0) % tiles_per_seq == 0
    h = _rms(x_ref[...], g_ref[...], D_MODEL).astype(MXU_DTYPE)
    bat_ref[...] = _dot_nt(wbat_ref[...], h)
    off = 0
    for gi, (w_ref, width) in enumerate(zip(w_refs, widths)):
        for a in range(0, width, tn):
            n = min(tn, width - a)
            r = _dot(h, w_ref[:, a:a + n])
            if gi == conv_group:
                cw = conv_ref[:, a:a + n]
                halo = jnp.where(seq_start, 0.0, halo_scr[:, a:a + n])
                halo_scr[:, a:a + n] = r[tm - GDN_HALO:, :]
                xe = jnp.concatenate([halo, r], axis=0)
                y = r * cw[GDN_CONV - 1:GDN_CONV]
                for back in range(1, GDN_CONV):
                    y = y + pltpu.roll(xe, back, 0)[GDN_HALO:] * cw[GDN_CONV - 1 - back:GDN_CONV - back]
                r = _silu(y)
            o_ref[:, off + a:off + a + n] = r.astype(o_ref.dtype)
        off += width
    tail_ref[...] = r


def _inproj(x, gain, conv, w_groups, wbat, *, s, tm, tn, conv_group):
    t, d = x.shape
    widths = tuple(w.shape[1] for w in w_groups)
    assert sum(widths) == N_PROJ and widths[-1] == LANES and widths[conv_group] == conv.shape[1]
    return pl.pallas_call(
        functools.partial(_inproj_body, widths=widths, tn=tn, conv_group=conv_group, tiles_per_seq=s // tm),
        grid=(t // tm,),
        in_specs=[pl.BlockSpec((tm, d), lambda i: (i, 0)), _resident((1, d)), _resident(conv.shape)]
                 + [_resident((d, w)) for w in widths] + [_resident((8, d))],
        out_specs=[
            pl.BlockSpec((tm, N_PROJ), lambda i: (i, 0)),
            pl.BlockSpec((tm, LANES), lambda i: (i, 0)),
            pl.BlockSpec((8, tm), lambda i: (0, i)),
        ],
        out_shape=[jax.ShapeDtypeStruct((t, N_PROJ), MXU_DTYPE), jax.ShapeDtypeStruct((t, LANES), F32),
                   jax.ShapeDtypeStruct((8, t), F32)],
        scratch_shapes=[pltpu.VMEM((GDN_HALO, conv.shape[1]), F32)],
        compiler_params=_params(("arbitrary",)),
        name="inproj",
    )(x, gain, conv, *w_groups, wbat)


def _mla_rot(x, lane):
    return jnp.where(lane < MLA_NOPE + MLA_ROPE // 2, -pltpu.roll(x, LANES - MLA_ROPE // 2, 1),
                     pltpu.roll(x, MLA_ROPE // 2, 1))


def _lane_tile(x, n):
    return x if n == 1 else jnp.concatenate([x] * n, axis=1)


def _value_rows(v_ref, h, vt_h):
    n = vt_h.shape[1]
    v_ref[0, h, 0:MLA_V, :] = vt_h.astype(v_ref.dtype)
    pad_rows = lax.broadcasted_iota(jnp.int32, (V_ROWS - MLA_V, n), 0)
    v_ref[0, h, MLA_V:V_ROWS, :] = jnp.where(pad_rows == 0, 1.0, 0.0).astype(v_ref.dtype)


def _mla_prep_body(lat_ref, tail_ref, cqg_ref, ckvg_ref, wqt_ref, wk_ref, wvt_ref, qgt_ref, kg_ref, cost_ref,
                   sint_ref, cos_ref, sin_ref, q_ref, k_ref, v_ref, *, scale):
    lat = lat_ref[...].astype(F32)
    tm = lat.shape[0]
    cqn = _rms(lat[:, :MLA_Q_RANK], cqg_ref[...], MLA_Q_RANK).astype(MXU_DTYPE)
    ckvn = _rms(lat[:, MLA_Q_RANK:MLA_Q_RANK + MLA_KV_RANK], ckvg_ref[...], MLA_KV_RANK).astype(MXU_DTYPE)

    qt_all = _dot_nt(wqt_ref[...], cqn)
    qgt = _lane_tile(qgt_ref[...], tm // LANES)
    cost = cost_ref[...]
    sint = sint_ref[...]
    half = MLA_ROPE // 2
    for h in range(MLA_HEADS):
        x = qt_all[h * LANES:(h + 1) * LANES, :]
        ms = jnp.sum(x * x, axis=0, keepdims=True) * (1.0 / MLA_QK)
        qn = x * lax.rsqrt(ms + EPS) * qgt
        rot = jnp.concatenate([jnp.zeros((MLA_NOPE, tm), F32), -qn[MLA_NOPE + half:MLA_QK],
                               qn[MLA_NOPE:MLA_NOPE + half], jnp.zeros((LANES - MLA_QK, tm), F32)], axis=0)
        q_ref[0, h] = ((qn * cost + rot * sint) * scale).astype(q_ref.dtype)

    lane = lax.broadcasted_iota(jnp.int32, (tm, LANES), 1)
    kr = jnp.where((lane >= MLA_NOPE) & (lane < MLA_QK), tail_ref[...], 0.0)
    kg = kg_ref[...]
    cos = cos_ref[...]
    rot_kr_sin = _mla_rot(kr * kg, lane) * sin_ref[...]
    k_all = _dot(ckvn, wk_ref[...])
    for h in range(MLA_HEADS):
        kh = k_all[:, h * LANES:(h + 1) * LANES] + kr
        r = lax.rsqrt(jnp.sum(kh * kh, axis=-1, keepdims=True) * (1.0 / MLA_QK) + EPS)
        k_ref[0, h] = (r * (kh * kg * cos + rot_kr_sin)).astype(k_ref.dtype)

    vt = _dot_nt(wvt_ref[...], ckvn)
    for h in range(MLA_HEADS):
        _value_rows(v_ref, h, vt[h * MLA_V:(h + 1) * MLA_V, :])


def _mla_prep(proj, tail, cqg, ckvg, wqt, wk, wvt, qgt, kg, cost, sint, cos, sin, *, b, s, tm):
    nt = s // tm
    scale = (MLA_QK ** -0.5) * LOG2E
    cblk = COL_MLA // 512
    full = lambda shape: pl.BlockSpec(shape, lambda bi, i: (0,) * len(shape))
    return pl.pallas_call(
        functools.partial(_mla_prep_body, scale=scale),
        grid=(b, nt),
        in_specs=[
            pl.BlockSpec((tm, 512), lambda bi, i: (bi * nt + i, cblk)),
            pl.BlockSpec((tm, LANES), lambda bi, i: (bi * nt + i, 0)),
            full((1, MLA_Q_RANK)), full((1, MLA_KV_RANK)),
            full((MLA_HEADS * LANES, MLA_Q_RANK)), full((MLA_KV_RANK, MLA_HEADS * LANES)),
            full((MIX_WIDTH, MLA_KV_RANK)), full((LANES, LANES)), full((1, LANES)),
            pl.BlockSpec((LANES, tm), lambda bi, i: (0, i)),
            pl.BlockSpec((LANES, tm), lambda bi, i: (0, i)),
            pl.BlockSpec((tm, LANES), lambda bi, i: (i, 0)),
            pl.BlockSpec((tm, LANES), lambda bi, i: (i, 0)),
        ],
        out_specs=[
            pl.BlockSpec((1, MLA_HEADS, LANES, tm), lambda bi, i: (bi, 0, 0, i)),
            pl.BlockSpec((1, MLA_HEADS, tm, LANES), lambda bi, i: (bi, 0, i, 0)),
            pl.BlockSpec((1, MLA_HEADS, V_ROWS, tm), lambda bi, i: (bi, 0, 0, i)),
        ],
        out_shape=[
            jax.ShapeDtypeStruct((b, MLA_HEADS, LANES, s), MXU_DTYPE),
            jax.ShapeDtypeStruct((b, MLA_HEADS, s, LANES), MXU_DTYPE),
            jax.ShapeDtypeStruct((b, MLA_HEADS, V_ROWS, s), MXU_DTYPE),
        ],
        compiler_params=_params(("parallel", "parallel")),
        name="mla_prep",
    )(proj, tail, cqg, ckvg, wqt, wk, wvt, qgt, kg, cost, sint, cos, sin)


def _moba_rot(x, lane):
    half = MOBA_DH // 2
    return jnp.where(lane % MOBA_DH < half, -pltpu.roll(x, LANES - half, 1), pltpu.roll(x, half, 1))


def _pair_rms(x, gain, lo_mask):
    sq = x * x
    ss_lo = jnp.sum(jnp.where(lo_mask, sq, 0.0), axis=-1, keepdims=True)
    ss_hi = jnp.sum(jnp.where(lo_mask, 0.0, sq), axis=-1, keepdims=True)
    r = jnp.where(lo_mask, lax.rsqrt(ss_lo * (1.0 / MOBA_DH) + EPS), lax.rsqrt(ss_hi * (1.0 / MOBA_DH) + EPS))
    return x * r * gain


def _select_bias(gate, blk, blk_f, j):
    g = jnp.where(blk < j, gate, SENTINEL)
    allowed = blk == j
    for _ in range(MOBA_TOPK):
        m = jnp.max(g, axis=0, keepdims=True)
        first = jnp.min(jnp.where(g == m, blk_f, 1e9), axis=0, keepdims=True)
        pick = (blk_f == first) & (m > SENTINEL)
        allowed = allowed | pick
        g = jnp.where(pick, SENTINEL, g)
    return jnp.where(allowed, 0.0, NEG_INF)


def _moba_prep_body(qkv_ref, qgt_ref, kg_ref, cost_ref, sint_ref, cos_ref, sin_ref, q_ref, k_ref, v_ref, kmt_scr,
                    *, scale):
    j = pl.program_id(1)

    @pl.when(j == 0)
    def _():
        kmt_scr[...] = jnp.zeros_like(kmt_scr)

    tq = qkv_ref.shape[0]
    hw = MOBA_HEADS * MOBA_DH
    dh, half, nblk = MOBA_DH, MOBA_DH // 2, MOBA_MAX_BLOCKS

    vt = qkv_ref[:, 2 * hw:].astype(F32).T
    for h in range(MOBA_HEADS):
        _value_rows(v_ref, h, vt[h * dh:(h + 1) * dh, :])

    qt_all = qkv_ref[:, :hw].astype(F32).T
    qgt = _lane_tile(qgt_ref[...], tq // LANES)
    cost = cost_ref[...]
    sint = sint_ref[...]
    blk = lax.broadcasted_iota(jnp.int32, (nblk, tq), 0)
    blk_f = blk.astype(F32)
    zeros = jnp.zeros((nblk, tq), F32)

    lane = lax.broadcasted_iota(jnp.int32, (tq, LANES), 1)
    lo_mask = lane < dh
    cos = cos_ref[...]
    sin = sin_ref[...]
    row_lane = lax.broadcasted_iota(jnp.int32, (1, LANES), 1)
    slot = lax.broadcasted_iota(jnp.int32, (LANES, LANES), 0)
    for p in range(MOBA_HEADS // 2):
        x = qt_all[p * LANES:(p + 1) * LANES, :]
        sq = x * x
        r_e = lax.rsqrt(jnp.sum(sq[:dh], axis=0, keepdims=True) * (1.0 / dh) + EPS)
        r_o = lax.rsqrt(jnp.sum(sq[dh:], axis=0, keepdims=True) * (1.0 / dh) + EPS)
        qn = jnp.concatenate([x[:dh] * r_e, x[dh:] * r_o], axis=0) * qgt
        rot = jnp.concatenate([-qn[half:dh], qn[:half], -qn[dh + half:], qn[dh:dh + half]], axis=0)
        q = qn * cost + rot * sint
        gate = _dot(kmt_scr[p], q, precision=HIGHEST)
        bias_e = _select_bias(gate[dh:dh + nblk], blk, blk_f, j)
        bias_o = _select_bias(gate[:nblk], blk, blk_f, j)
        qs = q * scale
        q_ref[0, 2 * p] = jnp.concatenate([qs[:dh], bias_e, zeros], axis=0).astype(q_ref.dtype)
        q_ref[0, 2 * p + 1] = jnp.concatenate([bias_o, zeros, qs[dh:]], axis=0).astype(q_ref.dtype)

        kn = _pair_rms(qkv_ref[:, hw + p * LANES:hw + (p + 1) * LANES].astype(F32), kg_ref[...], lo_mask)
        k = kn * cos + _moba_rot(kn, lane) * sin
        k_ref[0, 2 * p] = jnp.where(lo_mask, k, jnp.where(lane == dh + j, 1.0, 0.0)).astype(k_ref.dtype)
        k_ref[0, 2 * p + 1] = jnp.where(lo_mask, jnp.where(lane == j, 1.0, 0.0), k).astype(k_ref.dtype)
        km = jnp.sum(k, axis=0, keepdims=True) * (1.0 / MOBA_BLOCK)
        new_rows = jnp.where(slot == dh + j, jnp.where(row_lane < dh, km, 0.0),
                             jnp.where(row_lane < dh, 0.0, km))
        kmt_scr[p] = jnp.where((slot == dh + j) | (slot == j), new_rows, kmt_scr[p])


def _moba_prep(proj, qgt, kg, cost, sint, cos, sin, *, b, s):
    tq = MOBA_BLOCK
    nt = s // tq
    scale = (MOBA_DH ** -0.5) * LOG2E
    cblk = COL_MOBA // (3 * MOBA_HEADS * MOBA_DH)
    return pl.pallas_call(
        functools.partial(_moba_prep_body, scale=scale),
        grid=(b, nt),
        in_specs=[
            pl.BlockSpec((tq, 3 * MOBA_HEADS * MOBA_DH), lambda bi, j: (bi * nt + j, cblk)),
            pl.BlockSpec((LANES, LANES), lambda bi, j: (0, 0)),
            pl.BlockSpec((1, LANES), lambda bi, j: (0, 0)),
            pl.BlockSpec((LANES, tq), lambda bi, j: (0, j)),
            pl.BlockSpec((LANES, tq), lambda bi, j: (0, j)),
            pl.BlockSpec((tq, LANES), lambda bi, j: (j, 0)),
            pl.BlockSpec((tq, LANES), lambda bi, j: (j, 0)),
        ],
        out_specs=[
            pl.BlockSpec((1, MOBA_HEADS, LANES, tq), lambda bi, j: (bi, 0, 0, j)),
            pl.BlockSpec((1, MOBA_HEADS, tq, LANES), lambda bi, j: (bi, 0, j, 0)),
            pl.BlockSpec((1, MOBA_HEADS, V_ROWS, tq), lambda bi, j: (bi, 0, 0, j)),
        ],
        out_shape=[
            jax.ShapeDtypeStruct((b, MOBA_HEADS, LANES, s), MXU_DTYPE),
            jax.ShapeDtypeStruct((b, MOBA_HEADS, s, LANES), MXU_DTYPE),
            jax.ShapeDtypeStruct((b, MOBA_HEADS, V_ROWS, s), MXU_DTYPE),
        ],
        scratch_shapes=[pltpu.VMEM((MOBA_HEADS // 2, LANES, LANES), F32)],
        compiler_params=_params(("parallel", "arbitrary")),
        name="moba_prep",
    )(proj, qgt, kg, cost, sint, cos, sin)


def _flash_body(qt_ref, k_ref, vt_ref, o_ref, diag, ev0, ev1, od0, od1, *, tq, n_tiles):
    def q_tile(i):
        qs = pl.ds(pl.multiple_of(i * tq, tq), tq)
        return (qt_ref[0, 0, :, qs], qt_ref[0, 1, :, qs])

    def scores(qts, t, buf, hh, diagonal=False):
        ks = pl.ds(pl.multiple_of(t * tq, tq), tq)
        s = _dot(k_ref[0, hh, ks, :], qts[hh])
        if diagonal:
            key = lax.broadcasted_iota(jnp.int32, s.shape, 0)
            qry = lax.broadcasted_iota(jnp.int32, s.shape, 1)
            s = jnp.where(key <= qry, s, NEG_INF)
        buf[hh] = s
        return jnp.max(s, axis=0, keepdims=True)

    def diag_scores(i, buf):
        qts = q_tile(i)
        return tuple(scores(qts, i, buf, hh, True) for hh in range(2))

    def consume(t, buf, hh, smax, stat):
        ks = pl.ds(pl.multiple_of(t * tq, tq), tq)
        m, acc = stat
        m_new = jnp.maximum(m, smax)
        p = jnp.exp2(buf[hh] - m_new)
        return m_new, jnp.exp2(m - m_new) * acc + _dot(vt_ref[0, hh, :, ks], p.astype(MXU_DTYPE))

    def step(qts, t_next, buf_next, t_cur, buf_cur, carry):
        smax, stats = carry
        new_smax, new_stats = [], []
        for hh in range(2):
            new_smax.append(scores(qts, t_next, buf_next, hh))
            new_stats.append(consume(t_cur, buf_cur, hh, smax[hh], stats[hh]))
        return tuple(new_smax), tuple(new_stats)

    def run(qts, i, t_first, carry, bufs):
        n_steps = i - t_first

        def quad(u, carry):
            t0 = t_first + 4 * u
            carry = step(qts, t0, bufs[0], t0 - 1, bufs[1], carry)
            carry = step(qts, t0 + 1, bufs[1], t0, bufs[0], carry)
            carry = step(qts, t0 + 2, bufs[0], t0 + 1, bufs[1], carry)
            return step(qts, t0 + 3, bufs[1], t0 + 2, bufs[0], carry)

        carry = lax.fori_loop(0, n_steps // 4, quad, carry)
        t_pair = t_first + 4 * (n_steps // 4)

        def pair(u, carry):
            carry = step(qts, t_pair, bufs[0], t_pair - 1, bufs[1], carry)
            return step(qts, t_pair + 1, bufs[1], t_pair, bufs[0], carry)

        return lax.fori_loop(0, (n_steps % 4) // 2, pair, carry)

    def finish(i, t_last, buf, carry):
        smax, stats = carry
        out = [consume(t_last, buf, hh, smax[hh], stats[hh])[1] for hh in range(2)]
        o_t = jnp.concatenate([a[:MLA_V] / a[MLA_V:MLA_V + 1] for a in out], axis=0)
        o_ref[0, pl.ds(pl.multiple_of(i * tq, tq), tq), :] = o_t.T.astype(o_ref.dtype)

    init = ((jnp.full((1, tq), NEG_INF, F32), jnp.zeros((V_ROWS, tq), F32)),) * 2

    smax0 = diag_scores(0, diag)
    smax1 = diag_scores(1, ev0)
    finish(0, 0, diag, (smax0, init))
    carry = step(q_tile(1), 0, ev1, 1, ev0, (smax1, init))
    smax2 = diag_scores(2, diag)
    finish(1, 0, ev1, carry)

    def tile_pair(j, smax_even):
        i = 2 * j
        qts = q_tile(i)
        carry = step(qts, 0, ev0, i, diag, (smax_even, init))
        carry = step(qts, 1, ev1, 0, ev0, carry)
        carry = run(qts, i, 2, carry, (ev0, ev1))
        smax_odd = diag_scores(i + 1, diag)
        finish(i, i - 1, ev1, carry)

        i = 2 * j + 1
        qts = q_tile(i)
        carry = step(qts, 0, od0, i, diag, (smax_odd, init))
        carry = step(qts, 1, od1, 0, od0, carry)
        carry = step(qts, 2, od0, 1, od1, carry)
        carry = run(qts, i, 3, carry, (od1, od0))
        smax_next = diag_scores(jnp.minimum(i + 1, n_tiles - 1), diag)
        finish(i, i - 1, od0, carry)
        return smax_next

    lax.fori_loop(1, n_tiles // 2, tile_pair, smax2)


def _flash(qt, k, vt, *, tq):
    b, h, s, _ = k.shape
    n_tiles = s // tq
    assert n_tiles % 2 == 0 and n_tiles >= 4
    buf = pltpu.VMEM((2, tq, tq), F32)
    return pl.pallas_call(
        functools.partial(_flash_body, tq=tq, n_tiles=n_tiles),
        grid=(b, h // 2),
        in_specs=[
            pl.BlockSpec((1, 2, LANES, s), lambda bi, p: (bi, p, 0, 0)),
            pl.BlockSpec((1, 2, s, LANES), lambda bi, p: (bi, p, 0, 0)),
            pl.BlockSpec((1, 2, V_ROWS, s), lambda bi, p: (bi, p, 0, 0)),
        ],
        out_specs=pl.BlockSpec((1, s, LANES), lambda bi, p: (bi, 0, p)),
        out_shape=jax.ShapeDtypeStruct((b, s, MIX_WIDTH), MXU_DTYPE),
        scratch_shapes=[buf] * 5,
        compiler_params=_params(("parallel", "parallel")),
        name="flash",
    )(qt, k, vt)


def _hi_lo(x):
    hi = x.astype(MXU_DTYPE)
    return hi, (x - hi.astype(F32)).astype(MXU_DTYPE)


def _unit_lower_solve(a_list, rhs_list, eye):
    c = eye.shape[0]
    n = range(len(a_list))
    t = [eye - a for a in a_list]
    pk = []
    for a in a_list:
        pwm = (-a).astype(MXU_DTYPE)
        pk.append(_dot(pwm, pwm))
    n_sq = int(np.log2(c)) - 1
    for it in range(n_sq):
        if it < n_sq - 1:
            both = [_dot(jnp.concatenate([t[j], pk[j]], axis=0).astype(MXU_DTYPE), pk[j].astype(MXU_DTYPE))
                    for j in n]
            t = [t[j] + both[j][:c] for j in n]
            pk = [both[j][c:] for j in n]
        else:
            t = [t[j] + _dot(t[j].astype(MXU_DTYPE), pk[j].astype(MXU_DTYPE)) for j in n]
    tm = [x.astype(MXU_DTYPE) for x in t]
    x1 = [_dot(tm[j], rhs_list[j].astype(MXU_DTYPE)) for j in n]
    a_split = [_hi_lo(a) for a in a_list]
    x_split = [_hi_lo(x) for x in x1]
    ax = [_dot(jnp.concatenate(a_split[j], axis=0), x_split[j][0]) for j in n]
    ax_lo = [_dot(a_split[j][0], x_split[j][1]) for j in n]
    res = [rhs_list[j] - x1[j] - (ax[j][:c] + ax[j][c:] + ax_lo[j]) for j in n]
    return [x1[j] + _dot(tm[j], res[j].astype(MXU_DTYPE)) for j in n]


def _gdn_prep_body(qkv_ref, ba_ref, bat_ref, alog_ref, dtb_ref, alogt_ref, dtbt_ref,
                   u_ref, w_ref, qd_ref, kd_ref, qk_ref, egl_ref, bg_scr, *, nc):
    c = GDN_CHUNK
    hk = GDN_HEADS * GDN_DK

    ba = ba_ref[...]
    lane = lax.broadcasted_iota(jnp.int32, ba.shape, 1)
    g_col = -jnp.exp(alog_ref[...]) * _softplus(ba + dtb_ref[...])
    bg_scr[...] = jnp.where(lane < GDN_HEADS, jax.nn.sigmoid(ba), g_col)

    ri = lax.broadcasted_iota(jnp.int32, (c, c), 0)
    ci = lax.broadcasted_iota(jnp.int32, (c, c), 1)
    tril = ci <= ri
    strict = ci < ri
    eye = jnp.where(ci == ri, 1.0, 0.0)
    lower_ones = jnp.where(tril, 1.0, 0.0)
    upper_ones = jnp.where(ci >= ri, 1.0, 0.0)

    def chunk_group(grp, _):
        chs = [grp * GDN_GROUP + j for j in range(GDN_GROUP)]
        rows = [pl.ds(pl.multiple_of(ch * c, c), c) for ch in chs]
        bg = [bg_scr[r, :] for r in rows]
        gcum_col = [_dot(lower_ones, x, precision=HIGHEST) for x in bg]
        g_row = [-jnp.exp(alogt_ref[...]) * _softplus(bat_ref[ch] + dtbt_ref[...]) for ch in chs]
        gcum_row = [_dot(x, upper_ones, precision=HIGHEST) for x in g_row]
        probs = [(j, h) for j in range(GDN_GROUP) for h in range(GDN_HEADS)]
        n = range(len(probs))
        sls = [slice(h * GDN_DK, (h + 1) * GDN_DK) for _, h in probs]
        qh = [qkv_ref[rows[j], sls[i]].astype(F32) for i, (j, h) in enumerate(probs)]
        kh = [qkv_ref[rows[j], pl.ds(hk + h * GDN_DK, GDN_DK)].astype(F32) for j, h in probs]
        vh = [qkv_ref[rows[j], pl.ds(2 * hk + h * GDN_DV, GDN_DV)].astype(F32) for j, h in probs]
        qn = [x * lax.rsqrt(jnp.sum(x * x, axis=-1, keepdims=True) + EPS) * (GDN_DK ** -0.5) for x in qh]
        kn = [x * lax.rsqrt(jnp.sum(x * x, axis=-1, keepdims=True) + EPS) for x in kh]
        beta = [bg[j][:, h:h + 1] for j, h in probs]
        gc = [gcum_col[j][:, GDN_HEADS + h:GDN_HEADS + h + 1] for j, h in probs]
        gr = [gcum_row[j][GDN_HEADS + h:GDN_HEADS + h + 1, :] for j, h in probs]
        kb = [kn[i] * beta[i] for i in n]
        knm = [x.astype(MXU_DTYPE) for x in kn]
        dec = [jnp.exp(jnp.where(tril, gc[i] - gr[i], NEG_INF)) for i in n]
        skk = [_dot_nt(kb[i].astype(MXU_DTYPE), knm[i]) for i in n]
        sqk = [_dot_nt(qn[i].astype(MXU_DTYPE), knm[i]) for i in n]
        a = [jnp.where(strict, skk[i] * dec[i], 0.0) for i in n]
        eg = [jnp.exp(x) for x in gc]
        rhs = [jnp.concatenate([vh[i] * beta[i], kb[i] * eg[i]], axis=1) for i in n]
        sol = _unit_lower_solve(a, rhs, eye)
        for i, (j, h) in enumerate(probs):
            sl, r = sls[i], rows[j]
            gl = gc[i][c - 1:c, :]
            u_ref[r, sl] = sol[i][:, :GDN_DV]
            w_ref[r, sl] = sol[i][:, GDN_DV:].astype(w_ref.dtype)
            qd_ref[r, sl] = (qn[i] * eg[i]).astype(qd_ref.dtype)
            kd_ref[r, sl] = (kn[i] * jnp.exp(gl - gc[i])).astype(kd_ref.dtype)
            qk_ref[r, pl.ds(h * LANES, c)] = (sqk[i] * dec[i]).astype(qk_ref.dtype)
            qk_ref[r, pl.ds(h * LANES + c, LANES - c)] = jnp.zeros((c, LANES - c), qk_ref.dtype)
            egl_ref[chs[j], :, sl] = jnp.broadcast_to(jnp.exp(gl), (1, GDN_DV))
        return 0

    lax.fori_loop(0, nc // GDN_GROUP, chunk_group, 0)


def _gdn_prep(proj, tail, bat, alog, dtb, alogt, dtbt, *, tm):
    t = proj.shape[0]
    nc = tm // GDN_CHUNK
    hk = GDN_HEADS * GDN_DK
    width = 3 * hk
    cblk = COL_GDN // width
    full = lambda shape: pl.BlockSpec(shape, lambda i: (0,) * len(shape))
    row = lambda w: pl.BlockSpec((tm, w), lambda i: (i, 0))
    return pl.pallas_call(
        functools.partial(_gdn_prep_body, nc=nc),
        grid=(t // tm,),
        in_specs=[
            pl.BlockSpec((tm, width), lambda i: (i, cblk)),
            pl.BlockSpec((tm, LANES), lambda i: (i, 0)),
            pl.BlockSpec((nc, 8, GDN_CHUNK), lambda i: (i, 0, 0)),
            full((1, LANES)), full((1, LANES)), full((8, 1)), full((8, 1)),
        ],
        out_specs=[row(hk), row(hk), row(hk), row(hk), row(GDN_HEADS * LANES),
                   pl.BlockSpec((nc, 1, hk), lambda i: (i, 0, 0))],
        out_shape=[
            jax.ShapeDtypeStruct((t, hk), F32),
            jax.ShapeDtypeStruct((t, hk), MXU_DTYPE),
            jax.ShapeDtypeStruct((t, hk), MXU_DTYPE),
            jax.ShapeDtypeStruct((t, hk), MXU_DTYPE),
            jax.ShapeDtypeStruct((t, GDN_HEADS * LANES), MXU_DTYPE),
            jax.ShapeDtypeStruct((t // GDN_CHUNK, 1, hk), F32),
        ],
        scratch_shapes=[pltpu.VMEM((tm, LANES), F32)],
        compiler_params=_params(("parallel",)),
        name="gdn_prep",
    )(proj, tail, bat, alog, dtb, alogt, dtbt)


def _gdn_scan_body(u_ref, w_ref, qd_ref, kd_ref, qk_ref, egl_ref, z_ref, og_ref, o_ref, s_scr, *, nb, nc):
    @pl.when(pl.program_id(0) == 0)
    def _():
        s_scr[...] = jnp.zeros_like(s_scr)

    c = GDN_CHUNK
    og = og_ref[...]

    def chunk(ch, _):
        rows = pl.ds(pl.multiple_of(ch * c, c), c)
        probs = [(bi, h) for bi in range(nb) for h in range(GDN_HEADS)]
        n = range(len(probs))
        sls = [slice(h * GDN_DV, (h + 1) * GDN_DV) for _, h in probs]
        st = [s_scr[bi * GDN_HEADS + h] for bi, h in probs]
        stm = [x.astype(MXU_DTYPE) for x in st]
        wq = [jnp.concatenate([w_ref[bi, rows, sls[i]], qd_ref[bi, rows, sls[i]]], axis=0)
              for i, (bi, h) in enumerate(probs)]
        r1 = [_dot(wq[i], stm[i]) for i in n]
        vm = [(u_ref[bi, rows, sls[i]] - r1[i][:c]).astype(MXU_DTYPE) for i, (bi, h) in enumerate(probs)]
        o2 = [_dot(qk_ref[bi, rows, pl.ds(h * LANES, c)], vm[i]) for i, (bi, h) in enumerate(probs)]
        sd = [_dot_tn(kd_ref[bi, rows, sls[i]], vm[i]) for i, (bi, h) in enumerate(probs)]
        for i, (bi, h) in enumerate(probs):
            s_scr[bi * GDN_HEADS + h] = st[i] * egl_ref[bi, ch, :, sls[i]] + sd[i]
            o = r1[i][c:] + o2[i]
            o_ref[bi, rows, sls[i]] = (_rms(o, og, GDN_DV)
                                       * _silu(z_ref[bi, rows, sls[i]].astype(F32))).astype(o_ref.dtype)
        return 0

    lax.fori_loop(0, nc, chunk, 0)


def _gdn_scan(u, w, qd, kd, qk, egl, proj3, og, *, tt):
    b, s, hk = u.shape
    nc = tt // GDN_CHUNK
    blk = lambda width: pl.BlockSpec((b, tt, width), lambda i: (0, i, 0))
    return pl.pallas_call(
        functools.partial(_gdn_scan_body, nb=b, nc=nc),
        grid=(s // tt,),
        in_specs=[
            blk(hk), blk(hk), blk(hk), blk(hk), blk(GDN_HEADS * LANES),
            pl.BlockSpec((b, nc, 1, hk), lambda i: (0, i, 0, 0)),
            pl.BlockSpec((b, tt, hk), lambda i: (0, i, COL_Z // hk)),
            pl.BlockSpec((1, GDN_DV), lambda i: (0, 0)),
        ],
        out_specs=blk(hk),
        out_shape=jax.ShapeDtypeStruct((b, s, hk), MXU_DTYPE),
        scratch_shapes=[pltpu.VMEM((b * GDN_HEADS, GDN_DK, GDN_DV), F32)],
        compiler_params=_params(("arbitrary",)),
        name="gdn_scan",
    )(u, w, qd, kd, qk, egl, proj3, og)


def _merge_body(x_ref, om_ref, og_ref, ob_ref, gl_ref, wb_ref, wo_ref, o_ref):
    mixed = None
    for n, br in enumerate((om_ref, og_ref, ob_ref)):
        up = _dot(br[...], wb_ref[n].astype(MXU_DTYPE))
        term = jax.nn.sigmoid(gl_ref[:, n * D_MODEL:(n + 1) * D_MODEL].astype(F32)) * up
        mixed = term if mixed is None else mixed + term
    o_ref[...] = x_ref[...] + _dot(mixed.astype(MXU_DTYPE), wo_ref[...].astype(MXU_DTYPE))


def _merge(x, o_mla, o_gdn, o_moba, proj, wb, wo, layer, *, tm):
    t, d = x.shape
    row = lambda w: pl.BlockSpec((tm, w), lambda i: (i, 0))
    return pl.pallas_call(
        _merge_body,
        grid=(t // tm,),
        in_specs=[
            row(d), row(MIX_WIDTH), row(MIX_WIDTH), row(MIX_WIDTH), row(3 * d),
            _resident((3, MIX_WIDTH, d), layer),
            _resident((d, d), layer),
        ],
        out_specs=row(d),
        out_shape=jax.ShapeDtypeStruct((t, d), F32),
        compiler_params=_params(("parallel",)),
        name="merge",
    )(x, o_mla, o_gdn, o_moba, proj, wb, wo)


def _rope_tables(s):
    pos = jnp.arange(s, dtype=jnp.int32).astype(F32)[:, None]

    def cs(d):
        half = d // 2
        inv_freq = ROPE_THETA ** (-jnp.arange(half, dtype=F32) * 2.0 / d)
        ang = pos * inv_freq[None, :]
        return jnp.cos(ang), jnp.sin(ang)

    def mla(c, sn, axis):
        n = lambda k: (s, k) if axis == 1 else (k, s)
        pad = LANES - MLA_QK
        return (jnp.concatenate([jnp.ones(n(MLA_NOPE), F32), c, c, jnp.zeros(n(pad), F32)], axis=axis),
                jnp.concatenate([jnp.zeros(n(MLA_NOPE), F32), sn, sn, jnp.zeros(n(pad), F32)], axis=axis))

    c, sn = cs(MLA_ROPE)
    mla_cos, mla_sin = mla(c, sn, 1)
    mla_cost, mla_sint = mla(c.T, sn.T, 0)
    c, sn = cs(MOBA_DH)
    moba_cos, moba_sin = jnp.tile(c, (1, 4)), jnp.tile(sn, (1, 4))
    moba_cost, moba_sint = jnp.tile(c.T, (4, 1)), jnp.tile(sn.T, (4, 1))
    return mla_cos, mla_sin, mla_cost, mla_sint, moba_cos, moba_sin, moba_cost, moba_sint


def _proj_weights(w_in):
    d = w_in.shape[0]
    o = np.cumsum((0, MLA_Q_RANK, MLA_KV_RANK, MLA_ROPE, 512, 512, 512, GDN_HEADS, GDN_HEADS, 512, 1536, 3 * D_MODEL))
    cut = lambda a, b: w_in[:, o[a]:o[b]].astype(MXU_DTYPE)
    ba = cut(6, 8)
    tail = jnp.zeros((d, LANES), MXU_DTYPE).at[:, :2 * GDN_HEADS].set(ba).at[:, MLA_NOPE:MLA_QK].set(cut(2, 3))
    groups = [cut(10, 11), cut(3, 6), cut(9, 10), cut(8, 9), cut(0, 1), cut(1, 2), tail]
    return groups, ba.T


def _lane_pad(v, n=LANES):
    return jnp.pad(v, (0, n - v.shape[0]))[None, :]


def kernel(x, ffa_norm, ffa_w_in, ffa_w_out, mix_norm, w_in, mla_cq_norm, mla_ckv_norm, mla_w_uq, mla_w_ukv, mla_q_norm, mla_k_norm, gdn_conv, gdn_a_log, gdn_dt_bias, gdn_out_norm, moba_q_norm, moba_k_norm, w_branch, w_out, ffb_norm, ffb_w_in, ffb_w_out):
    b, s, d = x.shape
    t = b * s
    assert d == D_MODEL and s % 2048 == 0 and s // MOBA_BLOCK <= MOBA_MAX_BLOCKS
    depth = ffa_norm.shape[0]
    tm = 512
    mla_cos, mla_sin, mla_cost, mla_sint, moba_cos, moba_sin, moba_cost, moba_sint = _rope_tables(s)
    lane_bcast = lambda v: jnp.broadcast_to(v[:, None], (LANES, LANES))
    x = x.reshape(t, d)
    for l in range(depth):
        x = _ffn(x, ffa_norm[l][None, :], ffa_w_in, ffa_w_out, l,
                 tm=tm, tf=256)

        w_proj, w_bat = _proj_weights(w_in[l])
        proj, tail, bat = _inproj(x, mix_norm[l][None, :], gdn_conv[l], w_proj, w_bat, s=s, tm=tm, tn=512,
                                  conv_group=1)

        wq = jnp.pad(mla_w_uq[l].reshape(MLA_Q_RANK, MLA_HEADS, MLA_QK),
                     ((0, 0), (0, 0), (0, LANES - MLA_QK))).reshape(MLA_Q_RANK, MLA_HEADS * LANES)
        wkv = mla_w_ukv[l].reshape(MLA_KV_RANK, MLA_HEADS, MLA_NOPE + MLA_V)
        wk = jnp.pad(wkv[:, :, :MLA_NOPE], ((0, 0), (0, 0), (0, LANES - MLA_NOPE))).reshape(MLA_KV_RANK, MLA_HEADS * LANES)
        wv = wkv[:, :, MLA_NOPE:].reshape(MLA_KV_RANK, MLA_HEADS * MLA_V).T
        q, k, v = _mla_prep(proj, tail, mla_cq_norm[l][None, :], mla_ckv_norm[l][None, :], wq.T.astype(MXU_DTYPE),
                            wk.astype(MXU_DTYPE), wv.astype(MXU_DTYPE), lane_bcast(_lane_pad(mla_q_norm[l])[0]),
                            _lane_pad(mla_k_norm[l]), mla_cost, mla_sint, mla_cos, mla_sin, b=b, s=s, tm=tm)
        o_mla = _flash(q, k, v, tq=512)

        bat_chunks = bat.reshape(8, t // GDN_CHUNK, GDN_CHUNK).transpose(1, 0, 2)
        head_pad = lambda v: jnp.pad(v, (GDN_HEADS, LANES - 2 * GDN_HEADS))[None, :]
        head_col = lambda v: jnp.pad(v, (GDN_HEADS, 0))[:, None]
        u, w, qd, kd, qk, egl = _gdn_prep(proj, tail, bat_chunks, head_pad(gdn_a_log[l]), head_pad(gdn_dt_bias[l]),
                                          head_col(gdn_a_log[l]), head_col(gdn_dt_bias[l]), tm=tm)
        r3 = lambda a: a.reshape(b, s, a.shape[-1])
        o_gdn = _gdn_scan(r3(u), r3(w), r3(qd), r3(kd), r3(qk), egl.reshape(b, s // GDN_CHUNK, 1, -1),
                          r3(proj), gdn_out_norm[l][None, :], tt=tm)

        q, k, v = _moba_prep(proj, lane_bcast(jnp.tile(moba_q_norm[l], 2)), jnp.tile(moba_k_norm[l], 2)[None, :],
                             moba_cost, moba_sint, moba_cos, moba_sin, b=b, s=s)
        o_moba = _flash(q, k, v, tq=512)

        x = _merge(x, o_mla.reshape(t, -1), o_gdn.reshape(t, -1), o_moba.reshape(t, -1), proj,
                   w_branch, w_out, l, tm=tm)

        x = _ffn(x, ffb_norm[l][None, :], ffb_w_in, ffb_w_out, l,
                 tm=tm, tf=256)
    return x.reshape(b, s, d)
```

```python
import functools

import numpy as np
import jax
import jax.numpy as jnp
from jax import lax
from jax.experimental import pallas as pl
from jax.experimental.pallas import tpu as pltpu

F32 = jnp.float32
BF16 = jnp.bfloat16
MXU_DTYPE = BF16
HIGHEST = lax.Precision.HIGHEST

EPS = 1e-6
ROPE_THETA = 10000.0
NEG_INF = -1e30
SENTINEL = -3e38
LOG2E = 1.4426950408889634

D_MODEL = 1024
D_FF = 2816
MIX_WIDTH = 512
LANES = 128

MLA_HEADS = 8
MLA_Q_RANK = 256
MLA_KV_RANK = 128
MLA_NOPE = 64
MLA_ROPE = 32
MLA_V = 64
MLA_QK = MLA_NOPE + MLA_ROPE

GDN_HEADS = 4
GDN_DK = 128
GDN_DV = 128
GDN_CONV = 4
GDN_CHUNK = 64
GDN_HALO = 8
GDN_GROUP = 8

MOBA_HEADS = 8
MOBA_DH = 64
MOBA_BLOCK = 256
MOBA_TOPK = 3
MOBA_MAX_BLOCKS = 32
V_ROWS = 80

COL_GATE = 0
COL_GDN = 3072
COL_MOBA = 4608
COL_Z = 6144
COL_MLA = 6656
N_PROJ = 7168

VMEM_LIMIT = 56 * 1024 * 1024


def _dot(a, b, precision=None):
    return jnp.dot(a, b, preferred_element_type=F32, precision=precision)


def _dot_nt(a, b, precision=None):
    return lax.dot_general(a, b, (((1,), (1,)), ((), ())), preferred_element_type=F32,
                           precision=precision)


def _dot_tn(a, b, precision=None):
    return lax.dot_general(a, b, (((0,), (0,)), ((), ())), preferred_element_type=F32,
                           precision=precision)


def _rms(x, gain, n):
    ms = jnp.sum(x * x, axis=-1, keepdims=True) * (1.0 / n)
    return x * lax.rsqrt(ms + EPS) * gain


def _silu(x):
    return x * jax.nn.sigmoid(x)


def _softplus(x):
    return jnp.maximum(x, 0.0) + jnp.log1p(jnp.exp(-jnp.abs(x)))


def _params(sem):
    return pltpu.CompilerParams(dimension_semantics=sem, vmem_limit_bytes=VMEM_LIMIT)


def _ffn_body(x_ref, g_ref, wi_ref, wo_ref, o_ref, act_scr, *, tf):
    x = x_ref[...]
    h = _rms(x, g_ref[...], D_MODEL).astype(MXU_DTYPE)
    for k in range(D_FF // tf):
        gate = _dot(h, wi_ref[:, k * tf:(k + 1) * tf].astype(MXU_DTYPE))
        up = _dot(h, wi_ref[:, D_FF + k * tf:D_FF + (k + 1) * tf].astype(MXU_DTYPE))
        act_scr[:, k * tf:(k + 1) * tf] = (_silu(gate) * up).astype(MXU_DTYPE)
    o_ref[...] = x + 0.5 * _dot(act_scr[...], wo_ref[...].astype(MXU_DTYPE))


def _resident(shape, layer=None):
    if layer is None:
        return pl.BlockSpec(shape, lambda *_: (0,) * len(shape), pipeline_mode=pl.Buffered(1))
    return pl.BlockSpec((None,) + tuple(shape), lambda *_: (layer,) + (0,) * len(shape),
                        pipeline_mode=pl.Buffered(1))


def _ffn(x, gain, w_in, w_out, layer, *, tm, tf):
    t, d = x.shape
    return pl.pallas_call(
        functools.partial(_ffn_body, tf=tf),
        grid=(t // tm,),
        in_specs=[
            pl.BlockSpec((tm, d), lambda i: (i, 0)),
            _resident((1, d)),
            _resident((d, 2 * D_FF), layer),
            _resident((D_FF, d), layer),
        ],
        out_specs=pl.BlockSpec((tm, d), lambda i: (i, 0)),
        out_shape=jax.ShapeDtypeStruct((t, d), F32),
        scratch_shapes=[pltpu.VMEM((tm, D_FF), MXU_DTYPE)],
        compiler_params=_params(("parallel",)),
        name="ffn",
    )(x, gain, w_in, w_out)


def _inproj_body(x_ref, g_ref, conv_ref, *refs, widths, tn, conv_group, tiles_per_seq):
    w_refs, wbat_ref = refs[:len(widths)], refs[len(widths)]
    o_ref, tail_ref, bat_ref, halo_scr = refs[len(widths) + 1:]
    tm = x_ref.shape[0]
    seq_start = pl.program_id(0) % tiles_per_seq == 0
    h = _rms(x_ref[...], g_ref[...], D_MODEL).astype(MXU_DTYPE)
    bat_ref[...] = _dot_nt(wbat_ref[...], h)
    off = 0
    for gi, (w_ref, width) in enumerate(zip(w_refs, widths)):
        for a in range(0, width, tn):
            n = min(tn, width - a)
            r = _dot(h, w_ref[:, a:a + n])
            if gi == conv_group:
                cw = conv_ref[:, a:a + n]
                halo = jnp.where(seq_start, 0.0, halo_scr[:, a:a + n])
                halo_scr[:, a:a + n] = r[tm - GDN_HALO:, :]
                xe = jnp.concatenate([halo, r], axis=0)
                y = r * cw[GDN_CONV - 1:GDN_CONV]
                for back in range(1, GDN_CONV):
                    y = y + pltpu.roll(xe, back, 0)[GDN_HALO:] * cw[GDN_CONV - 1 - back:GDN_CONV - back]
                r = _silu(y)
            o_ref[:, off + a:off + a + n] = r.astype(o_ref.dtype)
        off += width
    tail_ref[...] = r


def _inproj(x, gain, conv, w_groups, wbat, *, s, tm, tn, conv_group):
    t, d = x.shape
    widths = tuple(w.shape[1] for w in w_groups)
    assert sum(widths) == N_PROJ and widths[-1] == LANES and widths[conv_group] == conv.shape[1]
    return pl.pallas_call(
        functools.partial(_inproj_body, widths=widths, tn=tn, conv_group=conv_group, tiles_per_seq=s // tm),
        grid=(t // tm,),
        in_specs=[pl.BlockSpec((tm, d), lambda i: (i, 0)), _resident((1, d)), _resident(conv.shape)]
                 + [_resident((d, w)) for w in widths] + [_resident((8, d))],
        out_specs=[
            pl.BlockSpec((tm, N_PROJ), lambda i: (i, 0)),
            pl.BlockSpec((tm, LANES), lambda i: (i, 0)),
            pl.BlockSpec((8, tm), lambda i: (0, i)),
        ],
        out_shape=[jax.ShapeDtypeStruct((t, N_PROJ), MXU_DTYPE), jax.ShapeDtypeStruct((t, LANES), F32),
                   jax.ShapeDtypeStruct((8, t), F32)],
        scratch_shapes=[pltpu.VMEM((GDN_HALO, conv.shape[1]), F32)],
        compiler_params=_params(("arbitrary",)),
        name="inproj",
    )(x, gain, conv, *w_groups, wbat)


def _mla_rot(x, lane):
    return jnp.where(lane < MLA_NOPE + MLA_ROPE // 2, -pltpu.roll(x, LANES - MLA_ROPE // 2, 1),
                     pltpu.roll(x, MLA_ROPE // 2, 1))


def _lane_tile(x, n):
    return x if n == 1 else jnp.concatenate([x] * n, axis=1)


def _value_rows(v_ref, h, vt_h):
    n = vt_h.shape[1]
    v_ref[0, h, 0:MLA_V, :] = vt_h.astype(v_ref.dtype)
    pad_rows = lax.broadcasted_iota(jnp.int32, (V_ROWS - MLA_V, n), 0)
    v_ref[0, h, MLA_V:V_ROWS, :] = jnp.where(pad_rows == 0, 1.0, 0.0).astype(v_ref.dtype)


def _mla_prep_body(lat_ref, tail_ref, cqg_ref, ckvg_ref, wqt_ref, wk_ref, wvt_ref, qgt_ref, kg_ref, cost_ref,
                   sint_ref, cos_ref, sin_ref, q_ref, k_ref, v_ref, *, scale):
    lat = lat_ref[...].astype(F32)
    tm = lat.shape[0]
    cqn = _rms(lat[:, :MLA_Q_RANK], cqg_ref[...], MLA_Q_RANK).astype(MXU_DTYPE)
    ckvn = _rms(lat[:, MLA_Q_RANK:MLA_Q_RANK + MLA_KV_RANK], ckvg_ref[...], MLA_KV_RANK).astype(MXU_DTYPE)

    qt_all = _dot_nt(wqt_ref[...], cqn)
    qgt = _lane_tile(qgt_ref[...], tm // LANES)
    cost = cost_ref[...]
    sint = sint_ref[...]
    half = MLA_ROPE // 2
    for h in range(MLA_HEADS):
        x = qt_all[h * LANES:(h + 1) * LANES, :]
        ms = jnp.sum(x * x, axis=0, keepdims=True) * (1.0 / MLA_QK)
        qn = x * lax.rsqrt(ms + EPS) * qgt
        rot = jnp.concatenate([jnp.zeros((MLA_NOPE, tm), F32), -qn[MLA_NOPE + half:MLA_QK],
                               qn[MLA_NOPE:MLA_NOPE + half], jnp.zeros((LANES - MLA_QK, tm), F32)], axis=0)
        q_ref[0, h] = ((qn * cost + rot * sint) * scale).astype(q_ref.dtype)

    lane = lax.broadcasted_iota(jnp.int32, (tm, LANES), 1)
    kr = jnp.where((lane >= MLA_NOPE) & (lane < MLA_QK), tail_ref[...], 0.0)
    kg = kg_ref[...]
    cos = cos_ref[...]
    rot_kr_sin = _mla_rot(kr * kg, lane) * sin_ref[...]
    k_all = _dot(ckvn, wk_ref[...])
    for h in range(MLA_HEADS):
        kh = k_all[:, h * LANES:(h + 1) * LANES] + kr
        r = lax.rsqrt(jnp.sum(kh * kh, axis=-1, keepdims=True) * (1.0 / MLA_QK) + EPS)
        k_ref[0, h] = (r * (kh * kg * cos + rot_kr_sin)).astype(k_ref.dtype)

    vt = _dot_nt(wvt_ref[...], ckvn)
    for h in range(MLA_HEADS):
        _value_rows(v_ref, h, vt[h * MLA_V:(h + 1) * MLA_V, :])


def _mla_prep(proj, tail, cqg, ckvg, wqt, wk, wvt, qgt, kg, cost, sint, cos, sin, *, b, s, tm):
    nt = s // tm
    scale = (MLA_QK ** -0.5) * LOG2E
    cblk = COL_MLA // 512
    full = lambda shape: pl.BlockSpec(shape, lambda bi, i: (0,) * len(shape))
    return pl.pallas_call(
        functools.partial(_mla_prep_body, scale=scale),
        grid=(b, nt),
        in_specs=[
            pl.BlockSpec((tm, 512), lambda bi, i: (bi * nt + i, cblk)),
            pl.BlockSpec((tm, LANES), lambda bi, i: (bi * nt + i, 0)),
            full((1, MLA_Q_RANK)), full((1, MLA_KV_RANK)),
            full((MLA_HEADS * LANES, MLA_Q_RANK)), full((MLA_KV_RANK, MLA_HEADS * LANES)),
            full((MIX_WIDTH, MLA_KV_RANK)), full((LANES, LANES)), full((1, LANES)),
            pl.BlockSpec((LANES, tm), lambda bi, i: (0, i)),
            pl.BlockSpec((LANES, tm), lambda bi, i: (0, i)),
            pl.BlockSpec((tm, LANES), lambda bi, i: (i, 0)),
            pl.BlockSpec((tm, LANES), lambda bi, i: (i, 0)),
        ],
        out_specs=[
            pl.BlockSpec((1, MLA_HEADS, LANES, tm), lambda bi, i: (bi, 0, 0, i)),
            pl.BlockSpec((1, MLA_HEADS, tm, LANES), lambda bi, i: (bi, 0, i, 0)),
            pl.BlockSpec((1, MLA_HEADS, V_ROWS, tm), lambda bi, i: (bi, 0, 0, i)),
        ],
        out_shape=[
            jax.ShapeDtypeStruct((b, MLA_HEADS, LANES, s), MXU_DTYPE),
            jax.ShapeDtypeStruct((b, MLA_HEADS, s, LANES), MXU_DTYPE),
            jax.ShapeDtypeStruct((b, MLA_HEADS, V_ROWS, s), MXU_DTYPE),
        ],
        compiler_params=_params(("parallel", "parallel")),
        name="mla_prep",
    )(proj, tail, cqg, ckvg, wqt, wk, wvt, qgt, kg, cost, sint, cos, sin)


def _moba_rot(x, lane):
    half = MOBA_DH // 2
    return jnp.where(lane % MOBA_DH < half, -pltpu.roll(x, LANES - half, 1), pltpu.roll(x, half, 1))


def _pair_rms(x, gain, lo_mask):
    sq = x * x
    ss_lo = jnp.sum(jnp.where(lo_mask, sq, 0.0), axis=-1, keepdims=True)
    ss_hi = jnp.sum(jnp.where(lo_mask, 0.0, sq), axis=-1, keepdims=True)
    r = jnp.where(lo_mask, lax.rsqrt(ss_lo * (1.0 / MOBA_DH) + EPS), lax.rsqrt(ss_hi * (1.0 / MOBA_DH) + EPS))
    return x * r * gain


def _select_bias(gate, blk, blk_f, j):
    g = jnp.where(blk < j, gate, SENTINEL)
    allowed = blk == j
    for _ in range(MOBA_TOPK):
        m = jnp.max(g, axis=0, keepdims=True)
        first = jnp.min(jnp.where(g == m, blk_f, 1e9), axis=0, keepdims=True)
        pick = (blk_f == first) & (m > SENTINEL)
        allowed = allowed | pick
        g = jnp.where(pick, SENTINEL, g)
    return jnp.where(allowed, 0.0, NEG_INF)


def _moba_prep_body(qkv_ref, qgt_ref, kg_ref, cost_ref, sint_ref, cos_ref, sin_ref, q_ref, k_ref, v_ref, kmt_scr,
                    *, scale):
    j = pl.program_id(1)

    @pl.when(j == 0)
    def _():
        kmt_scr[...] = jnp.zeros_like(kmt_scr)

    tq = qkv_ref.shape[0]
    hw = MOBA_HEADS * MOBA_DH
    dh, half, nblk = MOBA_DH, MOBA_DH // 2, MOBA_MAX_BLOCKS

    vt = qkv_ref[:, 2 * hw:].astype(F32).T
    for h in range(MOBA_HEADS):
        _value_rows(v_ref, h, vt[h * dh:(h + 1) * dh, :])

    qt_all = qkv_ref[:, :hw].astype(F32).T
    qgt = _lane_tile(qgt_ref[...], tq // LANES)
    cost = cost_ref[...]
    sint = sint_ref[...]
    blk = lax.broadcasted_iota(jnp.int32, (nblk, tq), 0)
    blk_f = blk.astype(F32)
    zeros = jnp.zeros((nblk, tq), F32)

    lane = lax.broadcasted_iota(jnp.int32, (tq, LANES), 1)
    lo_mask = lane < dh
    cos = cos_ref[...]
    sin = sin_ref[...]
    row_lane = lax.broadcasted_iota(jnp.int32, (1, LANES), 1)
    slot = lax.broadcasted_iota(jnp.int32, (LANES, LANES), 0)
    for p in range(MOBA_HEADS // 2):
        x = qt_all[p * LANES:(p + 1) * LANES, :]
        sq = x * x
        r_e = lax.rsqrt(jnp.sum(sq[:dh], axis=0, keepdims=True) * (1.0 / dh) + EPS)
        r_o = lax.rsqrt(jnp.sum(sq[dh:], axis=0, keepdims=True) * (1.0 / dh) + EPS)
        qn = jnp.concatenate([x[:dh] * r_e, x[dh:] * r_o], axis=0) * qgt
        rot = jnp.concatenate([-qn[half:dh], qn[:half], -qn[dh + half:], qn[dh:dh + half]], axis=0)
        q = qn * cost + rot * sint
        km_hi, km_lo = _hi_lo(kmt_scr[p])
        q_hi, q_lo = _hi_lo(q)
        gate = _dot(km_hi, q_hi) + (_dot(km_hi, q_lo) + _dot(km_lo, q_hi))
        bias_e = _select_bias(gate[dh:dh + nblk], blk, blk_f, j)
        bias_o = _select_bias(gate[:nblk], blk, blk_f, j)
        qs = q * scale
        q_ref[0, 2 * p] = jnp.concatenate([qs[:dh], bias_e, zeros], axis=0).astype(q_ref.dtype)
        q_ref[0, 2 * p + 1] = jnp.concatenate([bias_o, zeros, qs[dh:]], axis=0).astype(q_ref.dtype)

        kn = _pair_rms(qkv_ref[:, hw + p * LANES:hw + (p + 1) * LANES].astype(F32), kg_ref[...], lo_mask)
        k = kn * cos + _moba_rot(kn, lane) * sin
        k_ref[0, 2 * p] = jnp.where(lo_mask, k, jnp.where(lane == dh + j, 1.0, 0.0)).astype(k_ref.dtype)
        k_ref[0, 2 * p + 1] = jnp.where(lo_mask, jnp.where(lane == j, 1.0, 0.0), k).astype(k_ref.dtype)
        km = jnp.sum(k, axis=0, keepdims=True) * (1.0 / MOBA_BLOCK)
        new_rows = jnp.where(slot == dh + j, jnp.where(row_lane < dh, km, 0.0),
                             jnp.where(row_lane < dh, 0.0, km))
        kmt_scr[p] = jnp.where((slot == dh + j) | (slot == j), new_rows, kmt_scr[p])


def _moba_prep(proj, qgt, kg, cost, sint, cos, sin, *, b, s):
    tq = MOBA_BLOCK
    nt = s // tq
    scale = (MOBA_DH ** -0.5) * LOG2E
    cblk = COL_MOBA // (3 * MOBA_HEADS * MOBA_DH)
    return pl.pallas_call(
        functools.partial(_moba_prep_body, scale=scale),
        grid=(b, nt),
        in_specs=[
            pl.BlockSpec((tq, 3 * MOBA_HEADS * MOBA_DH), lambda bi, j: (bi * nt + j, cblk)),
            pl.BlockSpec((LANES, LANES), lambda bi, j: (0, 0)),
            pl.BlockSpec((1, LANES), lambda bi, j: (0, 0)),
            pl.BlockSpec((LANES, tq), lambda bi, j: (0, j)),
            pl.BlockSpec((LANES, tq), lambda bi, j: (0, j)),
            pl.BlockSpec((tq, LANES), lambda bi, j: (j, 0)),
            pl.BlockSpec((tq, LANES), lambda bi, j: (j, 0)),
        ],
        out_specs=[
            pl.BlockSpec((1, MOBA_HEADS, LANES, tq), lambda bi, j: (bi, 0, 0, j)),
            pl.BlockSpec((1, MOBA_HEADS, tq, LANES), lambda bi, j: (bi, 0, j, 0)),
            pl.BlockSpec((1, MOBA_HEADS, V_ROWS, tq), lambda bi, j: (bi, 0, 0, j)),
        ],
        out_shape=[
            jax.ShapeDtypeStruct((b, MOBA_HEADS, LANES, s), MXU_DTYPE),
            jax.ShapeDtypeStruct((b, MOBA_HEADS, s, LANES), MXU_DTYPE),
            jax.ShapeDtypeStruct((b, MOBA_HEADS, V_ROWS, s), MXU_DTYPE),
        ],
        scratch_shapes=[pltpu.VMEM((MOBA_HEADS // 2, LANES, LANES), F32)],
        compiler_params=_params(("parallel", "arbitrary")),
        name="moba_prep",
    )(proj, qgt, kg, cost, sint, cos, sin)


def _flash_body(qt_ref, k_ref, vt_ref, o_ref, diag, ev0, ev1, od0, od1, *, tq, n_tiles):
    def q_tile(i):
        qs = pl.ds(pl.multiple_of(i * tq, tq), tq)
        return (qt_ref[0, 0, :, qs], qt_ref[0, 1, :, qs])

    def scores(qts, t, buf, hh, diagonal=False):
        ks = pl.ds(pl.multiple_of(t * tq, tq), tq)
        s = _dot(k_ref[0, hh, ks, :], qts[hh])
        if diagonal:
            key = lax.broadcasted_iota(jnp.int32, s.shape, 0)
            qry = lax.broadcasted_iota(jnp.int32, s.shape, 1)
            s = jnp.where(key <= qry, s, NEG_INF)
        buf[hh] = s
        return jnp.max(s, axis=0, keepdims=True)

    def diag_scores(i, buf):
        qts = q_tile(i)
        return tuple(scores(qts, i, buf, hh, True) for hh in range(2))

    def consume(t, buf, hh, smax, stat):
        ks = pl.ds(pl.multiple_of(t * tq, tq), tq)
        m, acc = stat
        m_new = jnp.maximum(m, smax)
        p = jnp.exp2(buf[hh] - m_new)
        return m_new, jnp.exp2(m - m_new) * acc + _dot(vt_ref[0, hh, :, ks], p.astype(MXU_DTYPE))

    def step(qts, t_next, buf_next, t_cur, buf_cur, carry):
        smax, stats = carry
        new_smax, new_stats = [], []
        for hh in range(2):
            new_smax.append(scores(qts, t_next, buf_next, hh))
            new_stats.append(consume(t_cur, buf_cur, hh, smax[hh], stats[hh]))
        return tuple(new_smax), tuple(new_stats)

    def run(qts, i, t_first, carry, bufs):
        n_steps = i - t_first

        def quad(u, carry):
            t0 = t_first + 4 * u
            carry = step(qts, t0, bufs[0], t0 - 1, bufs[1], carry)
            carry = step(qts, t0 + 1, bufs[1], t0, bufs[0], carry)
            carry = step(qts, t0 + 2, bufs[0], t0 + 1, bufs[1], carry)
            return step(qts, t0 + 3, bufs[1], t0 + 2, bufs[0], carry)

        carry = lax.fori_loop(0, n_steps // 4, quad, carry)
        t_pair = t_first + 4 * (n_steps // 4)

        def pair(u, carry):
            carry = step(qts, t_pair, bufs[0], t_pair - 1, bufs[1], carry)
            return step(qts, t_pair + 1, bufs[1], t_pair, bufs[0], carry)

        return lax.fori_loop(0, (n_steps % 4) // 2, pair, carry)

    def finish(i, t_last, buf, carry):
        smax, stats = carry
        out = [consume(t_last, buf, hh, smax[hh], stats[hh])[1] for hh in range(2)]
        o_t = jnp.concatenate([a[:MLA_V] / a[MLA_V:MLA_V + 1] for a in out], axis=0)
        o_ref[0, pl.ds(pl.multiple_of(i * tq, tq), tq), :] = o_t.T.astype(o_ref.dtype)

    init = ((jnp.full((1, tq), NEG_INF, F32), jnp.zeros((V_ROWS, tq), F32)),) * 2

    smax0 = diag_scores(0, diag)
    smax1 = diag_scores(1, ev0)
    finish(0, 0, diag, (smax0, init))
    carry = step(q_tile(1), 0, ev1, 1, ev0, (smax1, init))
    smax2 = diag_scores(2, diag)
    finish(1, 0, ev1, carry)

    def tile_pair(j, smax_even):
        i = 2 * j
        qts = q_tile(i)
        carry = step(qts, 0, ev0, i, diag, (smax_even, init))
        carry = step(qts, 1, ev1, 0, ev0, carry)
        carry = run(qts, i, 2, carry, (ev0, ev1))
        smax_odd = diag_scores(i + 1, diag)
        finish(i, i - 1, ev1, carry)

        i = 2 * j + 1
        qts = q_tile(i)
        carry = step(qts, 0, od0, i, diag, (smax_odd, init))
        carry = step(qts, 1, od1, 0, od0, carry)
        carry = step(qts, 2, od0, 1, od1, carry)
        carry = run(qts, i, 3, carry, (od1, od0))
        smax_next = diag_scores(jnp.minimum(i + 1, n_tiles - 1), diag)
        finish(i, i - 1, od0, carry)
        return smax_next

    lax.fori_loop(1, n_tiles // 2, tile_pair, smax2)


def _flash(qt, k, vt, *, tq):
    b, h, s, _ = k.shape
    n_tiles = s // tq
    assert n_tiles % 2 == 0 and n_tiles >= 4
    buf = pltpu.VMEM((2, tq, tq), F32)
    return pl.pallas_call(
        functools.partial(_flash_body, tq=tq, n_tiles=n_tiles),
        grid=(b, h // 2),
        in_specs=[
            pl.BlockSpec((1, 2, LANES, s), lambda bi, p: (bi, p, 0, 0)),
            pl.BlockSpec((1, 2, s, LANES), lambda bi, p: (bi, p, 0, 0)),
            pl.BlockSpec((1, 2, V_ROWS, s), lambda bi, p: (bi, p, 0, 0)),
        ],
        out_specs=pl.BlockSpec((1, s, LANES), lambda bi, p: (bi, 0, p)),
        out_shape=jax.ShapeDtypeStruct((b, s, MIX_WIDTH), MXU_DTYPE),
        scratch_shapes=[buf] * 5,
        compiler_params=_params(("parallel", "parallel")),
        name="flash",
    )(qt, k, vt)


def _hi_lo(x):
    hi = x.astype(MXU_DTYPE)
    return hi, (x - hi.astype(F32)).astype(MXU_DTYPE)


def _unit_lower_solve(a_list, rhs_list, eye):
    c = eye.shape[0]
    n = range(len(a_list))
    t = [eye - a for a in a_list]
    pk = []
    for a in a_list:
        pwm = (-a).astype(MXU_DTYPE)
        pk.append(_dot(pwm, pwm))
    n_sq = int(np.log2(c)) - 1
    for it in range(n_sq):
        if it < n_sq - 1:
            both = [_dot(jnp.concatenate([t[j], pk[j]], axis=0).astype(MXU_DTYPE), pk[j].astype(MXU_DTYPE))
                    for j in n]
            t = [t[j] + both[j][:c] for j in n]
            pk = [both[j][c:] for j in n]
        else:
            t = [t[j] + _dot(t[j].astype(MXU_DTYPE), pk[j].astype(MXU_DTYPE)) for j in n]
    tm = [x.astype(MXU_DTYPE) for x in t]
    x1 = [_dot(tm[j], rhs_list[j].astype(MXU_DTYPE)) for j in n]
    a_split = [_hi_lo(a) for a in a_list]
    x_split = [_hi_lo(x) for x in x1]
    ax = [_dot(jnp.concatenate(a_split[j], axis=0), x_split[j][0]) for j in n]
    ax_lo = [_dot(a_split[j][0], x_split[j][1]) for j in n]
    res = [rhs_list[j] - x1[j] - (ax[j][:c] + ax[j][c:] + ax_lo[j]) for j in n]
    return [x1[j] + _dot(tm[j], res[j].astype(MXU_DTYPE)) for j in n]


def _gdn_prep_body(qkv_ref, ba_ref, bat_ref, alog_ref, dtb_ref, alogt_ref, dtbt_ref,
                   u_ref, w_ref, qd_ref, kd_ref, qk_ref, egl_ref, bg_scr, *, nc):
    c = GDN_CHUNK
    hk = GDN_HEADS * GDN_DK

    ba = ba_ref[...]
    lane = lax.broadcasted_iota(jnp.int32, ba.shape, 1)
    g_col = -jnp.exp(alog_ref[...]) * _softplus(ba + dtb_ref[...])
    bg_scr[...] = jnp.where(lane < GDN_HEADS, jax.nn.sigmoid(ba), g_col)

    ri = lax.broadcasted_iota(jnp.int32, (c, c), 0)
    ci = lax.broadcasted_iota(jnp.int32, (c, c), 1)
    tril = ci <= ri
    strict = ci < ri
    eye = jnp.where(ci == ri, 1.0, 0.0)
    lower_ones = jnp.where(tril, 1.0, 0.0)
    upper_ones = jnp.where(ci >= ri, 1.0, 0.0)

    def chunk_group(grp, _):
        chs = [grp * GDN_GROUP + j for j in range(GDN_GROUP)]
        rows = [pl.ds(pl.multiple_of(ch * c, c), c) for ch in chs]
        bg = [bg_scr[r, :] for r in rows]
        gcum_col = [_dot(lower_ones, x, precision=HIGHEST) for x in bg]
        g_row = [-jnp.exp(alogt_ref[...]) * _softplus(bat_ref[ch] + dtbt_ref[...]) for ch in chs]
        gcum_row = [_dot(x, upper_ones, precision=HIGHEST) for x in g_row]
        probs = [(j, h) for j in range(GDN_GROUP) for h in range(GDN_HEADS)]
        n = range(len(probs))
        sls = [slice(h * GDN_DK, (h + 1) * GDN_DK) for _, h in probs]
        qh = [qkv_ref[rows[j], sls[i]].astype(F32) for i, (j, h) in enumerate(probs)]
        kh = [qkv_ref[rows[j], pl.ds(hk + h * GDN_DK, GDN_DK)].astype(F32) for j, h in probs]
        vh = [qkv_ref[rows[j], pl.ds(2 * hk + h * GDN_DV, GDN_DV)].astype(F32) for j, h in probs]
        qn = [x * lax.rsqrt(jnp.sum(x * x, axis=-1, keepdims=True) + EPS) * (GDN_DK ** -0.5) for x in qh]
        kn = [x * lax.rsqrt(jnp.sum(x * x, axis=-1, keepdims=True) + EPS) for x in kh]
        beta = [bg[j][:, h:h + 1] for j, h in probs]
        gc = [gcum_col[j][:, GDN_HEADS + h:GDN_HEADS + h + 1] for j, h in probs]
        gr = [gcum_row[j][GDN_HEADS + h:GDN_HEADS + h + 1, :] for j, h in probs]
        kb = [kn[i] * beta[i] for i in n]
        knm = [x.astype(MXU_DTYPE) for x in kn]
        dec = [jnp.exp(jnp.where(tril, gc[i] - gr[i], NEG_INF)) for i in n]
        skk = [_dot_nt(kb[i].astype(MXU_DTYPE), knm[i]) for i in n]
        sqk = [_dot_nt(qn[i].astype(MXU_DTYPE), knm[i]) for i in n]
        a = [jnp.where(strict, skk[i] * dec[i], 0.0) for i in n]
        eg = [jnp.exp(x) for x in gc]
        rhs = [jnp.concatenate([vh[i] * beta[i], kb[i] * eg[i]], axis=1) for i in n]
        sol = _unit_lower_solve(a, rhs, eye)
        for i, (j, h) in enumerate(probs):
            sl, r = sls[i], rows[j]
            gl = gc[i][c - 1:c, :]
            u_ref[r, sl] = sol[i][:, :GDN_DV]
            w_ref[r, sl] = sol[i][:, GDN_DV:].astype(w_ref.dtype)
            qd_ref[r, sl] = (qn[i] * eg[i]).astype(qd_ref.dtype)
            kd_ref[r, sl] = (kn[i] * jnp.exp(gl - gc[i])).astype(kd_ref.dtype)
            qk_ref[r, pl.ds(h * LANES, c)] = (sqk[i] * dec[i]).astype(qk_ref.dtype)
            qk_ref[r, pl.ds(h * LANES + c, LANES - c)] = jnp.zeros((c, LANES - c), qk_ref.dtype)
            egl_ref[chs[j], :, sl] = jnp.broadcast_to(jnp.exp(gl), (1, GDN_DV))
        return 0

    lax.fori_loop(0, nc // GDN_GROUP, chunk_group, 0)


def _gdn_prep(proj, tail, bat, alog, dtb, alogt, dtbt, *, tm):
    t = proj.shape[0]
    nc = tm // GDN_CHUNK
    hk = GDN_HEADS * GDN_DK
    width = 3 * hk
    cblk = COL_GDN // width
    full = lambda shape: pl.BlockSpec(shape, lambda i: (0,) * len(shape))
    row = lambda w: pl.BlockSpec((tm, w), lambda i: (i, 0))
    return pl.pallas_call(
        functools.partial(_gdn_prep_body, nc=nc),
        grid=(t // tm,),
        in_specs=[
            pl.BlockSpec((tm, width), lambda i: (i, cblk)),
            pl.BlockSpec((tm, LANES), lambda i: (i, 0)),
            pl.BlockSpec((nc, 8, GDN_CHUNK), lambda i: (i, 0, 0)),
            full((1, LANES)), full((1, LANES)), full((8, 1)), full((8, 1)),
        ],
        out_specs=[row(hk), row(hk), row(hk), row(hk), row(GDN_HEADS * LANES),
                   pl.BlockSpec((nc, 1, hk), lambda i: (i, 0, 0))],
        out_shape=[
            jax.ShapeDtypeStruct((t, hk), F32),
            jax.ShapeDtypeStruct((t, hk), MXU_DTYPE),
            jax.ShapeDtypeStruct((t, hk), MXU_DTYPE),
            jax.ShapeDtypeStruct((t, hk), MXU_DTYPE),
            jax.ShapeDtypeStruct((t, GDN_HEADS * LANES), MXU_DTYPE),
            jax.ShapeDtypeStruct((t // GDN_CHUNK, 1, hk), F32),
        ],
        scratch_shapes=[pltpu.VMEM((tm, LANES), F32)],
        compiler_params=_params(("parallel",)),
        name="gdn_prep",
    )(proj, tail, bat, alog, dtb, alogt, dtbt)


def _gdn_scan_body(u_ref, w_ref, qd_ref, kd_ref, qk_ref, egl_ref, z_ref, og_ref, o_ref, s_scr, *, nb, nc):
    @pl.when(pl.program_id(0) == 0)
    def _():
        s_scr[...] = jnp.zeros_like(s_scr)

    c = GDN_CHUNK
    og = og_ref[...]

    def chunk(ch, _):
        rows = pl.ds(pl.multiple_of(ch * c, c), c)
        probs = [(bi, h) for bi in range(nb) for h in range(GDN_HEADS)]
        n = range(len(probs))
        sls = [slice(h * GDN_DV, (h + 1) * GDN_DV) for _, h in probs]
        st = [s_scr[bi * GDN_HEADS + h] for bi, h in probs]
        stm = [x.astype(MXU_DTYPE) for x in st]
        wq = [jnp.concatenate([w_ref[bi, rows, sls[i]], qd_ref[bi, rows, sls[i]]], axis=0)
              for i, (bi, h) in enumerate(probs)]
        r1 = [_dot(wq[i], stm[i]) for i in n]
        vm = [(u_ref[bi, rows, sls[i]] - r1[i][:c]).astype(MXU_DTYPE) for i, (bi, h) in enumerate(probs)]
        o2 = [_dot(qk_ref[bi, rows, pl.ds(h * LANES, c)], vm[i]) for i, (bi, h) in enumerate(probs)]
        sd = [_dot_tn(kd_ref[bi, rows, sls[i]], vm[i]) for i, (bi, h) in enumerate(probs)]
        for i, (bi, h) in enumerate(probs):
            s_scr[bi * GDN_HEADS + h] = st[i] * egl_ref[bi, ch, :, sls[i]] + sd[i]
            o = r1[i][c:] + o2[i]
            o_ref[bi, rows, sls[i]] = (_rms(o, og, GDN_DV)
                                       * _silu(z_ref[bi, rows, sls[i]].astype(F32))).astype(o_ref.dtype)
        return 0

    lax.fori_loop(0, nc, chunk, 0, unroll=4)


def _gdn_scan(u, w, qd, kd, qk, egl, proj3, og, *, tt):
    b, s, hk = u.shape
    nc = tt // GDN_CHUNK
    blk = lambda width: pl.BlockSpec((b, tt, width), lambda i: (0, i, 0))
    return pl.pallas_call(
        functools.partial(_gdn_scan_body, nb=b, nc=nc),
        grid=(s // tt,),
        in_specs=[
            blk(hk), blk(hk), blk(hk), blk(hk), blk(GDN_HEADS * LANES),
            pl.BlockSpec((b, nc, 1, hk), lambda i: (0, i, 0, 0)),
            pl.BlockSpec((b, tt, hk), lambda i: (0, i, COL_Z // hk)),
            pl.BlockSpec((1, GDN_DV), lambda i: (0, 0)),
        ],
        out_specs=blk(hk),
        out_shape=jax.ShapeDtypeStruct((b, s, hk), MXU_DTYPE),
        scratch_shapes=[pltpu.VMEM((b * GDN_HEADS, GDN_DK, GDN_DV), F32)],
        compiler_params=_params(("arbitrary",)),
        name="gdn_scan",
    )(u, w, qd, kd, qk, egl, proj3, og)


def _merge_body(x_ref, om_ref, og_ref, ob_ref, gl_ref, wb_ref, wo_ref, o_ref):
    mixed = None
    for n, br in enumerate((om_ref, og_ref, ob_ref)):
        up = _dot(br[...], wb_ref[n].astype(MXU_DTYPE))
        term = jax.nn.sigmoid(gl_ref[:, n * D_MODEL:(n + 1) * D_MODEL].astype(F32)) * up
        mixed = term if mixed is None else mixed + term
    o_ref[...] = x_ref[...] + _dot(mixed.astype(MXU_DTYPE), wo_ref[...].astype(MXU_DTYPE))


def _merge(x, o_mla, o_gdn, o_moba, proj, wb, wo, layer, *, tm):
    t, d = x.shape
    row = lambda w: pl.BlockSpec((tm, w), lambda i: (i, 0))
    return pl.pallas_call(
        _merge_body,
        grid=(t // tm,),
        in_specs=[
            row(d), row(MIX_WIDTH), row(MIX_WIDTH), row(MIX_WIDTH), row(3 * d),
            _resident((3, MIX_WIDTH, d), layer),
            _resident((d, d), layer),
        ],
        out_specs=row(d),
        out_shape=jax.ShapeDtypeStruct((t, d), F32),
        compiler_params=_params(("parallel",)),
        name="merge",
    )(x, o_mla, o_gdn, o_moba, proj, wb, wo)


def _rope_tables(s):
    pos = np.arange(s, dtype=np.float64)[:, None]

    def cs(d):
        half = d // 2
        inv_freq = ROPE_THETA ** (-np.arange(half, dtype=np.float64) * 2.0 / d)
        ang = pos * inv_freq[None, :]
        return jnp.asarray(np.cos(ang), F32), jnp.asarray(np.sin(ang), F32)

    def mla(c, sn, axis):
        n = lambda k: (s, k) if axis == 1 else (k, s)
        pad = LANES - MLA_QK
        return (jnp.concatenate([jnp.ones(n(MLA_NOPE), F32), c, c, jnp.zeros(n(pad), F32)], axis=axis),
                jnp.concatenate([jnp.zeros(n(MLA_NOPE), F32), sn, sn, jnp.zeros(n(pad), F32)], axis=axis))

    c, sn = cs(MLA_ROPE)
    mla_cos, mla_sin = mla(c, sn, 1)
    mla_cost, mla_sint = mla(c.T, sn.T, 0)
    c, sn = cs(MOBA_DH)
    moba_cos, moba_sin = jnp.tile(c, (1, 4)), jnp.tile(sn, (1, 4))
    moba_cost, moba_sint = jnp.tile(c.T, (4, 1)), jnp.tile(sn.T, (4, 1))
    return mla_cos, mla_sin, mla_cost, mla_sint, moba_cos, moba_sin, moba_cost, moba_sint


def _proj_weights(w_in):
    d = w_in.shape[0]
    o = np.cumsum((0, MLA_Q_RANK, MLA_KV_RANK, MLA_ROPE, 512, 512, 512, GDN_HEADS, GDN_HEADS, 512, 1536, 3 * D_MODEL))
    cut = lambda a, b: w_in[:, o[a]:o[b]].astype(MXU_DTYPE)
    ba = cut(6, 8)
    tail = jnp.zeros((d, LANES), MXU_DTYPE).at[:, :2 * GDN_HEADS].set(ba).at[:, MLA_NOPE:MLA_QK].set(cut(2, 3))
    groups = [cut(10, 11), cut(3, 6), cut(9, 10), cut(8, 9), cut(0, 1), cut(1, 2), tail]
    return groups, ba.T


def _lane_pad(v, n=LANES):
    return jnp.pad(v, (0, n - v.shape[0]))[None, :]


def kernel(x, ffa_norm, ffa_w_in, ffa_w_out, mix_norm, w_in, mla_cq_norm, mla_ckv_norm, mla_w_uq, mla_w_ukv, mla_q_norm, mla_k_norm, gdn_conv, gdn_a_log, gdn_dt_bias, gdn_out_norm, moba_q_norm, moba_k_norm, w_branch, w_out, ffb_norm, ffb_w_in, ffb_w_out):
    b, s, d = x.shape
    t = b * s
    assert d == D_MODEL and s % 2048 == 0 and s // MOBA_BLOCK <= MOBA_MAX_BLOCKS
    depth = ffa_norm.shape[0]
    tm = 512
    mla_cos, mla_sin, mla_cost, mla_sint, moba_cos, moba_sin, moba_cost, moba_sint = _rope_tables(s)
    lane_bcast = lambda v: jnp.broadcast_to(v[:, None], (LANES, LANES))
    x = x.reshape(t, d)
    for l in range(depth):
        x = _ffn(x, ffa_norm[l][None, :], ffa_w_in, ffa_w_out, l,
                 tm=tm, tf=256)

        w_proj, w_bat = _proj_weights(w_in[l])
        proj, tail, bat = _inproj(x, mix_norm[l][None, :], gdn_conv[l], w_proj, w_bat, s=s, tm=tm, tn=512,
                                  conv_group=1)

        wq = jnp.pad(mla_w_uq[l].reshape(MLA_Q_RANK, MLA_HEADS, MLA_QK),
                     ((0, 0), (0, 0), (0, LANES - MLA_QK))).reshape(MLA_Q_RANK, MLA_HEADS * LANES)
        wkv = mla_w_ukv[l].reshape(MLA_KV_RANK, MLA_HEADS, MLA_NOPE + MLA_V)
        wk = jnp.pad(wkv[:, :, :MLA_NOPE], ((0, 0), (0, 0), (0, LANES - MLA_NOPE))).reshape(MLA_KV_RANK, MLA_HEADS * LANES)
        wv = wkv[:, :, MLA_NOPE:].reshape(MLA_KV_RANK, MLA_HEADS * MLA_V).T
        q, k, v = _mla_prep(proj, tail, mla_cq_norm[l][None, :], mla_ckv_norm[l][None, :], wq.T.astype(MXU_DTYPE),
                            wk.astype(MXU_DTYPE), wv.astype(MXU_DTYPE), lane_bcast(_lane_pad(mla_q_norm[l])[0]),
                            _lane_pad(mla_k_norm[l]), mla_cost, mla_sint, mla_cos, mla_sin, b=b, s=s, tm=tm)
        o_mla = _flash(q, k, v, tq=512)

        bat_chunks = bat.reshape(8, t // GDN_CHUNK, GDN_CHUNK).transpose(1, 0, 2)
        head_pad = lambda v: jnp.pad(v, (GDN_HEADS, LANES - 2 * GDN_HEADS))[None, :]
        head_col = lambda v: jnp.pad(v, (GDN_HEADS, 0))[:, None]
        u, w, qd, kd, qk, egl = _gdn_prep(proj, tail, bat_chunks, head_pad(gdn_a_log[l]), head_pad(gdn_dt_bias[l]),
                                          head_col(gdn_a_log[l]), head_col(gdn_dt_bias[l]), tm=tm)
        r3 = lambda a: a.reshape(b, s, a.shape[-1])
        o_gdn = _gdn_scan(r3(u), r3(w), r3(qd), r3(kd), r3(qk), egl.reshape(b, s // GDN_CHUNK, 1, -1),
                          r3(proj), gdn_out_norm[l][None, :], tt=tm)

        q, k, v = _moba_prep(proj, lane_bcast(jnp.tile(moba_q_norm[l], 2)), jnp.tile(moba_k_norm[l], 2)[None, :],
                             moba_cost, moba_sint, moba_cos, moba_sin, b=b, s=s)
        o_moba = _flash(q, k, v, tq=512)

        x = _merge(x, o_mla.reshape(t, -1), o_gdn.reshape(t, -1), o_moba.reshape(t, -1), proj,
                   w_branch, w_out, l, tm=tm)

        x = _ffn(x, ffb_norm[l][None, :], ffb_w_in, ffb_w_out, l,
                 tm=tm, tf=256)
    return x.reshape(b, s, d)
```

```python
import functools

import numpy as np
import jax
import jax.numpy as jnp
from jax import lax
from jax.experimental import pallas as pl
from jax.experimental.pallas import tpu as pltpu

F32 = jnp.float32
BF16 = jnp.bfloat16
MXU_DTYPE = BF16
HIGHEST = lax.Precision.HIGHEST

EPS = 1e-6
ROPE_THETA = 10000.0
NEG_INF = -1e30
SENTINEL = -3e38
LOG2E = 1.4426950408889634

D_MODEL = 1024
D_FF = 2816
MIX_WIDTH = 512
LANES = 128

MLA_HEADS = 8
MLA_Q_RANK = 256
MLA_KV_RANK = 128
MLA_NOPE = 64
MLA_ROPE = 32
MLA_V = 64
MLA_QK = MLA_NOPE + MLA_ROPE

GDN_HEADS = 4
GDN_DK = 128
GDN_DV = 128
GDN_CONV = 4
GDN_CHUNK = 64
GDN_HALO = 8
GDN_GROUP = 8

MOBA_HEADS = 8
MOBA_DH = 64
MOBA_BLOCK = 256
MOBA_TOPK = 3
MOBA_MAX_BLOCKS = 32
V_ROWS = 80

COL_GATE = 0
COL_GDN = 3072
COL_MOBA = 4608
COL_Z = 6144
COL_MLA = 6656
N_PROJ = 7168

VMEM_LIMIT = 56 * 1024 * 1024


def _dot(a, b, precision=None):
    return jnp.dot(a, b, preferred_element_type=F32, precision=precision)


def _dot_nt(a, b, precision=None):
    return lax.dot_general(a, b, (((1,), (1,)), ((), ())), preferred_element_type=F32,
                           precision=precision)


def _dot_tn(a, b, precision=None):
    return lax.dot_general(a, b, (((0,), (0,)), ((), ())), preferred_element_type=F32,
                           precision=precision)


def _rms(x, gain, n):
    ms = jnp.sum(x * x, axis=-1, keepdims=True) * (1.0 / n)
    return x * lax.rsqrt(ms + EPS) * gain


def _silu(x):
    return x * jax.nn.sigmoid(x)


def _softplus(x):
    return jnp.maximum(x, 0.0) + jnp.log1p(jnp.exp(-jnp.abs(x)))


def _params(sem):
    return pltpu.CompilerParams(dimension_semantics=sem, vmem_limit_bytes=VMEM_LIMIT)


def _ffn_body(x_ref, g_ref, wi_ref, wo_ref, o_ref, act_scr, *, tf):
    x = x_ref[...]
    h = _rms(x, g_ref[...], D_MODEL).astype(MXU_DTYPE)
    for k in range(D_FF // tf):
        gate = _dot(h, wi_ref[:, k * tf:(k + 1) * tf].astype(MXU_DTYPE))
        up = _dot(h, wi_ref[:, D_FF + k * tf:D_FF + (k + 1) * tf].astype(MXU_DTYPE))
        act_scr[:, k * tf:(k + 1) * tf] = (_silu(gate) * up).astype(MXU_DTYPE)
    o_ref[...] = x + 0.5 * _dot(act_scr[...], wo_ref[...].astype(MXU_DTYPE))


def _resident(shape, layer=None):
    if layer is None:
        return pl.BlockSpec(shape, lambda *_: (0,) * len(shape), pipeline_mode=pl.Buffered(1))
    return pl.BlockSpec((None,) + tuple(shape), lambda *_: (layer,) + (0,) * len(shape),
                        pipeline_mode=pl.Buffered(1))


def _ffn(x, gain, w_in, w_out, layer, *, tm, tf):
    t, d = x.shape
    return pl.pallas_call(
        functools.partial(_ffn_body, tf=tf),
        grid=(t // tm,),
        in_specs=[
            pl.BlockSpec((tm, d), lambda i: (i, 0)),
            _resident((1, d)),
            _resident((d, 2 * D_FF), layer),
            _resident((D_FF, d), layer),
        ],
        out_specs=pl.BlockSpec((tm, d), lambda i: (i, 0)),
        out_shape=jax.ShapeDtypeStruct((t, d), F32),
        scratch_shapes=[pltpu.VMEM((tm, D_FF), MXU_DTYPE)],
        compiler_params=_params(("parallel",)),
        name="ffn",
    )(x, gain, w_in, w_out)


def _inproj_body(x_ref, g_ref, conv_ref, *refs, widths, tn, conv_group, tiles_per_seq):
    w_refs, wbat_ref = refs[:len(widths)], refs[len(widths)]
    o_ref, tail_ref, bat_ref, halo_scr = refs[len(widths) + 1:]
    tm = x_ref.shape[0]
    seq_start = pl.program_id(0) % tiles_per_seq == 0
    h = _rms(x_ref[...], g_ref[...], D_MODEL).astype(MXU_DTYPE)
    bat_ref[...] = _dot_nt(wbat_ref[...], h)
    off = 0
    for gi, (w_ref, width) in enumerate(zip(w_refs, widths)):
        for a in range(0, width, tn):
            n = min(tn, width - a)
            r = _dot(h, w_ref[:, a:a + n])
            if gi == conv_group:
                cw = conv_ref[:, a:a + n]
                halo = jnp.where(seq_start, 0.0, halo_scr[:, a:a + n])
                halo_scr[:, a:a + n] = r[tm - GDN_HALO:, :]
                xe = jnp.concatenate([halo, r], axis=0)
                y = r * cw[GDN_CONV - 1:GDN_CONV]
                for back in range(1, GDN_CONV):
                    y = y + pltpu.roll(xe, back, 0)[GDN_HALO:] * cw[GDN_CONV - 1 - back:GDN_CONV - back]
                r = _silu(y)
            o_ref[:, off + a:off + a + n] = r.astype(o_ref.dtype)
        off += width
    tail_ref[...] = r


def _inproj(x, gain, conv, w_groups, wbat, *, s, tm, tn, conv_group):
    t, d = x.shape
    widths = tuple(w.shape[1] for w in w_groups)
    assert sum(widths) == N_PROJ and widths[-1] == LANES and widths[conv_group] == conv.shape[1]
    return pl.pallas_call(
        functools.partial(_inproj_body, widths=widths, tn=tn, conv_group=conv_group, tiles_per_seq=s // tm),
        grid=(t // tm,),
        in_specs=[pl.BlockSpec((tm, d), lambda i: (i, 0)), _resident((1, d)), _resident(conv.shape)]
                 + [_resident((d, w)) for w in widths] + [_resident((8, d))],
        out_specs=[
            pl.BlockSpec((tm, N_PROJ), lambda i: (i, 0)),
            pl.BlockSpec((tm, LANES), lambda i: (i, 0)),
            pl.BlockSpec((8, tm), lambda i: (0, i)),
        ],
        out_shape=[jax.ShapeDtypeStruct((t, N_PROJ), MXU_DTYPE), jax.ShapeDtypeStruct((t, LANES), F32),
                   jax.ShapeDtypeStruct((8, t), F32)],
        scratch_shapes=[pltpu.VMEM((GDN_HALO, conv.shape[1]), F32)],
        compiler_params=_params(("arbitrary",)),
        name="inproj",
    )(x, gain, conv, *w_groups, wbat)


def _mla_rot(x, lane):
    return jnp.where(lane < MLA_NOPE + MLA_ROPE // 2, -pltpu.roll(x, LANES - MLA_ROPE // 2, 1),
                     pltpu.roll(x, MLA_ROPE // 2, 1))


def _lane_tile(x, n):
    return x if n == 1 else jnp.concatenate([x] * n, axis=1)


def _value_rows(v_ref, h, vt_h, cols=slice(None)):
    n = vt_h.shape[1]
    v_ref[0, h, 0:MLA_V, cols] = vt_h.astype(v_ref.dtype)
    pad_rows = lax.broadcasted_iota(jnp.int32, (V_ROWS - MLA_V, n), 0)
    v_ref[0, h, MLA_V:V_ROWS, cols] = jnp.where(pad_rows == 0, 1.0, 0.0).astype(v_ref.dtype)


def _mla_prep_body(lat_ref, tail_ref, cqg_ref, ckvg_ref, wqt_ref, wk_ref, wvt_ref, qgt_ref, kg_ref, cost_ref,
                   sint_ref, cos_ref, sin_ref, q_ref, k_ref, v_ref, *, scale):
    lat = lat_ref[...].astype(F32)
    tm = lat.shape[0]
    cqn = _rms(lat[:, :MLA_Q_RANK], cqg_ref[...], MLA_Q_RANK).astype(MXU_DTYPE)
    ckvn = _rms(lat[:, MLA_Q_RANK:MLA_Q_RANK + MLA_KV_RANK], ckvg_ref[...], MLA_KV_RANK).astype(MXU_DTYPE)

    qt_all = _dot_nt(wqt_ref[...], cqn)
    qgt = _lane_tile(qgt_ref[...], tm // LANES)
    cost = cost_ref[...]
    sint = sint_ref[...]
    half = MLA_ROPE // 2
    for h in range(MLA_HEADS):
        x = qt_all[h * LANES:(h + 1) * LANES, :]
        ms = jnp.sum(x * x, axis=0, keepdims=True) * (1.0 / MLA_QK)
        qn = x * lax.rsqrt(ms + EPS) * qgt
        rot = jnp.concatenate([jnp.zeros((MLA_NOPE, tm), F32), -qn[MLA_NOPE + half:MLA_QK],
                               qn[MLA_NOPE:MLA_NOPE + half], jnp.zeros((LANES - MLA_QK, tm), F32)], axis=0)
        q_ref[0, h] = ((qn * cost + rot * sint) * scale).astype(q_ref.dtype)

    lane = lax.broadcasted_iota(jnp.int32, (tm, LANES), 1)
    kr = jnp.where((lane >= MLA_NOPE) & (lane < MLA_QK), tail_ref[...], 0.0)
    kg = kg_ref[...]
    cos = cos_ref[...]
    rot_kr_sin = _mla_rot(kr * kg, lane) * sin_ref[...]
    k_all = _dot(ckvn, wk_ref[...])
    for h in range(MLA_HEADS):
        kh = k_all[:, h * LANES:(h + 1) * LANES] + kr
        r = lax.rsqrt(jnp.sum(kh * kh, axis=-1, keepdims=True) * (1.0 / MLA_QK) + EPS)
        k_ref[0, h] = (r * (kh * kg * cos + rot_kr_sin)).astype(k_ref.dtype)

    vt = _dot_nt(wvt_ref[...], ckvn)
    for h in range(MLA_HEADS):
        _value_rows(v_ref, h, vt[h * MLA_V:(h + 1) * MLA_V, :])


def _mla_prep(proj, tail, cqg, ckvg, wqt, wk, wvt, qgt, kg, cost, sint, cos, sin, *, b, s, tm):
    nt = s // tm
    scale = (MLA_QK ** -0.5) * LOG2E
    cblk = COL_MLA // 512
    full = lambda shape: pl.BlockSpec(shape, lambda bi, i: (0,) * len(shape))
    return pl.pallas_call(
        functools.partial(_mla_prep_body, scale=scale),
        grid=(b, nt),
        in_specs=[
            pl.BlockSpec((tm, 512), lambda bi, i: (bi * nt + i, cblk)),
            pl.BlockSpec((tm, LANES), lambda bi, i: (bi * nt + i, 0)),
            full((1, MLA_Q_RANK)), full((1, MLA_KV_RANK)),
            full((MLA_HEADS * LANES, MLA_Q_RANK)), full((MLA_KV_RANK, MLA_HEADS * LANES)),
            full((MIX_WIDTH, MLA_KV_RANK)), full((LANES, LANES)), full((1, LANES)),
            pl.BlockSpec((LANES, tm), lambda bi, i: (0, i)),
            pl.BlockSpec((LANES, tm), lambda bi, i: (0, i)),
            pl.BlockSpec((tm, LANES), lambda bi, i: (i, 0)),
            pl.BlockSpec((tm, LANES), lambda bi, i: (i, 0)),
        ],
        out_specs=[
            pl.BlockSpec((1, MLA_HEADS, LANES, tm), lambda bi, i: (bi, 0, 0, i)),
            pl.BlockSpec((1, MLA_HEADS, tm, LANES), lambda bi, i: (bi, 0, i, 0)),
            pl.BlockSpec((1, MLA_HEADS, V_ROWS, tm), lambda bi, i: (bi, 0, 0, i)),
        ],
        out_shape=[
            jax.ShapeDtypeStruct((b, MLA_HEADS, LANES, s), MXU_DTYPE),
            jax.ShapeDtypeStruct((b, MLA_HEADS, s, LANES), MXU_DTYPE),
            jax.ShapeDtypeStruct((b, MLA_HEADS, V_ROWS, s), MXU_DTYPE),
        ],
        compiler_params=_params(("parallel", "parallel")),
        name="mla_prep",
    )(proj, tail, cqg, ckvg, wqt, wk, wvt, qgt, kg, cost, sint, cos, sin)


def _moba_rot(x, lane):
    half = MOBA_DH // 2
    return jnp.where(lane % MOBA_DH < half, -pltpu.roll(x, LANES - half, 1), pltpu.roll(x, half, 1))


def _pair_rms(x, gain, lo_mask):
    sq = x * x
    ss_lo = jnp.sum(jnp.where(lo_mask, sq, 0.0), axis=-1, keepdims=True)
    ss_hi = jnp.sum(jnp.where(lo_mask, 0.0, sq), axis=-1, keepdims=True)
    r = jnp.where(lo_mask, lax.rsqrt(ss_lo * (1.0 / MOBA_DH) + EPS), lax.rsqrt(ss_hi * (1.0 / MOBA_DH) + EPS))
    return x * r * gain


def _select_bias(gate, blk, blk_f, j):
    g = jnp.where(blk < j, gate, SENTINEL)
    allowed = blk == j
    for _ in range(MOBA_TOPK):
        m = jnp.max(g, axis=0, keepdims=True)
        first = jnp.min(jnp.where(g == m, blk_f, 1e9), axis=0, keepdims=True)
        pick = (blk_f == first) & (m > SENTINEL)
        allowed = allowed | pick
        g = jnp.where(pick, SENTINEL, g)
    return jnp.where(allowed, 0.0, NEG_INF)


def _moba_prep_body(qkv_ref, qgt_ref, kg_ref, cost_ref, sint_ref, cos_ref, sin_ref, q_ref, k_ref, v_ref, kmt_scr,
                    *, scale, nsub):
    @pl.when(pl.program_id(1) == 0)
    def _():
        kmt_scr[...] = jnp.zeros_like(kmt_scr)

    tq = MOBA_BLOCK
    hw = MOBA_HEADS * MOBA_DH
    dh, half, nblk = MOBA_DH, MOBA_DH // 2, MOBA_MAX_BLOCKS
    qgt = _lane_tile(qgt_ref[...], tq // LANES)
    blk = lax.broadcasted_iota(jnp.int32, (nblk, tq), 0)
    blk_f = blk.astype(F32)
    zeros = jnp.zeros((nblk, tq), F32)
    lane = lax.broadcasted_iota(jnp.int32, (tq, LANES), 1)
    lo_mask = lane < dh
    row_lane = lax.broadcasted_iota(jnp.int32, (1, LANES), 1)
    slot = lax.broadcasted_iota(jnp.int32, (LANES, LANES), 0)

    for sb in range(nsub):
        j = pl.program_id(1) * nsub + sb
        rs = slice(sb * tq, (sb + 1) * tq)

        vt = qkv_ref[rs, 2 * hw:].astype(F32).T
        for h in range(MOBA_HEADS):
            _value_rows(v_ref, h, vt[h * dh:(h + 1) * dh, :], rs)

        qt_all = qkv_ref[rs, :hw].astype(F32).T
        cost = cost_ref[:, rs]
        sint = sint_ref[:, rs]
        cos = cos_ref[rs, :]
        sin = sin_ref[rs, :]
        for p in range(MOBA_HEADS // 2):
            x = qt_all[p * LANES:(p + 1) * LANES, :]
            sq = x * x
            r_e = lax.rsqrt(jnp.sum(sq[:dh], axis=0, keepdims=True) * (1.0 / dh) + EPS)
            r_o = lax.rsqrt(jnp.sum(sq[dh:], axis=0, keepdims=True) * (1.0 / dh) + EPS)
            qn = jnp.concatenate([x[:dh] * r_e, x[dh:] * r_o], axis=0) * qgt
            rot = jnp.concatenate([-qn[half:dh], qn[:half], -qn[dh + half:], qn[dh:dh + half]], axis=0)
            q = qn * cost + rot * sint
            km_hi, km_lo = _hi_lo(kmt_scr[p])
            q_hi, q_lo = _hi_lo(q)
            gate = _dot(km_hi, q_hi) + (_dot(km_hi, q_lo) + _dot(km_lo, q_hi))
            bias_e = _select_bias(gate[dh:dh + nblk], blk, blk_f, j)
            bias_o = _select_bias(gate[:nblk], blk, blk_f, j)
            qs = q * scale
            q_ref[0, 2 * p, :, rs] = jnp.concatenate([qs[:dh], bias_e, zeros], axis=0).astype(q_ref.dtype)
            q_ref[0, 2 * p + 1, :, rs] = jnp.concatenate([bias_o, zeros, qs[dh:]], axis=0).astype(q_ref.dtype)

            kn = _pair_rms(qkv_ref[rs, hw + p * LANES:hw + (p + 1) * LANES].astype(F32), kg_ref[...], lo_mask)
            k = kn * cos + _moba_rot(kn, lane) * sin
            k_ref[0, 2 * p, rs, :] = jnp.where(lo_mask, k, jnp.where(lane == dh + j, 1.0, 0.0)).astype(k_ref.dtype)
            k_ref[0, 2 * p + 1, rs, :] = jnp.where(lo_mask, jnp.where(lane == j, 1.0, 0.0), k).astype(k_ref.dtype)
            km = jnp.sum(k, axis=0, keepdims=True) * (1.0 / MOBA_BLOCK)
            new_rows = jnp.where(slot == dh + j, jnp.where(row_lane < dh, km, 0.0),
                                 jnp.where(row_lane < dh, 0.0, km))
            kmt_scr[p] = jnp.where((slot == dh + j) | (slot == j), new_rows, kmt_scr[p])


def _moba_prep(proj, qgt, kg, cost, sint, cos, sin, *, b, s, nsub):
    tq = MOBA_BLOCK * nsub
    nt = s // tq
    scale = (MOBA_DH ** -0.5) * LOG2E
    cblk = COL_MOBA // (3 * MOBA_HEADS * MOBA_DH)
    return pl.pallas_call(
        functools.partial(_moba_prep_body, scale=scale, nsub=nsub),
        grid=(b, nt),
        in_specs=[
            pl.BlockSpec((tq, 3 * MOBA_HEADS * MOBA_DH), lambda bi, j: (bi * nt + j, cblk)),
            pl.BlockSpec((LANES, LANES), lambda bi, j: (0, 0)),
            pl.BlockSpec((1, LANES), lambda bi, j: (0, 0)),
            pl.BlockSpec((LANES, tq), lambda bi, j: (0, j)),
            pl.BlockSpec((LANES, tq), lambda bi, j: (0, j)),
            pl.BlockSpec((tq, LANES), lambda bi, j: (j, 0)),
            pl.BlockSpec((tq, LANES), lambda bi, j: (j, 0)),
        ],
        out_specs=[
            pl.BlockSpec((1, MOBA_HEADS, LANES, tq), lambda bi, j: (bi, 0, 0, j)),
            pl.BlockSpec((1, MOBA_HEADS, tq, LANES), lambda bi, j: (bi, 0, j, 0)),
            pl.BlockSpec((1, MOBA_HEADS, V_ROWS, tq), lambda bi, j: (bi, 0, 0, j)),
        ],
        out_shape=[
            jax.ShapeDtypeStruct((b, MOBA_HEADS, LANES, s), MXU_DTYPE),
            jax.ShapeDtypeStruct((b, MOBA_HEADS, s, LANES), MXU_DTYPE),
            jax.ShapeDtypeStruct((b, MOBA_HEADS, V_ROWS, s), MXU_DTYPE),
        ],
        scratch_shapes=[pltpu.VMEM((MOBA_HEADS // 2, LANES, LANES), F32)],
        compiler_params=_params(("parallel", "arbitrary")),
        name="moba_prep",
    )(proj, qgt, kg, cost, sint, cos, sin)


def _flash_body(qt_ref, k_ref, vt_ref, o_ref, diag, ev0, ev1, od0, od1, *, tq, n_tiles):
    def q_tile(i):
        qs = pl.ds(pl.multiple_of(i * tq, tq), tq)
        return (qt_ref[0, 0, :, qs], qt_ref[0, 1, :, qs])

    def scores(qts, t, buf, hh, diagonal=False):
        ks = pl.ds(pl.multiple_of(t * tq, tq), tq)
        s = _dot(k_ref[0, hh, ks, :], qts[hh])
        if diagonal:
            key = lax.broadcasted_iota(jnp.int32, s.shape, 0)
            qry = lax.broadcasted_iota(jnp.int32, s.shape, 1)
            s = jnp.where(key <= qry, s, NEG_INF)
        buf[hh] = s
        return jnp.max(s, axis=0, keepdims=True)

    def diag_scores(i, buf):
        qts = q_tile(i)
        return tuple(scores(qts, i, buf, hh, True) for hh in range(2))

    def consume(t, buf, hh, smax, stat):
        ks = pl.ds(pl.multiple_of(t * tq, tq), tq)
        m, acc = stat
        m_new = jnp.maximum(m, smax)
        p = jnp.exp2(buf[hh] - m_new)
        return m_new, jnp.exp2(m - m_new) * acc + _dot(vt_ref[0, hh, :, ks], p.astype(MXU_DTYPE))

    def step(qts, t_next, buf_next, t_cur, buf_cur, carry):
        smax, stats = carry
        new_smax, new_stats = [], []
        for hh in range(2):
            new_smax.append(scores(qts, t_next, buf_next, hh))
            new_stats.append(consume(t_cur, buf_cur, hh, smax[hh], stats[hh]))
        return tuple(new_smax), tuple(new_stats)

    def run(qts, i, t_first, carry, bufs):
        n_steps = i - t_first

        def quad(u, carry):
            t0 = t_first + 4 * u
            carry = step(qts, t0, bufs[0], t0 - 1, bufs[1], carry)
            carry = step(qts, t0 + 1, bufs[1], t0, bufs[0], carry)
            carry = step(qts, t0 + 2, bufs[0], t0 + 1, bufs[1], carry)
            return step(qts, t0 + 3, bufs[1], t0 + 2, bufs[0], carry)

        carry = lax.fori_loop(0, n_steps // 4, quad, carry)
        t_pair = t_first + 4 * (n_steps // 4)

        def pair(u, carry):
            carry = step(qts, t_pair, bufs[0], t_pair - 1, bufs[1], carry)
            return step(qts, t_pair + 1, bufs[1], t_pair, bufs[0], carry)

        return lax.fori_loop(0, (n_steps % 4) // 2, pair, carry)

    def finish(i, t_last, buf, carry):
        smax, stats = carry
        out = [consume(t_last, buf, hh, smax[hh], stats[hh])[1] for hh in range(2)]
        o_t = jnp.concatenate([a[:MLA_V] / a[MLA_V:MLA_V + 1] for a in out], axis=0)
        o_ref[0, pl.ds(pl.multiple_of(i * tq, tq), tq), :] = o_t.T.astype(o_ref.dtype)

    init = ((jnp.full((1, tq), NEG_INF, F32), jnp.zeros((V_ROWS, tq), F32)),) * 2

    smax0 = diag_scores(0, diag)
    smax1 = diag_scores(1, ev0)
    finish(0, 0, diag, (smax0, init))
    carry = step(q_tile(1), 0, ev1, 1, ev0, (smax1, init))
    smax2 = diag_scores(2, diag)
    finish(1, 0, ev1, carry)

    def tile_pair(j, smax_even):
        i = 2 * j
        qts = q_tile(i)
        carry = step(qts, 0, ev0, i, diag, (smax_even, init))
        carry = step(qts, 1, ev1, 0, ev0, carry)
        carry = run(qts, i, 2, carry, (ev0, ev1))
        smax_odd = diag_scores(i + 1, diag)
        finish(i, i - 1, ev1, carry)

        i = 2 * j + 1
        qts = q_tile(i)
        carry = step(qts, 0, od0, i, diag, (smax_odd, init))
        carry = step(qts, 1, od1, 0, od0, carry)
        carry = step(qts, 2, od0, 1, od1, carry)
        carry = run(qts, i, 3, carry, (od1, od0))
        smax_next = diag_scores(jnp.minimum(i + 1, n_tiles - 1), diag)
        finish(i, i - 1, od0, carry)
        return smax_next

    lax.fori_loop(1, n_tiles // 2, tile_pair, smax2)


def _flash(qt, k, vt, *, tq):
    b, h, s, _ = k.shape
    n_tiles = s // tq
    assert n_tiles % 2 == 0 and n_tiles >= 4
    buf = pltpu.VMEM((2, tq, tq), F32)
    return pl.pallas_call(
        functools.partial(_flash_body, tq=tq, n_tiles=n_tiles),
        grid=(b, h // 2),
        in_specs=[
            pl.BlockSpec((1, 2, LANES, s), lambda bi, p: (bi, p, 0, 0)),
            pl.BlockSpec((1, 2, s, LANES), lambda bi, p: (bi, p, 0, 0)),
            pl.BlockSpec((1, 2, V_ROWS, s), lambda bi, p: (bi, p, 0, 0)),
        ],
        out_specs=pl.BlockSpec((1, s, LANES), lambda bi, p: (bi, 0, p)),
        out_shape=jax.ShapeDtypeStruct((b, s, MIX_WIDTH), MXU_DTYPE),
        scratch_shapes=[buf] * 5,
        compiler_params=_params(("parallel", "parallel")),
        name="flash",
    )(qt, k, vt)


def _hi_lo(x):
    hi = x.astype(MXU_DTYPE)
    return hi, (x - hi.astype(F32)).astype(MXU_DTYPE)


def _unit_lower_solve(a_list, rhs_list, eye):
    c = eye.shape[0]
    n = range(len(a_list))
    t = [eye - a for a in a_list]
    pk = []
    for a in a_list:
        pwm = (-a).astype(MXU_DTYPE)
        pk.append(_dot(pwm, pwm))
    n_sq = int(np.log2(c)) - 1
    for it in range(n_sq):
        if it < n_sq - 1:
            both = [_dot(jnp.concatenate([t[j], pk[j]], axis=0).astype(MXU_DTYPE), pk[j].astype(MXU_DTYPE))
                    for j in n]
            t = [t[j] + both[j][:c] for j in n]
            pk = [both[j][c:] for j in n]
        else:
            t = [t[j] + _dot(t[j].astype(MXU_DTYPE), pk[j].astype(MXU_DTYPE)) for j in n]
    tm = [x.astype(MXU_DTYPE) for x in t]
    x1 = [_dot(tm[j], rhs_list[j].astype(MXU_DTYPE)) for j in n]
    a_split = [_hi_lo(a) for a in a_list]
    x_split = [_hi_lo(x) for x in x1]
    ax = [_dot(jnp.concatenate(a_split[j], axis=0), x_split[j][0]) for j in n]
    ax_lo = [_dot(a_split[j][0], x_split[j][1]) for j in n]
    res = [rhs_list[j] - x1[j] - (ax[j][:c] + ax[j][c:] + ax_lo[j]) for j in n]
    return [x1[j] + _dot(tm[j], res[j].astype(MXU_DTYPE)) for j in n]


def _gdn_prep_body(qkv_ref, ba_ref, bat_ref, alog_ref, dtb_ref, alogt_ref, dtbt_ref,
                   u_ref, w_ref, qd_ref, kd_ref, qk_ref, egl_ref, bg_scr, *, nc):
    c = GDN_CHUNK
    hk = GDN_HEADS * GDN_DK

    ba = ba_ref[...]
    lane = lax.broadcasted_iota(jnp.int32, ba.shape, 1)
    g_col = -jnp.exp(alog_ref[...]) * _softplus(ba + dtb_ref[...])
    bg_scr[...] = jnp.where(lane < GDN_HEADS, jax.nn.sigmoid(ba), g_col)

    ri = lax.broadcasted_iota(jnp.int32, (c, c), 0)
    ci = lax.broadcasted_iota(jnp.int32, (c, c), 1)
    tril = ci <= ri
    strict = ci < ri
    eye = jnp.where(ci == ri, 1.0, 0.0)
    lower_ones = jnp.where(tril, 1.0, 0.0)
    upper_ones = jnp.where(ci >= ri, 1.0, 0.0)

    def chunk_group(grp, _):
        chs = [grp * GDN_GROUP + j for j in range(GDN_GROUP)]
        rows = [pl.ds(pl.multiple_of(ch * c, c), c) for ch in chs]
        bg = [bg_scr[r, :] for r in rows]
        gcum_col = [_dot(lower_ones, x, precision=HIGHEST) for x in bg]
        g_row = [-jnp.exp(alogt_ref[...]) * _softplus(bat_ref[ch] + dtbt_ref[...]) for ch in chs]
        gcum_row = [_dot(x, upper_ones, precision=HIGHEST) for x in g_row]
        probs = [(j, h) for j in range(GDN_GROUP) for h in range(GDN_HEADS)]
        n = range(len(probs))
        sls = [slice(h * GDN_DK, (h + 1) * GDN_DK) for _, h in probs]
        qh = [qkv_ref[rows[j], sls[i]].astype(F32) for i, (j, h) in enumerate(probs)]
        kh = [qkv_ref[rows[j], pl.ds(hk + h * GDN_DK, GDN_DK)].astype(F32) for j, h in probs]
        vh = [qkv_ref[rows[j], pl.ds(2 * hk + h * GDN_DV, GDN_DV)].astype(F32) for j, h in probs]
        qn = [x * lax.rsqrt(jnp.sum(x * x, axis=-1, keepdims=True) + EPS) * (GDN_DK ** -0.5) for x in qh]
        kn = [x * lax.rsqrt(jnp.sum(x * x, axis=-1, keepdims=True) + EPS) for x in kh]
        beta = [bg[j][:, h:h + 1] for j, h in probs]
        gc = [gcum_col[j][:, GDN_HEADS + h:GDN_HEADS + h + 1] for j, h in probs]
        gr = [gcum_row[j][GDN_HEADS + h:GDN_HEADS + h + 1, :] for j, h in probs]
        kb = [kn[i] * beta[i] for i in n]
        knm = [x.astype(MXU_DTYPE) for x in kn]
        dec = [jnp.exp(jnp.where(tril, gc[i] - gr[i], NEG_INF)) for i in n]
        skk = [_dot_nt(kb[i].astype(MXU_DTYPE), knm[i]) for i in n]
        sqk = [_dot_nt(qn[i].astype(MXU_DTYPE), knm[i]) for i in n]
        a = [jnp.where(strict, skk[i] * dec[i], 0.0) for i in n]
        eg = [jnp.exp(x) for x in gc]
        rhs = [jnp.concatenate([vh[i] * beta[i], kb[i] * eg[i]], axis=1) for i in n]
        sol = _unit_lower_solve(a, rhs, eye)
        for i, (j, h) in enumerate(probs):
            sl, r = sls[i], rows[j]
            gl = gc[i][c - 1:c, :]
            u_ref[r, sl] = sol[i][:, :GDN_DV]
            w_ref[r, sl] = sol[i][:, GDN_DV:].astype(w_ref.dtype)
            qd_ref[r, sl] = (qn[i] * eg[i]).astype(qd_ref.dtype)
            kd_ref[r, sl] = (kn[i] * jnp.exp(gl - gc[i])).astype(kd_ref.dtype)
            qk_ref[r, pl.ds(h * LANES, c)] = (sqk[i] * dec[i]).astype(qk_ref.dtype)
            qk_ref[r, pl.ds(h * LANES + c, LANES - c)] = jnp.zeros((c, LANES - c), qk_ref.dtype)
            egl_ref[chs[j], :, sl] = jnp.broadcast_to(jnp.exp(gl), (1, GDN_DV))
        return 0

    lax.fori_loop(0, nc // GDN_GROUP, chunk_group, 0)


def _gdn_prep(proj, tail, bat, alog, dtb, alogt, dtbt, *, tm):
    t = proj.shape[0]
    nc = tm // GDN_CHUNK
    hk = GDN_HEADS * GDN_DK
    width = 3 * hk
    cblk = COL_GDN // width
    full = lambda shape: pl.BlockSpec(shape, lambda i: (0,) * len(shape))
    row = lambda w: pl.BlockSpec((tm, w), lambda i: (i, 0))
    return pl.pallas_call(
        functools.partial(_gdn_prep_body, nc=nc),
        grid=(t // tm,),
        in_specs=[
            pl.BlockSpec((tm, width), lambda i: (i, cblk)),
            pl.BlockSpec((tm, LANES), lambda i: (i, 0)),
            pl.BlockSpec((nc, 8, GDN_CHUNK), lambda i: (i, 0, 0)),
            full((1, LANES)), full((1, LANES)), full((8, 1)), full((8, 1)),
        ],
        out_specs=[row(hk), row(hk), row(hk), row(hk), row(GDN_HEADS * LANES),
                   pl.BlockSpec((nc, 1, hk), lambda i: (i, 0, 0))],
        out_shape=[
            jax.ShapeDtypeStruct((t, hk), F32),
            jax.ShapeDtypeStruct((t, hk), MXU_DTYPE),
            jax.ShapeDtypeStruct((t, hk), MXU_DTYPE),
            jax.ShapeDtypeStruct((t, hk), MXU_DTYPE),
            jax.ShapeDtypeStruct((t, GDN_HEADS * LANES), MXU_DTYPE),
            jax.ShapeDtypeStruct((t // GDN_CHUNK, 1, hk), F32),
        ],
        scratch_shapes=[pltpu.VMEM((tm, LANES), F32)],
        compiler_params=_params(("parallel",)),
        name="gdn_prep",
    )(proj, tail, bat, alog, dtb, alogt, dtbt)


def _gdn_scan_body(u_ref, w_ref, qd_ref, kd_ref, qk_ref, egl_ref, z_ref, og_ref, o_ref, s_scr, *, nb, nc):
    @pl.when(pl.program_id(0) == 0)
    def _():
        s_scr[...] = jnp.zeros_like(s_scr)

    c = GDN_CHUNK
    og = og_ref[...]

    def chunk(ch, _):
        rows = pl.ds(pl.multiple_of(ch * c, c), c)
        probs = [(bi, h) for bi in range(nb) for h in range(GDN_HEADS)]
        n = range(len(probs))
        sls = [slice(h * GDN_DV, (h + 1) * GDN_DV) for _, h in probs]
        st = [s_scr[bi * GDN_HEADS + h] for bi, h in probs]
        stm = [x.astype(MXU_DTYPE) for x in st]
        wq = [jnp.concatenate([w_ref[bi, rows, sls[i]], qd_ref[bi, rows, sls[i]]], axis=0)
              for i, (bi, h) in enumerate(probs)]
        r1 = [_dot(wq[i], stm[i]) for i in n]
        vm = [(u_ref[bi, rows, sls[i]] - r1[i][:c]).astype(MXU_DTYPE) for i, (bi, h) in enumerate(probs)]
        o2 = [_dot(qk_ref[bi, rows, pl.ds(h * LANES, c)], vm[i]) for i, (bi, h) in enumerate(probs)]
        sd = [_dot_tn(kd_ref[bi, rows, sls[i]], vm[i]) for i, (bi, h) in enumerate(probs)]
        for i, (bi, h) in enumerate(probs):
            s_scr[bi * GDN_HEADS + h] = st[i] * egl_ref[bi, ch, :, sls[i]] + sd[i]
            o = r1[i][c:] + o2[i]
            o_ref[bi, rows, sls[i]] = (_rms(o, og, GDN_DV)
                                       * _silu(z_ref[bi, rows, sls[i]].astype(F32))).astype(o_ref.dtype)
        return 0

    lax.fori_loop(0, nc, chunk, 0, unroll=4)


def _gdn_scan(u, w, qd, kd, qk, egl, proj3, og, *, tt):
    b, s, hk = u.shape
    nc = tt // GDN_CHUNK
    blk = lambda width: pl.BlockSpec((b, tt, width), lambda i: (0, i, 0))
    return pl.pallas_call(
        functools.partial(_gdn_scan_body, nb=b, nc=nc),
        grid=(s // tt,),
        in_specs=[
            blk(hk), blk(hk), blk(hk), blk(hk), blk(GDN_HEADS * LANES),
            pl.BlockSpec((b, nc, 1, hk), lambda i: (0, i, 0, 0)),
            pl.BlockSpec((b, tt, hk), lambda i: (0, i, COL_Z // hk)),
            pl.BlockSpec((1, GDN_DV), lambda i: (0, 0)),
        ],
        out_specs=blk(hk),
        out_shape=jax.ShapeDtypeStruct((b, s, hk), MXU_DTYPE),
        scratch_shapes=[pltpu.VMEM((b * GDN_HEADS, GDN_DK, GDN_DV), F32)],
        compiler_params=_params(("arbitrary",)),
        name="gdn_scan",
    )(u, w, qd, kd, qk, egl, proj3, og)


def _merge_body(x_ref, om_ref, og_ref, ob_ref, gl_ref, wb_ref, wo_ref, o_ref):
    mixed = None
    for n, br in enumerate((om_ref, og_ref, ob_ref)):
        up = _dot(br[...], wb_ref[n].astype(MXU_DTYPE))
        term = jax.nn.sigmoid(gl_ref[:, n * D_MODEL:(n + 1) * D_MODEL].astype(F32)) * up
        mixed = term if mixed is None else mixed + term
    o_ref[...] = x_ref[...] + _dot(mixed.astype(MXU_DTYPE), wo_ref[...].astype(MXU_DTYPE))


def _merge(x, o_mla, o_gdn, o_moba, proj, wb, wo, layer, *, tm):
    t, d = x.shape
    row = lambda w: pl.BlockSpec((tm, w), lambda i: (i, 0))
    return pl.pallas_call(
        _merge_body,
        grid=(t // tm,),
        in_specs=[
            row(d), row(MIX_WIDTH), row(MIX_WIDTH), row(MIX_WIDTH), row(3 * d),
            _resident((3, MIX_WIDTH, d), layer),
            _resident((d, d), layer),
        ],
        out_specs=row(d),
        out_shape=jax.ShapeDtypeStruct((t, d), F32),
        compiler_params=_params(("parallel",)),
        name="merge",
    )(x, o_mla, o_gdn, o_moba, proj, wb, wo)


def _rope_tables(s):
    pos = np.arange(s, dtype=np.float64)[:, None]

    def cs(d):
        half = d // 2
        inv_freq = ROPE_THETA ** (-np.arange(half, dtype=np.float64) * 2.0 / d)
        ang = pos * inv_freq[None, :]
        return jnp.asarray(np.cos(ang), F32), jnp.asarray(np.sin(ang), F32)

    def mla(c, sn, axis):
        n = lambda k: (s, k) if axis == 1 else (k, s)
        pad = LANES - MLA_QK
        return (jnp.concatenate([jnp.ones(n(MLA_NOPE), F32), c, c, jnp.zeros(n(pad), F32)], axis=axis),
                jnp.concatenate([jnp.zeros(n(MLA_NOPE), F32), sn, sn, jnp.zeros(n(pad), F32)], axis=axis))

    c, sn = cs(MLA_ROPE)
    mla_cos, mla_sin = mla(c, sn, 1)
    mla_cost, mla_sint = mla(c.T, sn.T, 0)
    c, sn = cs(MOBA_DH)
    moba_cos, moba_sin = jnp.tile(c, (1, 4)), jnp.tile(sn, (1, 4))
    moba_cost, moba_sint = jnp.tile(c.T, (4, 1)), jnp.tile(sn.T, (4, 1))
    return mla_cos, mla_sin, mla_cost, mla_sint, moba_cos, moba_sin, moba_cost, moba_sint


def _proj_weights(w_in):
    d = w_in.shape[0]
    o = np.cumsum((0, MLA_Q_RANK, MLA_KV_RANK, MLA_ROPE, 512, 512, 512, GDN_HEADS, GDN_HEADS, 512, 1536, 3 * D_MODEL))
    cut = lambda a, b: w_in[:, o[a]:o[b]].astype(MXU_DTYPE)
    ba = cut(6, 8)
    tail = jnp.zeros((d, LANES), MXU_DTYPE).at[:, :2 * GDN_HEADS].set(ba).at[:, MLA_NOPE:MLA_QK].set(cut(2, 3))
    groups = [cut(10, 11), cut(3, 6), cut(9, 10), cut(8, 9), cut(0, 1), cut(1, 2), tail]
    return groups, ba.T


def _lane_pad(v, n=LANES):
    return jnp.pad(v, (0, n - v.shape[0]))[None, :]


def kernel(x, ffa_norm, ffa_w_in, ffa_w_out, mix_norm, w_in, mla_cq_norm, mla_ckv_norm, mla_w_uq, mla_w_ukv, mla_q_norm, mla_k_norm, gdn_conv, gdn_a_log, gdn_dt_bias, gdn_out_norm, moba_q_norm, moba_k_norm, w_branch, w_out, ffb_norm, ffb_w_in, ffb_w_out):
    b, s, d = x.shape
    t = b * s
    assert d == D_MODEL and s % 2048 == 0 and s // MOBA_BLOCK <= MOBA_MAX_BLOCKS
    depth = ffa_norm.shape[0]
    tm = 512
    mla_cos, mla_sin, mla_cost, mla_sint, moba_cos, moba_sin, moba_cost, moba_sint = _rope_tables(s)
    lane_bcast = lambda v: jnp.broadcast_to(v[:, None], (LANES, LANES))
    x = x.reshape(t, d)
    for l in range(depth):
        x = _ffn(x, ffa_norm[l][None, :], ffa_w_in, ffa_w_out, l,
                 tm=tm, tf=256)

        w_proj, w_bat = _proj_weights(w_in[l])
        proj, tail, bat = _inproj(x, mix_norm[l][None, :], gdn_conv[l], w_proj, w_bat, s=s, tm=tm, tn=512,
                                  conv_group=1)

        wq = jnp.pad(mla_w_uq[l].reshape(MLA_Q_RANK, MLA_HEADS, MLA_QK),
                     ((0, 0), (0, 0), (0, LANES - MLA_QK))).reshape(MLA_Q_RANK, MLA_HEADS * LANES)
        wkv = mla_w_ukv[l].reshape(MLA_KV_RANK, MLA_HEADS, MLA_NOPE + MLA_V)
        wk = jnp.pad(wkv[:, :, :MLA_NOPE], ((0, 0), (0, 0), (0, LANES - MLA_NOPE))).reshape(MLA_KV_RANK, MLA_HEADS * LANES)
        wv = wkv[:, :, MLA_NOPE:].reshape(MLA_KV_RANK, MLA_HEADS * MLA_V).T
        q, k, v = _mla_prep(proj, tail, mla_cq_norm[l][None, :], mla_ckv_norm[l][None, :], wq.T.astype(MXU_DTYPE),
                            wk.astype(MXU_DTYPE), wv.astype(MXU_DTYPE), lane_bcast(_lane_pad(mla_q_norm[l])[0]),
                            _lane_pad(mla_k_norm[l]), mla_cost, mla_sint, mla_cos, mla_sin, b=b, s=s, tm=2 * tm)
        o_mla = _flash(q, k, v, tq=512)

        bat_chunks = bat.reshape(8, t // GDN_CHUNK, GDN_CHUNK).transpose(1, 0, 2)
        head_pad = lambda v: jnp.pad(v, (GDN_HEADS, LANES - 2 * GDN_HEADS))[None, :]
        head_col = lambda v: jnp.pad(v, (GDN_HEADS, 0))[:, None]
        u, w, qd, kd, qk, egl = _gdn_prep(proj, tail, bat_chunks, head_pad(gdn_a_log[l]), head_pad(gdn_dt_bias[l]),
                                          head_col(gdn_a_log[l]), head_col(gdn_dt_bias[l]), tm=tm)
        r3 = lambda a: a.reshape(b, s, a.shape[-1])
        o_gdn = _gdn_scan(r3(u), r3(w), r3(qd), r3(kd), r3(qk), egl.reshape(b, s // GDN_CHUNK, 1, -1),
                          r3(proj), gdn_out_norm[l][None, :], tt=tm)

        q, k, v = _moba_prep(proj, lane_bcast(jnp.tile(moba_q_norm[l], 2)), jnp.tile(moba_k_norm[l], 2)[None, :],
                             moba_cost, moba_sint, moba_cos, moba_sin, b=b, s=s, nsub=4)
        o_moba = _flash(q, k, v, tq=512)

        x = _merge(x, o_mla.reshape(t, -1), o_gdn.reshape(t, -1), o_moba.reshape(t, -1), proj,
                   w_branch, w_out, l, tm=tm)

        x = _ffn(x, ffb_norm[l][None, :], ffb_w_in, ffb_w_out, l,
                 tm=tm, tf=256)
    return x.reshape(b, s, d)
```
